```python
import math
import jax, jax.numpy as jnp
from jax import lax
import numpy as np

D_MODEL = 4096
BATCH = 2
SEQ = 8192
DEPTH = 2

N_BRANCH = 4
BRANCH_W = D_MODEL // 4
DA_HEADS = 8
DA_QK = BRANCH_W // (2 * DA_HEADS)
DA_V = 2 * DA_QK
RET_HEADS = 8
RET_V = BRANCH_W // RET_HEADS
RET_QK = RET_V // 2
RET_CHUNK = 128
RET_DECAY_BASE = 5.0
HG_HEADS = 8
HG_K = BRANCH_W // HG_HEADS
HG_V = 128
HG_CHUNK = 64
DIL_HEADS = 8
DIL_HD = BRANCH_W // DIL_HEADS
DIL_WINDOWS = (128, 512, 2048)
DIL_RATES = (1, 4, 16)
DIL_GROUPS = 3
Q_BLOCK = 128
D_FF = 11008
CONV_W = 3
EPS = 1e-5
ALPHA = (2 * DEPTH) ** 0.25
BETA = (8 * DEPTH) ** -0.25
IN_SPLIT = (
    DA_HEADS * 2 * DA_QK, DA_HEADS * 2 * DA_QK, DA_HEADS * DA_V,
    RET_HEADS * RET_QK, RET_HEADS * RET_QK, RET_HEADS * RET_V, RET_HEADS * RET_V,
    HG_HEADS * HG_K, HG_HEADS * HG_K, HG_HEADS * HG_V, HG_HEADS * HG_V,
    DIL_GROUPS * DIL_HEADS * DIL_HD, DIL_GROUPS * DIL_HEADS * DIL_HD, DIL_GROUPS * DIL_HEADS * DIL_HD,
    N_BRANCH * D_MODEL,
)
N_IN = sum(IN_SPLIT)

kernel_name = 'hybrid_gated_four_mixer_deepnorm'


def _layernorm(x, g, b):
    xf = x.astype(jnp.float32)
    mu = jnp.mean(xf, -1, keepdims=True)
    var = jnp.mean(jnp.square(xf - mu), -1, keepdims=True)
    return ((xf - mu) * lax.rsqrt(var + EPS) * g + b).astype(x.dtype)


def _rmsnorm(x, g):
    xf = x.astype(jnp.float32)
    return xf * lax.rsqrt(jnp.mean(xf * xf, -1, keepdims=True) + EPS) * g.astype(jnp.float32)


def _groupnorm(t):
    mu = jnp.mean(t, -1, keepdims=True)
    var = jnp.mean(jnp.square(t - mu), -1, keepdims=True)
    return (t - mu) * lax.rsqrt(var + EPS)


def _heads(t, h):
    b, s, _ = t.shape
    return t.reshape(b, s, h, -1).transpose(0, 2, 1, 3)


def _merge_heads(t):
    b, h, s, d = t.shape
    return t.transpose(0, 2, 1, 3).reshape(b, s, h * d)


def _to_chunks(t, c):
    b, h, s, d = t.shape
    return t.reshape(b, h, s // c, c, d).transpose(2, 0, 1, 3, 4)


def _from_chunks(t):
    n, b, h, c, d = t.shape
    return t.transpose(1, 2, 0, 3, 4).reshape(b, h, n * c, d)


def _alibi_slopes():
    n = DA_HEADS + DIL_GROUPS * DIL_HEADS
    slopes = 2.0 ** (-8.0 * jnp.arange(1, n + 1, dtype=jnp.float32) / n)
    return slopes.reshape(DIL_HEADS, 1 + DIL_GROUPS)


def _diff_attention(q, k, v, lam, slopes):
    b, h, s, _ = q.shape
    q1, q2 = jnp.split(q * (DA_QK ** -0.5), 2, axis=-1)
    k1, k2 = jnp.split(k, 2, axis=-1)
    kpos = jnp.arange(s)

    def block(args):
        a1, a2, n = args
        qpos = n * Q_BLOCK + jnp.arange(Q_BLOCK)
        dist = (qpos[:, None] - kpos[None, :]).astype(jnp.float32)
        bias = jnp.where(dist >= 0, -slopes[:, None, None] * dist, -jnp.inf)
        s1 = jnp.einsum('bhqd,bhkd->bhqk', a1, k1).astype(jnp.float32) + bias
        s2 = jnp.einsum('bhqd,bhkd->bhqk', a2, k2).astype(jnp.float32) + bias
        p = jax.nn.softmax(s1, axis=-1) - lam * jax.nn.softmax(s2, axis=-1)
        return jnp.einsum('bhqk,bhkd->bhqd', p.astype(v.dtype), v)

    out = lax.map(block, (_to_chunks(q1, Q_BLOCK), _to_chunks(q2, Q_BLOCK), jnp.arange(s // Q_BLOCK)))
    return _from_chunks(out)


def _retention(q, k, v):
    b, h, s, _ = q.shape
    c = RET_CHUNK
    log_g = jnp.log1p(-(2.0 ** (-RET_DECAY_BASE - jnp.arange(h, dtype=jnp.float32))))
    idx = jnp.arange(c, dtype=jnp.float32)
    rel = idx[:, None] - idx[None, :]
    inner_decay = jnp.where(rel >= 0, jnp.exp(log_g[:, None, None] * jnp.maximum(rel, 0.0)), 0.0)
    q_decay = jnp.exp(log_g[:, None] * (idx + 1.0))[:, :, None]
    k_decay = jnp.exp(log_g[:, None] * (c - 1.0 - idx))[:, :, None]
    chunk_decay = jnp.exp(log_g * c)[:, None, None]
    q = q * (RET_QK ** -0.5)

    def step(state, inp):
        qc, kc, vc = inp
        att = jnp.einsum('bhid,bhjd->bhij', qc, kc) * inner_decay
        o = jnp.einsum('bhij,bhje->bhie', att, vc) + jnp.einsum('bhid,bhde->bhie', qc, state) * q_decay
        state = chunk_decay * state + jnp.einsum('bhjd,bhje->bhde', kc * k_decay, vc)
        return state, o

    state0 = jnp.zeros((b, h, q.shape[-1], v.shape[-1]), jnp.float32)
    _, o = lax.scan(step, state0, (_to_chunks(q, c), _to_chunks(k, c), _to_chunks(v, c)))
    return _from_chunks(o)


def _hgrn2(q, k, v, log_f):
    b, h, s, dk = q.shape
    c = HG_CHUNK
    mask = jnp.tril(jnp.ones((c, c), bool))[:, :, None]

    def step(state, inp):
        qc, kc, vc, lf = inp
        bcum = jnp.cumsum(lf, axis=2)
        diff = bcum[:, :, :, None, :] - bcum[:, :, None, :, :]
        decay = jnp.exp(jnp.where(mask, diff, -jnp.inf))
        att = jnp.einsum('bhijc,bhjc->bhij', qc[:, :, :, None, :] * decay, kc)
        o = jnp.einsum('bhij,bhje->bhie', att, vc) + jnp.einsum('bhic,bhce->bhie', qc * jnp.exp(bcum), state)
        blast = bcum[:, :, -1:, :]
        state = jnp.exp(blast[:, :, 0, :])[..., None] * state + jnp.einsum('bhjc,bhje->bhce', kc * jnp.exp(blast - bcum), vc)
        return state, o

    state0 = jnp.zeros((b, h, dk, v.shape[-1]), jnp.float32)
    _, o = lax.scan(step, state0, (_to_chunks(q, c), _to_chunks(k, c), _to_chunks(v, c), _to_chunks(log_f, c)))
    return _from_chunks(o)


def _dilated_group(q, k, v, slopes, window, rate):
    b, h, s, dh = q.shape
    n_keys = window // rate + 1
    steps = jnp.arange(n_keys)
    bias = -slopes[:, None] * (steps * rate).astype(jnp.float32)

    def block(args):
        qc, n = args
        qpos = n * Q_BLOCK + jnp.arange(Q_BLOCK)
        kpos = qpos[:, None] - rate * steps[None, :]
        kidx = jnp.maximum(kpos, 0).reshape(-1)
        kg = jnp.take(k, kidx, axis=2).reshape(b, h, Q_BLOCK, n_keys, dh)
        vg = jnp.take(v, kidx, axis=2).reshape(b, h, Q_BLOCK, n_keys, dh)
        sc = jnp.einsum('bhqd,bhqjd->bhqj', qc, kg).astype(jnp.float32) + bias[:, None, :]
        sc = jnp.where(kpos >= 0, sc, -jnp.inf)
        lse = jax.nn.logsumexp(sc, axis=-1)
        p = jnp.exp(sc - lse[..., None])
        o = jnp.einsum('bhqj,bhqjd->bhqd', p.astype(vg.dtype), vg)
        return o, lse

    o, lse = lax.map(block, (_to_chunks(q * (dh ** -0.5), Q_BLOCK), jnp.arange(s // Q_BLOCK)))
    return _from_chunks(o), lse.transpose(1, 2, 0, 3).reshape(b, h, s)


def _hybrid_mixer(x, w_in, lam_vec, da_g, lb, hg_g, w_branch, w_o, layer):
    f32 = jnp.float32
    b, s, _ = x.shape
    (da_q, da_k, da_v, rt_q, rt_k, rt_v, rt_g, hg_q, hg_f, hg_i, hg_gt,
     dl_q, dl_k, dl_v, gates) = jnp.split(x @ w_in, np.cumsum(IN_SPLIT)[:-1].tolist(), axis=-1)
    slopes = _alibi_slopes()

    lam_init = 0.8 - 0.6 * math.exp(-0.3 * layer)
    lv = lam_vec.astype(f32)
    lam = jnp.exp(jnp.sum(lv[0] * lv[1])) - jnp.exp(jnp.sum(lv[2] * lv[3])) + lam_init
    ya = _diff_attention(_heads(da_q, DA_HEADS), _heads(da_k, DA_HEADS), _heads(da_v, DA_HEADS), lam, slopes[:, 0])
    ya = _merge_heads(_rmsnorm(ya, da_g) * (1.0 - lam_init)).astype(x.dtype)

    yr = _retention(_heads(rt_q, RET_HEADS).astype(f32), _heads(rt_k, RET_HEADS).astype(f32), _heads(rt_v, RET_HEADS).astype(f32))
    yr = (_merge_heads(_groupnorm(yr)) * jax.nn.silu(rt_g.astype(f32))).astype(x.dtype)

    fr = _heads(hg_f, HG_HEADS).astype(f32)
    lbh = lb.reshape(HG_HEADS, 1, HG_K)
    log_f = jnp.log(lbh + (1.0 - lbh) * jax.nn.sigmoid(fr))
    k_in = (1.0 - lbh) * jax.nn.sigmoid(-fr)
    yc = _hgrn2(jax.nn.silu(_heads(hg_q, HG_HEADS).astype(f32)), k_in, _heads(hg_i, HG_HEADS).astype(f32), log_f)
    yc = (_merge_heads(_rmsnorm(yc, hg_g)) * jax.nn.silu(hg_gt.astype(f32))).astype(x.dtype)

    def grp(t):
        return t.reshape(b, s, DIL_GROUPS, DIL_HEADS, DIL_HD).transpose(2, 0, 3, 1, 4)
    qd, kd, vd = grp(dl_q), grp(dl_k), grp(dl_v)
    outs, lses = [], []
    for g in range(DIL_GROUPS):
        o, l = _dilated_group(qd[g], kd[g], vd[g], slopes[:, 1 + g], DIL_WINDOWS[g], DIL_RATES[g])
        outs.append(o)
        lses.append(l)
    wts = jax.nn.softmax(jnp.stack(lses), axis=0)
    yd = jnp.sum(wts[..., None] * jnp.stack(outs).astype(f32), axis=0)
    yd = _merge_heads(yd).astype(x.dtype)

    gt = jax.nn.sigmoid(gates.reshape(b, s, N_BRANCH, D_MODEL).astype(f32))
    branches = (ya, yr, yc, yd)
    merged = gt[:, :, 0] * (branches[0] @ w_branch[0]).astype(f32)
    for n in range(1, N_BRANCH):
        merged = merged + gt[:, :, n] * (branches[n] @ w_branch[n]).astype(f32)
    return merged.astype(x.dtype) @ w_o


def _conv_ffn(x, w_up, conv_w, conv_b, w_down):
    a, gl = jnp.split(x @ w_up, 2, axis=-1)
    s = a.shape[1]
    ap = jnp.pad(a, ((0, 0), (CONV_W - 1, 0), (0, 0)))
    conv = conv_b + conv_w[0] * ap[:, 0:s]
    for j in range(1, CONV_W):
        conv = conv + conv_w[j] * ap[:, j:j + s]
    return (jax.nn.gelu(conv, approximate=False) * gl) @ w_down


def setup_inputs(seed: int = 0) -> dict:
    key = jax.random.key(seed)
    ks = jax.random.split(key, 16)
    f32 = jnp.float32

    def nrm(k, shape, scale):
        return jax.random.normal(k, shape, f32) * scale

    return {
        'x': nrm(ks[0], (BATCH, SEQ, D_MODEL), 1.0),
        'w_in': nrm(ks[1], (DEPTH, D_MODEL, N_IN), D_MODEL ** -0.5),
        'diff_lambda': nrm(ks[2], (DEPTH, 4, DA_QK), 0.1),
        'diff_norm_g': 1.0 + nrm(ks[3], (DEPTH, DA_V), 0.02),
        'hgrn_lb_logits': nrm(ks[4], (DEPTH, HG_HEADS * HG_K), 0.5),
        'hgrn_norm_g': 1.0 + nrm(ks[5], (DEPTH, HG_V), 0.02),
        'w_branch': nrm(ks[6], (DEPTH, N_BRANCH, BRANCH_W, D_MODEL), BRANCH_W ** -0.5),
        'w_o': nrm(ks[7], (DEPTH, D_MODEL, D_MODEL), BETA * D_MODEL ** -0.5),
        'ln1_g': 1.0 + nrm(ks[8], (DEPTH, D_MODEL), 0.02),
        'ln1_b': nrm(ks[9], (DEPTH, D_MODEL), 0.02),
        'w_up': nrm(ks[10], (DEPTH, D_MODEL, 2 * D_FF), D_MODEL ** -0.5),
        'conv_w': nrm(ks[11], (DEPTH, CONV_W, D_FF), CONV_W ** -0.5),
        'conv_b': nrm(ks[12], (DEPTH, D_FF), 0.02),
        'w_down': nrm(ks[13], (DEPTH, D_FF, D_MODEL), BETA * D_FF ** -0.5),
        'ln2_g': 1.0 + nrm(ks[14], (DEPTH, D_MODEL), 0.02),
        'ln2_b': nrm(ks[15], (DEPTH, D_MODEL), 0.02),
    }


def reference(x, w_in, diff_lambda, diff_norm_g, hgrn_lb_logits, hgrn_norm_g, w_branch, w_o,
              ln1_g, ln1_b, w_up, conv_w, conv_b, w_down, ln2_g, ln2_b):
    p = jax.nn.softmax(hgrn_lb_logits.astype(jnp.float32), axis=0)
    lower_bounds = jnp.cumsum(p, axis=0) - p[0]
    h = x
    for l in range(DEPTH):
        mix = _hybrid_mixer(h, w_in[l], diff_lambda[l], diff_norm_g[l], lower_bounds[l],
                            hgrn_norm_g[l], w_branch[l], w_o[l], l)
        h = _layernorm(ALPHA * h + mix, ln1_g[l], ln1_b[l])
        h = _layernorm(ALPHA * h + _conv_ffn(h, w_up[l], conv_w[l], conv_b[l], w_down[l]), ln2_g[l], ln2_b[l])
    return h
```

```python
import functools
import math

import jax
import jax.numpy as jnp
from jax import lax
from jax.experimental import pallas as pl
from jax.experimental.pallas import tpu as pltpu

F32 = jnp.float32
BF16 = jnp.bfloat16

N_HEADS = 8
HEAD_W = 128
BRANCH_W = N_HEADS * HEAD_W
N_BRANCH = 4
DA_QK = 64
RET_QK = 64
RET_CHUNK = 128
RET_DECAY_BASE = 5.0
HG_CHUNK = 64
DIL_WINDOWS = (128, 512, 2048)
DIL_RATES = (1, 4, 16)
DIL_KEYS = 129
DIL_BLOCK = 128
CONV_W = 3
EPS = 1e-5
NEG_INF = float("-inf")

_C = BRANCH_W // 128
OFF_DA_Q, OFF_DA_K, OFF_DA_V = 0, _C, 2 * _C
OFF_RT_Q, OFF_RT_K = 3 * _C, 3 * _C + _C // 2
OFF_RT_V, OFF_RT_G = 4 * _C, 5 * _C
OFF_HG_Q, OFF_HG_F, OFF_HG_I, OFF_HG_G = 6 * _C, 7 * _C, 8 * _C, 9 * _C
OFF_DL_Q, OFF_DL_K, OFF_DL_V = 10 * _C, 13 * _C, 16 * _C
OFF_GATES = 19 * _C

VMEM_LIMIT = 56 * 1024 * 1024


def _params(sem):
    return pltpu.CompilerParams(dimension_semantics=sem, vmem_limit_bytes=VMEM_LIMIT)


def _dot(a, b):
    return jnp.dot(a, b, preferred_element_type=F32)


def _dot_nt(a, b):
    return lax.dot_general(a, b, (((1,), (1,)), ((), ())), preferred_element_type=F32)


def _dot_tn(a, b):
    return lax.dot_general(a, b, (((0,), (0,)), ((), ())), preferred_element_type=F32)


def _pick(dim, pref):
    if dim <= pref:
        return dim
    t = pref
    while t >= 128:
        if dim % t == 0 and t % 128 == 0:
            return t
        t -= 128
    return dim


def _mm_kernel(*refs, nk, alpha, has_res):
    if has_res:
        x_ref, w_ref, r_ref, o_ref, acc_ref = refs
    else:
        x_ref, w_ref, o_ref, acc_ref = refs
        r_ref = None
    k = pl.program_id(2)
    p = _dot(x_ref[...], w_ref[...])

    def finish(acc):
        if has_res:
            acc = alpha * r_ref[...] + acc
        o_ref[...] = acc.astype(o_ref.dtype)

    if nk == 1:
        finish(p)
    else:
        @pl.when(k == 0)
        def _():
            acc_ref[...] = p

        @pl.when(jnp.logical_and(k > 0, k < nk - 1))
        def _():
            acc_ref[...] += p

        @pl.when(k == nk - 1)
        def _():
            finish(acc_ref[...] + p)


def _matmul(x, w, *, out_dtype, tm=1024, tn=1024, tk=1024, res=None, alpha=1.0, name="mm"):
    m, kdim = x.shape
    _, n = w.shape
    tm, tn, tk = _pick(m, tm), _pick(n, tn), _pick(kdim, tk)
    nk = kdim // tk
    in_specs = [
        pl.BlockSpec((tm, tk), lambda j, i, k: (i, k)),
        pl.BlockSpec((tk, tn), lambda j, i, k: (k, j)),
    ]
    args = [x, w]
    if res is not None:
        in_specs.append(pl.BlockSpec((tm, tn), lambda j, i, k: (i, j)))
        args.append(res)
    return pl.pallas_call(
        functools.partial(_mm_kernel, nk=nk, alpha=alpha, has_res=res is not None),
        grid=(n // tn, m // tm, nk),
        in_specs=in_specs,
        out_specs=pl.BlockSpec((tm, tn), lambda j, i, k: (i, j)),
        out_shape=jax.ShapeDtypeStruct((m, n), out_dtype),
        scratch_shapes=[pltpu.VMEM((tm, tn) if nk > 1 else (8, 128), F32)],
        compiler_params=_params(("parallel", "parallel", "arbitrary")),
        name=name,
    )(*args)


def _ln_kernel(x_ref, g_ref, b_ref, o_ref, ob_ref):
    x = x_ref[...]
    mu = jnp.mean(x, axis=-1, keepdims=True)
    xc = x - mu
    var = jnp.mean(xc * xc, axis=-1, keepdims=True)
    y = xc * lax.rsqrt(var + EPS) * g_ref[...] + b_ref[...]
    o_ref[...] = y
    ob_ref[...] = y.astype(BF16)


def _layernorm(x, g, b, *, tm=256):
    m, d = x.shape
    tm = _pick(m, tm)
    return pl.pallas_call(
        _ln_kernel,
        grid=(m // tm,),
        in_specs=[
            pl.BlockSpec((tm, d), lambda i: (i, 0)),
            pl.BlockSpec((1, d), lambda i: (0, 0)),
            pl.BlockSpec((1, d), lambda i: (0, 0)),
        ],
        out_specs=[pl.BlockSpec((tm, d), lambda i: (i, 0)), pl.BlockSpec((tm, d), lambda i: (i, 0))],
        out_shape=[jax.ShapeDtypeStruct((m, d), F32), jax.ShapeDtypeStruct((m, d), BF16)],
        compiler_params=_params(("parallel",)),
        name="layernorm",
    )(x, g.reshape(1, d), b.reshape(1, d))


def _diff_attn_kernel(slope_ref, lam_ref, g_ref, q_ref, k_ref, v_ref, o_ref,
                      m1_ref, l1_ref, a1_ref, m2_ref, l2_ref, a2_ref, *, t, lam_init):
    h = pl.program_id(1)
    qi = pl.program_id(2)
    ki = pl.program_id(3)

    @pl.when(ki == 0)
    def _():
        for m_ref, l_ref, a_ref in ((m1_ref, l1_ref, a1_ref), (m2_ref, l2_ref, a2_ref)):
            m_ref[...] = jnp.full(m_ref.shape, NEG_INF, F32)
            l_ref[...] = jnp.zeros(l_ref.shape, F32)
            a_ref[...] = jnp.zeros(a_ref.shape, F32)

    @pl.when(ki <= qi)
    def _():
        slope = slope_ref[h]
        q = q_ref[...] * (DA_QK ** -0.5)
        k = k_ref[...]
        v = v_ref[...].astype(BF16)
        qpos = qi * t + lax.broadcasted_iota(jnp.int32, (t, t), 0)
        kpos = ki * t + lax.broadcasted_iota(jnp.int32, (t, t), 1)
        dist = (qpos - kpos).astype(F32)
        bias = jnp.where(dist >= 0, -slope * dist, NEG_INF)
        for idx, (m_ref, l_ref, a_ref) in enumerate(((m1_ref, l1_ref, a1_ref), (m2_ref, l2_ref, a2_ref))):
            qq = q[:, idx * DA_QK:(idx + 1) * DA_QK].astype(BF16)
            kk = k[:, idx * DA_QK:(idx + 1) * DA_QK].astype(BF16)
            s = _dot_nt(qq, kk) + bias
            m_prev = m_ref[...]
            m_new = jnp.maximum(m_prev, jnp.max(s, axis=-1, keepdims=True))
            corr = jnp.exp(m_prev - m_new)
            p = jnp.exp(s - m_new[:, 0:1])
            l_ref[...] = corr * l_ref[...] + jnp.sum(p, axis=-1, keepdims=True)
            a_ref[...] = corr * a_ref[...] + _dot(p.astype(BF16), v)
            m_ref[...] = m_new

    @pl.when(ki == pl.num_programs(3) - 1)
    def _():
        lv = lam_ref[...]
        lam = (jnp.exp(jnp.sum(lv[0:1] * lv[1:2], axis=-1, keepdims=True))
               - jnp.exp(jnp.sum(lv[2:3] * lv[3:4], axis=-1, keepdims=True)) + lam_init)
        o = a1_ref[...] / l1_ref[...] - lam * (a2_ref[...] / l2_ref[...])
        ms = jnp.mean(o * o, axis=-1, keepdims=True)
        o = o * lax.rsqrt(ms + EPS) * g_ref[...] * (1.0 - lam_init)
        o_ref[...] = o.astype(o_ref.dtype)


def _diff_attention(proj3, slopes, lam_vec, g, lam_init, *, t=512):
    b, s, _ = proj3.shape
    t = _pick(s, t)
    n = s // t
    kern = functools.partial(_diff_attn_kernel, t=t, lam_init=lam_init)
    grid_spec = pltpu.PrefetchScalarGridSpec(
        num_scalar_prefetch=1,
        grid=(b, N_HEADS, n, n),
        in_specs=[
            pl.BlockSpec(lam_vec.shape, lambda bi, h, qi, ki, sl: (0, 0)),
            pl.BlockSpec((1, HEAD_W), lambda bi, h, qi, ki, sl: (0, 0)),
            pl.BlockSpec((None, t, HEAD_W), lambda bi, h, qi, ki, sl: (bi, qi, OFF_DA_Q + h)),
            pl.BlockSpec((None, t, HEAD_W), lambda bi, h, qi, ki, sl: (bi, jnp.minimum(ki, qi), OFF_DA_K + h)),
            pl.BlockSpec((None, t, HEAD_W), lambda bi, h, qi, ki, sl: (bi, jnp.minimum(ki, qi), OFF_DA_V + h)),
        ],
        out_specs=pl.BlockSpec((None, t, HEAD_W), lambda bi, h, qi, ki, sl: (bi, qi, h)),
        scratch_shapes=[pltpu.VMEM((t, HEAD_W), F32) for _ in range(6)],
    )
    return pl.pallas_call(
        kern,
        grid_spec=grid_spec,
        out_shape=jax.ShapeDtypeStruct((b, s, BRANCH_W), BF16),
        compiler_params=_params(("parallel", "parallel", "parallel", "arbitrary")),
        name="diff_attention",
    )(slopes, lam_vec, g.reshape(1, HEAD_W), proj3, proj3, proj3)


def _retention_kernel(inner_ref, qd_ref, kd_ref, cd_ref, q_ref, k_ref, v_ref, g_ref, o_ref,
                      state_ref, *, rows):
    ci = pl.program_id(2)

    @pl.when(ci == 0)
    def _():
        state_ref[...] = jnp.zeros(state_ref.shape, F32)

    c = RET_CHUNK
    for p in range(2):
        inner = inner_ref[p]
        qdec = qd_ref[p]
        kdec = kd_ref[p]
        cdec = cd_ref[p]
        state = state_ref[p]
        for n in range(rows // c):
            r0 = n * c
            qc = (q_ref[r0:r0 + c, p * RET_QK:(p + 1) * RET_QK] * (RET_QK ** -0.5)).astype(BF16)
            kf = k_ref[r0:r0 + c, p * RET_QK:(p + 1) * RET_QK]
            vc = v_ref[r0:r0 + c, p * HEAD_W:(p + 1) * HEAD_W].astype(BF16)
            att = _dot_nt(qc, kf.astype(BF16)) * inner
            o = _dot(att.astype(BF16), vc) + _dot(qc, state.astype(BF16)) * qdec
            state = cdec * state + _dot_tn((kf * kdec).astype(BF16), vc)
            mu = jnp.mean(o, axis=-1, keepdims=True)
            oc = o - mu
            var = jnp.mean(oc * oc, axis=-1, keepdims=True)
            gate = g_ref[r0:r0 + c, p * HEAD_W:(p + 1) * HEAD_W]
            y = oc * lax.rsqrt(var + EPS) * (gate * jax.nn.sigmoid(gate))
            o_ref[r0:r0 + c, p * HEAD_W:(p + 1) * HEAD_W] = y.astype(o_ref.dtype)
        state_ref[p] = state


def _retention_tables():
    h = N_HEADS
    c = RET_CHUNK
    log_g = jnp.log1p(-(2.0 ** (-RET_DECAY_BASE - jnp.arange(h, dtype=F32))))
    idx = jnp.arange(c, dtype=F32)
    rel = idx[:, None] - idx[None, :]
    inner = jnp.where(rel >= 0, jnp.exp(log_g[:, None, None] * jnp.maximum(rel, 0.0)), 0.0)
    qd = jnp.broadcast_to(jnp.exp(log_g[:, None] * (idx + 1.0))[:, :, None], (h, c, HEAD_W))
    kd = jnp.broadcast_to(jnp.exp(log_g[:, None] * (c - 1.0 - idx))[:, :, None], (h, c, RET_QK))
    cd = jnp.broadcast_to(jnp.exp(log_g * c)[:, None, None], (h, RET_QK, HEAD_W))
    return inner, qd, kd, cd


def _retention(proj3, *, rows=512):
    b, s, _ = proj3.shape
    rows = _pick(s, rows)
    inner, qd, kd, cd = _retention_tables()
    c = RET_CHUNK
    return pl.pallas_call(
        functools.partial(_retention_kernel, rows=rows),
        grid=(b, N_HEADS // 2, s // rows),
        in_specs=[
            pl.BlockSpec((2, c, c), lambda bi, hp, ci: (hp, 0, 0)),
            pl.BlockSpec((2, c, HEAD_W), lambda bi, hp, ci: (hp, 0, 0)),
            pl.BlockSpec((2, c, RET_QK), lambda bi, hp, ci: (hp, 0, 0)),
            pl.BlockSpec((2, RET_QK, HEAD_W), lambda bi, hp, ci: (hp, 0, 0)),
            pl.BlockSpec((None, rows, 128), lambda bi, hp, ci: (bi, ci, OFF_RT_Q + hp)),
            pl.BlockSpec((None, rows, 128), lambda bi, hp, ci: (bi, ci, OFF_RT_K + hp)),
            pl.BlockSpec((None, rows, 2 * HEAD_W), lambda bi, hp, ci: (bi, ci, OFF_RT_V // 2 + hp)),
            pl.BlockSpec((None, rows, 2 * HEAD_W), lambda bi, hp, ci: (bi, ci, OFF_RT_G // 2 + hp)),
        ],
        out_specs=pl.BlockSpec((None, rows, 2 * HEAD_W), lambda bi, hp, ci: (bi, ci, hp)),
        out_shape=jax.ShapeDtypeStruct((b, s, BRANCH_W), BF16),
        scratch_shapes=[pltpu.VMEM((2, RET_QK, HEAD_W), F32)],
        compiler_params=_params(("parallel", "parallel", "arbitrary")),
        name="retention",
    )(inner, qd, kd, cd, proj3, proj3, proj3, proj3)


def _split3(x):
    hi = x.astype(BF16)
    r1 = x - hi.astype(F32)
    mid = r1.astype(BF16)
    lo = (r1 - mid.astype(F32)).astype(BF16)
    return hi, mid, lo


def _hgrn2_kernel(lb_ref, gn_ref, q_ref, f_ref, i_ref, gt_ref, o_ref,
                  state_ref, b_sc, k_sc, *, rows):
    ci = pl.program_id(2)
    c = HG_CHUNK

    @pl.when(ci == 0)
    def _():
        state_ref[...] = jnp.zeros(state_ref.shape, F32)

    lb = lb_ref[...]
    gn = gn_ref[...]
    row = lax.broadcasted_iota(jnp.int32, (c, HEAD_W), 0)
    tri = (lax.broadcasted_iota(jnp.int32, (c, c), 0)
           >= lax.broadcasted_iota(jnp.int32, (c, c), 1)).astype(BF16)

    def chunk(n, carry):
        r0 = pl.multiple_of(n * c, c)
        fr = f_ref[pl.ds(r0, c), :]
        log_f = jnp.log(lb + (1.0 - lb) * jax.nn.sigmoid(fr))
        k_in = (1.0 - lb) * jax.nn.sigmoid(-fr)
        qraw = q_ref[pl.ds(r0, c), :]
        q = qraw * jax.nn.sigmoid(qraw)
        v = i_ref[pl.ds(r0, c), :]
        hi, mid, lo = _split3(log_f)
        bcum = _dot(tri, hi) + _dot(tri, mid) + _dot(tri, lo)
        b_sc[...] = bcum
        k_sc[...] = k_in

        def intra(j, acc):
            bj = b_sc[pl.ds(j, 1), :]
            kj = k_sc[pl.ds(j, 1), :]
            vj = i_ref[pl.ds(r0 + j, 1), :]
            e = jnp.exp(jnp.where(row >= j, bcum - bj, NEG_INF))
            a = jnp.sum(q * e * kj, axis=-1, keepdims=True)
            return acc + a * vj

        o = lax.fori_loop(0, c, intra, jnp.zeros((c, HEAD_W), F32))
        state_t = state_ref[...]
        o = o + _dot_nt((q * jnp.exp(bcum)).astype(BF16), state_t.astype(BF16))
        blast = bcum[c - 1:c, :]
        kd = (k_in * jnp.exp(blast - bcum)).astype(BF16)
        state_ref[...] = jnp.exp(blast) * state_t + _dot_tn(v.astype(BF16), kd)
        ms = jnp.mean(o * o, axis=-1, keepdims=True)
        gate = gt_ref[pl.ds(r0, c), :]
        y = o * lax.rsqrt(ms + EPS) * gn * (gate * jax.nn.sigmoid(gate))
        o_ref[pl.ds(r0, c), :] = y.astype(o_ref.dtype)
        return carry

    lax.fori_loop(0, rows // c, chunk, 0)


def _hgrn2(proj3, lb, gn, *, rows=512):
    b, s, _ = proj3.shape
    rows = _pick(s, rows)
    c = HG_CHUNK
    blk = lambda off: pl.BlockSpec((None, rows, HEAD_W), lambda bi, h, ci: (bi, ci, off + h))
    return pl.pallas_call(
        functools.partial(_hgrn2_kernel, rows=rows),
        grid=(b, N_HEADS, s // rows),
        in_specs=[
            pl.BlockSpec((None, 1, HEAD_W), lambda bi, h, ci: (h, 0, 0)),
            pl.BlockSpec((1, HEAD_W), lambda bi, h, ci: (0, 0)),
            blk(OFF_HG_Q), blk(OFF_HG_F), blk(OFF_HG_I), blk(OFF_HG_G),
        ],
        out_specs=pl.BlockSpec((None, rows, HEAD_W), lambda bi, h, ci: (bi, ci, h)),
        out_shape=jax.ShapeDtypeStruct((b, s, BRANCH_W), BF16),
        scratch_shapes=[pltpu.VMEM((HEAD_W, HEAD_W), F32),
                        pltpu.VMEM((c, HEAD_W), F32), pltpu.VMEM((c, HEAD_W), F32)],
        compiler_params=_params(("parallel", "parallel", "arbitrary")),
        name="hgrn2",
    )(lb.reshape(N_HEADS, 1, HEAD_W), gn.reshape(1, HEAD_W), proj3, proj3, proj3, proj3)


def _dil_kernel(slope_ref, q_ref, kp_ref, kc_ref, vp_ref, vc_ref, *rest, rate, first, last):
    if first:
        o_ref, lse_ref = rest
        po_ref = pl_ref = None
    elif last:
        po_ref, pl_ref, o_ref = rest
        lse_ref = None
    else:
        po_ref, pl_ref, o_ref, lse_ref = rest
    h = pl.program_id(2)
    mi = pl.program_id(3)
    t = DIL_BLOCK
    slope = slope_ref[h] * float(rate)
    q = (q_ref[...] * (HEAD_W ** -0.5)).astype(BF16)
    i_idx = lax.broadcasted_iota(jnp.int32, (t, t), 0)
    j_idx = lax.broadcasted_iota(jnp.int32, (t, t), 1)
    d_cur = (i_idx - j_idx).astype(F32)
    s_cur = _dot_nt(q, kc_ref[...].astype(BF16))
    s_cur = jnp.where(d_cur >= 0, s_cur - slope * d_cur, NEG_INF)
    d_prev = d_cur + float(t)
    s_prev = _dot_nt(q, kp_ref[...].astype(BF16))
    lim = jnp.where(mi > 0, float(t), -1.0)
    s_prev = jnp.where(d_prev <= lim, s_prev - slope * d_prev, NEG_INF)
    m = jnp.maximum(jnp.max(s_cur, axis=-1, keepdims=True), jnp.max(s_prev, axis=-1, keepdims=True))
    p_cur = jnp.exp(s_cur - m)
    p_prev = jnp.exp(s_prev - m)
    l = jnp.sum(p_cur, axis=-1, keepdims=True) + jnp.sum(p_prev, axis=-1, keepdims=True)
    acc = _dot(p_cur.astype(BF16), vc_ref[...].astype(BF16)) + _dot(p_prev.astype(BF16), vp_ref[...].astype(BF16))
    o = acc / l
    lse = jnp.broadcast_to(m + jnp.log(l), (t, HEAD_W))
    if not first:
        lse_p = pl_ref[...]
        lse_n = jnp.maximum(lse, lse_p)
        lse_n = lse_n + jnp.log(jnp.exp(lse - lse_n) + jnp.exp(lse_p - lse_n))
        o = po_ref[...] * jnp.exp(lse_p - lse_n) + o * jnp.exp(lse - lse_n)
        lse = lse_n
    o_ref[...] = o.astype(o_ref.dtype)
    if not last:
        lse_ref[...] = lse


def _dilated_group(proj3, slopes, g, prev):
    b, s, n_in = proj3.shape
    rate = DIL_RATES[g]
    first = g == 0
    last = g == len(DIL_RATES) - 1
    t = DIL_BLOCK
    sm = s // rate
    cin = n_in // 128
    view = proj3.reshape(b, sm, rate * n_in)

    def col(off):
        return lambda bi, r, h, mi, sl: (bi, mi, r * cin + off + g * N_HEADS + h)

    def col_prev(off):
        return lambda bi, r, h, mi, sl: (bi, jnp.maximum(mi - 1, 0), r * cin + off + g * N_HEADS + h)

    def out_idx(bi, r, h, mi, sl):
        return (bi, mi, r * N_HEADS + h)

    blk = (None, t, HEAD_W)
    in_specs = [
        pl.BlockSpec(blk, col(OFF_DL_Q)),
        pl.BlockSpec(blk, col_prev(OFF_DL_K)), pl.BlockSpec(blk, col(OFF_DL_K)),
        pl.BlockSpec(blk, col_prev(OFF_DL_V)), pl.BlockSpec(blk, col(OFF_DL_V)),
    ]
    args = [view, view, view, view, view]
    if not first:
        in_specs += [pl.BlockSpec(blk, out_idx), pl.BlockSpec(blk, out_idx)]
        args += [prev[0].reshape(b, sm, rate * BRANCH_W), prev[1].reshape(b, sm, rate * BRANCH_W)]
    o_dtype = BF16 if last else F32
    out_shape = [jax.ShapeDtypeStruct((b, sm, rate * BRANCH_W), o_dtype)]
    out_specs = [pl.BlockSpec(blk, out_idx)]
    if not last:
        out_shape.append(jax.ShapeDtypeStruct((b, sm, rate * BRANCH_W), F32))
        out_specs.append(pl.BlockSpec(blk, out_idx))
    grid_spec = pltpu.PrefetchScalarGridSpec(
        num_scalar_prefetch=1,
        grid=(b, rate, N_HEADS, sm // t),
        in_specs=in_specs,
        out_specs=out_specs,
    )
    outs = pl.pallas_call(
        functools.partial(_dil_kernel, rate=rate, first=first, last=last),
        grid_spec=grid_spec,
        out_shape=out_shape,
        compiler_params=_params(("parallel", "parallel", "parallel", "parallel")),
        name=f"dilated_g{g}",
    )(slopes, *args)
    return [o.reshape(b, s, BRANCH_W) for o in outs]


def _dilated_attention(proj3, slopes4):
    state = None
    for g in range(len(DIL_RATES)):
        state = _dilated_group(proj3, slopes4[:, 1 + g], g, state)
    return state[0]


def _merge_kernel(ya_ref, yr_ref, yc_ref, yd_ref, w_ref, g0_ref, g1_ref, g2_ref, g3_ref, o_ref):
    acc = None
    for n, (y_ref, g_ref) in enumerate(((ya_ref, g0_ref), (yr_ref, g1_ref), (yc_ref, g2_ref), (yd_ref, g3_ref))):
        t = jax.nn.sigmoid(g_ref[...]) * _dot(y_ref[...], w_ref[n])
        acc = t if acc is None else acc + t
    o_ref[...] = acc.astype(o_ref.dtype)


def _merge(ys, w_branch, proj, d_model, *, tm=512, tn=512):
    m = proj.shape[0]
    tm, tn = _pick(m, tm), _pick(d_model, tn)
    gcols = d_model // tn
    goff = OFF_GATES * 128 // tn
    y_spec = pl.BlockSpec((tm, BRANCH_W), lambda j, i: (i, 0))

    def gate_spec(n):
        return pl.BlockSpec((tm, tn), lambda j, i: (i, goff + n * gcols + j))

    return pl.pallas_call(
        _merge_kernel,
        grid=(d_model // tn, m // tm),
        in_specs=[y_spec, y_spec, y_spec, y_spec,
                  pl.BlockSpec((N_BRANCH, BRANCH_W, tn), lambda j, i: (0, 0, j)),
                  gate_spec(0), gate_spec(1), gate_spec(2), gate_spec(3)],
        out_specs=pl.BlockSpec((tm, tn), lambda j, i: (i, j)),
        out_shape=jax.ShapeDtypeStruct((m, d_model), BF16),
        compiler_params=_params(("parallel", "parallel")),
        name="merge",
    )(*ys, w_branch, proj, proj, proj, proj)


def _convgate_kernel(a_ref, ap_ref, gl_ref, cw_ref, cb_ref, o_ref, *, ts):
    ti = pl.program_id(1)
    a = a_ref[...]
    prev = ap_ref[...]
    prev = jnp.where(ti > 0, prev, jnp.zeros_like(prev))
    row = lax.broadcasted_iota(jnp.int32, a.shape, 0)
    a1 = jnp.where(row == 0, prev[7:8, :], pltpu.roll(a, 1, 0))
    a2 = jnp.where(row == 0, prev[6:7, :],
                   jnp.where(row == 1, prev[7:8, :], pltpu.roll(a, 2, 0)))
    cw = cw_ref[...]
    conv = cb_ref[...] + cw[0:1, :] * a2 + cw[1:2, :] * a1 + cw[2:3, :] * a
    gelu = 0.5 * conv * (1.0 + lax.erf(conv * (2.0 ** -0.5)))
    o_ref[...] = (gelu * gl_ref[...]).astype(o_ref.dtype)


def _convgate(u3, conv_w, conv_b, *, ts=512, tc=512):
    b, s, two_ff = u3.shape
    d_ff = two_ff // 2
    ts, tc = _pick(s, ts), _pick(d_ff, tc)
    nc = d_ff // tc
    return pl.pallas_call(
        functools.partial(_convgate_kernel, ts=ts),
        grid=(b, s // ts, nc),
        in_specs=[
            pl.BlockSpec((None, ts, tc), lambda bi, ti, j: (bi, ti, j)),
            pl.BlockSpec((None, 8, tc), lambda bi, ti, j: (bi, jnp.maximum(ti * (ts // 8) - 1, 0), j)),
            pl.BlockSpec((None, ts, tc), lambda bi, ti, j: (bi, ti, nc + j)),
            pl.BlockSpec((CONV_W, tc), lambda bi, ti, j: (0, j)),
            pl.BlockSpec((1, tc), lambda bi, ti, j: (0, j)),
        ],
        out_specs=pl.BlockSpec((None, ts, tc), lambda bi, ti, j: (bi, ti, j)),
        out_shape=jax.ShapeDtypeStruct((b, s, d_ff), BF16),
        compiler_params=_params(("parallel", "parallel", "parallel")),
        name="convgate",
    )(u3, u3, u3, conv_w, conv_b.reshape(1, d_ff))


def _alibi_slopes():
    n = 4 * N_HEADS
    slopes = 2.0 ** (-8.0 * jnp.arange(1, n + 1, dtype=F32) / n)
    return slopes.reshape(N_HEADS, 4)


def kernel(x, w_in, diff_lambda, diff_norm_g, hgrn_lb_logits, hgrn_norm_g, w_branch, w_o,
           ln1_g, ln1_b, w_up, conv_w, conv_b, w_down, ln2_g, ln2_b):
    b, s, d = x.shape
    depth = w_in.shape[0]
    m = b * s
    alpha = (2 * depth) ** 0.25
    slopes4 = _alibi_slopes()
    p = jax.nn.softmax(hgrn_lb_logits.astype(F32), axis=0)
    lower_bounds = jnp.cumsum(p, axis=0) - p[0]

    h = x.reshape(m, d)
    hb = h.astype(BF16)
    for l in range(depth):
        lam_init = 0.8 - 0.6 * math.exp(-0.3 * l)
        proj = _matmul(hb, w_in[l].astype(BF16), out_dtype=F32, name="in_proj")
        proj3 = proj.reshape(b, s, -1)
        ya = _diff_attention(proj3, slopes4[:, 0], diff_lambda[l], diff_norm_g[l], lam_init)
        yr = _retention(proj3)
        yc = _hgrn2(proj3, lower_bounds[l], hgrn_norm_g[l])
        yd = _dilated_attention(proj3, slopes4)
        ys = [y.reshape(m, BRANCH_W) for y in (ya, yr, yc, yd)]
        merged = _merge(ys, w_branch[l].astype(BF16), proj, d)
        z = _matmul(merged, w_o[l].astype(BF16), out_dtype=F32, res=h, alpha=alpha, name="out_proj")
        h, hb = _layernorm(z, ln1_g[l], ln1_b[l])
        u = _matmul(hb, w_up[l].astype(BF16), out_dtype=F32, name="ffn_up")
        act = _convgate(u.reshape(b, s, -1), conv_w[l], conv_b[l])
        z = _matmul(act.reshape(m, -1), w_down[l].astype(BF16), out_dtype=F32, res=h, alpha=alpha,
                    name="ffn_down")
        h, hb = _layernorm(z, ln2_g[l], ln2_b[l])
    return h.reshape(b, s, d)
```

```python
import functools
import math

import jax
import jax.numpy as jnp
from jax import lax
from jax.experimental import pallas as pl
from jax.experimental.pallas import tpu as pltpu

F32 = jnp.float32
BF16 = jnp.bfloat16

N_HEADS = 8
HEAD_W = 128
BRANCH_W = N_HEADS * HEAD_W
N_BRANCH = 4
DA_QK = 64
RET_QK = 64
RET_CHUNK = 128
RET_DECAY_BASE = 5.0
HG_CHUNK = 64
DIL_WINDOWS = (128, 512, 2048)
DIL_RATES = (1, 4, 16)
DIL_KEYS = 129
DIL_BLOCK = 128
CONV_W = 3
EPS = 1e-5
NEG_INF = float("-inf")

_C = BRANCH_W // 128
OFF_DA_Q, OFF_DA_K, OFF_DA_V = 0, _C, 2 * _C
OFF_RT_Q, OFF_RT_K = 3 * _C, 3 * _C + _C // 2
OFF_RT_V, OFF_RT_G = 4 * _C, 5 * _C
OFF_HG_Q, OFF_HG_F, OFF_HG_I, OFF_HG_G = 6 * _C, 7 * _C, 8 * _C, 9 * _C
OFF_DL_Q, OFF_DL_K, OFF_DL_V = 10 * _C, 13 * _C, 16 * _C
OFF_GATES = 19 * _C

VMEM_LIMIT = 56 * 1024 * 1024


def _params(sem):
    return pltpu.CompilerParams(dimension_semantics=sem, vmem_limit_bytes=VMEM_LIMIT)


def _dot(a, b):
    return jnp.dot(a, b, preferred_element_type=F32)


def _dot_nt(a, b):
    return lax.dot_general(a, b, (((1,), (1,)), ((), ())), preferred_element_type=F32)


def _dot_tn(a, b):
    return lax.dot_general(a, b, (((0,), (0,)), ((), ())), preferred_element_type=F32)


def _pick(dim, pref):
    if dim <= pref:
        return dim
    t = pref
    while t >= 128:
        if dim % t == 0 and t % 128 == 0:
            return t
        t -= 128
    return dim


def _mm_kernel(*refs, nk, alpha, has_res):
    if has_res:
        x_ref, w_ref, r_ref, o_ref, acc_ref = refs
    else:
        x_ref, w_ref, o_ref, acc_ref = refs
        r_ref = None
    k = pl.program_id(2)
    p = _dot(x_ref[...], w_ref[...])

    def finish(acc):
        if has_res:
            acc = alpha * r_ref[...] + acc
        o_ref[...] = acc.astype(o_ref.dtype)

    if nk == 1:
        finish(p)
    else:
        @pl.when(k == 0)
        def _():
            acc_ref[...] = p

        @pl.when(jnp.logical_and(k > 0, k < nk - 1))
        def _():
            acc_ref[...] += p

        @pl.when(k == nk - 1)
        def _():
            finish(acc_ref[...] + p)


def _matmul(x, w, *, out_dtype, tm=512, tn=1024, tk=None, res=None, alpha=1.0, name="mm"):
    m, kdim = x.shape
    _, n = w.shape
    tm, tn = _pick(m, tm), _pick(n, tn)
    tk = kdim if tk is None else _pick(kdim, tk)
    nk = kdim // tk
    w_mode = {"pipeline_mode": pl.Buffered(1)} if nk == 1 else {}
    in_specs = [
        pl.BlockSpec((tm, tk), lambda j, i, k: (i, k)),
        pl.BlockSpec((tk, tn), lambda j, i, k: (k, j), **w_mode),
    ]
    args = [x, w]
    if res is not None:
        in_specs.append(pl.BlockSpec((tm, tn), lambda j, i, k: (i, j)))
        args.append(res)
    return pl.pallas_call(
        functools.partial(_mm_kernel, nk=nk, alpha=alpha, has_res=res is not None),
        grid=(n // tn, m // tm, nk),
        in_specs=in_specs,
        out_specs=pl.BlockSpec((tm, tn), lambda j, i, k: (i, j)),
        out_shape=jax.ShapeDtypeStruct((m, n), out_dtype),
        scratch_shapes=[pltpu.VMEM((tm, tn) if nk > 1 else (8, 128), F32)],
        compiler_params=_params(("parallel", "parallel", "arbitrary")),
        name=name,
    )(*args)


def _ln_kernel(x_ref, g_ref, b_ref, o_ref, ob_ref):
    x = x_ref[...]
    mu = jnp.mean(x, axis=-1, keepdims=True)
    xc = x - mu
    var = jnp.mean(xc * xc, axis=-1, keepdims=True)
    y = xc * lax.rsqrt(var + EPS) * g_ref[...] + b_ref[...]
    o_ref[...] = y
    ob_ref[...] = y.astype(BF16)


def _layernorm(x, g, b, *, tm=256):
    m, d = x.shape
    tm = _pick(m, tm)
    return pl.pallas_call(
        _ln_kernel,
        grid=(m // tm,),
        in_specs=[
            pl.BlockSpec((tm, d), lambda i: (i, 0)),
            pl.BlockSpec((1, d), lambda i: (0, 0)),
            pl.BlockSpec((1, d), lambda i: (0, 0)),
        ],
        out_specs=[pl.BlockSpec((tm, d), lambda i: (i, 0)), pl.BlockSpec((tm, d), lambda i: (i, 0))],
        out_shape=[jax.ShapeDtypeStruct((m, d), F32), jax.ShapeDtypeStruct((m, d), BF16)],
        compiler_params=_params(("parallel",)),
        name="layernorm",
    )(x, g.reshape(1, d), b.reshape(1, d))


LOG2E = 1.4426950408889634


def _diff_attn_kernel(qi_ref, ki_ref, slope_ref, lam_ref, g_ref, q_ref, k_ref, v_ref, o_ref,
                      m1_ref, l1_ref, a1_ref, m2_ref, l2_ref, a2_ref, bias_sc, *, t, lam_init):
    h = pl.program_id(1)
    step = pl.program_id(2)
    qi = qi_ref[step]
    ki = ki_ref[step]
    slope2 = slope_ref[h] * LOG2E

    @pl.when(step == 0)
    def _():
        d = (lax.broadcasted_iota(jnp.int32, (t, t), 0)
             - lax.broadcasted_iota(jnp.int32, (t, t), 1)).astype(F32)
        base = -slope2 * d
        bias_sc[0] = base
        bias_sc[1] = jnp.where(d >= 0, base, NEG_INF)

    @pl.when(ki == 0)
    def _():
        for m_ref, l_ref, a_ref in ((m1_ref, l1_ref, a1_ref), (m2_ref, l2_ref, a2_ref)):
            m_ref[...] = jnp.full(m_ref.shape, NEG_INF, F32)
            l_ref[...] = jnp.zeros(l_ref.shape, F32)
            a_ref[...] = jnp.zeros(a_ref.shape, F32)

    block_bias = -slope2 * ((qi - ki) * t).astype(F32)
    q = (q_ref[...] * (DA_QK ** -0.5 * LOG2E)).astype(BF16)
    k = k_ref[...].astype(BF16)
    v = v_ref[...].astype(BF16)
    bias = bias_sc[(ki == qi).astype(jnp.int32)]
    for idx, (m_ref, l_ref, a_ref) in enumerate(((m1_ref, l1_ref, a1_ref), (m2_ref, l2_ref, a2_ref))):
        s = _dot_nt(q[:, idx * DA_QK:(idx + 1) * DA_QK], k[:, idx * DA_QK:(idx + 1) * DA_QK]) + bias
        m_prev = m_ref[...]
        m_new = jnp.maximum(m_prev, jnp.max(s, axis=-1, keepdims=True) + block_bias)
        corr = jnp.exp2(m_prev - m_new)
        p = jnp.exp2(s - jnp.tile(m_new - block_bias, (1, t // HEAD_W)))
        l_ref[...] = corr * l_ref[...] + jnp.sum(p, axis=-1, keepdims=True)
        a_ref[...] = corr * a_ref[...] + _dot(p.astype(BF16), v)
        m_ref[...] = m_new

    @pl.when(ki == qi)
    def _():
        lv = lam_ref[...]
        lam = (jnp.exp(jnp.sum(lv[0:1] * lv[1:2], axis=-1, keepdims=True))
               - jnp.exp(jnp.sum(lv[2:3] * lv[3:4], axis=-1, keepdims=True)) + lam_init)
        o = a1_ref[...] / l1_ref[...] - lam * (a2_ref[...] / l2_ref[...])
        ms = jnp.mean(o * o, axis=-1, keepdims=True)
        o = o * lax.rsqrt(ms + EPS) * g_ref[...] * (1.0 - lam_init)
        o_ref[...] = o.astype(o_ref.dtype)


def _diff_attention(proj3, slopes, lam_vec, g, lam_init, *, t=512):
    b, s, _ = proj3.shape
    t = _pick(s, t)
    n = s // t
    qi_list = [qi for qi in range(n) for _ in range(qi + 1)]
    ki_list = [ki for qi in range(n) for ki in range(qi + 1)]
    kern = functools.partial(_diff_attn_kernel, t=t, lam_init=lam_init)
    grid_spec = pltpu.PrefetchScalarGridSpec(
        num_scalar_prefetch=3,
        grid=(b, N_HEADS, len(qi_list)),
        in_specs=[
            pl.BlockSpec(lam_vec.shape, lambda bi, h, p, qa, ka, sl: (0, 0)),
            pl.BlockSpec((1, HEAD_W), lambda bi, h, p, qa, ka, sl: (0, 0)),
            pl.BlockSpec((None, t, HEAD_W), lambda bi, h, p, qa, ka, sl: (bi, qa[p], OFF_DA_Q + h)),
            pl.BlockSpec((None, t, HEAD_W), lambda bi, h, p, qa, ka, sl: (bi, ka[p], OFF_DA_K + h)),
            pl.BlockSpec((None, t, HEAD_W), lambda bi, h, p, qa, ka, sl: (bi, ka[p], OFF_DA_V + h)),
        ],
        out_specs=pl.BlockSpec((None, t, HEAD_W), lambda bi, h, p, qa, ka, sl: (bi, qa[p], h)),
        scratch_shapes=[pltpu.VMEM((t, HEAD_W), F32) for _ in range(6)] + [pltpu.VMEM((2, t, t), F32)],
    )
    return pl.pallas_call(
        kern,
        grid_spec=grid_spec,
        out_shape=jax.ShapeDtypeStruct((b, s, BRANCH_W), BF16),
        compiler_params=_params(("parallel", "parallel", "arbitrary")),
        name="diff_attention",
    )(jnp.asarray(qi_list, jnp.int32), jnp.asarray(ki_list, jnp.int32), slopes,
      lam_vec, g.reshape(1, HEAD_W), proj3, proj3, proj3)


def _retention_kernel(inner_ref, qd_ref, kd_ref, cd_ref, q_ref, k_ref, v_ref, g_ref, o_ref,
                      state_ref, *, rows):
    ci = pl.program_id(2)

    @pl.when(ci == 0)
    def _():
        state_ref[...] = jnp.zeros(state_ref.shape, F32)

    c = RET_CHUNK
    for p in range(2):
        inner = inner_ref[p]
        qdec = qd_ref[p]
        kdec = kd_ref[p]
        cdec = cd_ref[p]
        state = state_ref[p]
        for n in range(rows // c):
            r0 = n * c
            qc = (q_ref[r0:r0 + c, p * RET_QK:(p + 1) * RET_QK] * (RET_QK ** -0.5)).astype(BF16)
            kf = k_ref[r0:r0 + c, p * RET_QK:(p + 1) * RET_QK]
            vc = v_ref[r0:r0 + c, p * HEAD_W:(p + 1) * HEAD_W].astype(BF16)
            att = _dot_nt(qc, kf.astype(BF16)) * inner
            o = _dot(att.astype(BF16), vc) + _dot(qc, state.astype(BF16)) * qdec
            state = cdec * state + _dot_tn((kf * kdec).astype(BF16), vc)
            mu = jnp.mean(o, axis=-1, keepdims=True)
            oc = o - mu
            var = jnp.mean(oc * oc, axis=-1, keepdims=True)
            gate = g_ref[r0:r0 + c, p * HEAD_W:(p + 1) * HEAD_W]
            y = oc * lax.rsqrt(var + EPS) * (gate * jax.nn.sigmoid(gate))
            o_ref[r0:r0 + c, p * HEAD_W:(p + 1) * HEAD_W] = y.astype(o_ref.dtype)
        state_ref[p] = state


def _retention_tables():
    h = N_HEADS
    c = RET_CHUNK
    log_g = jnp.log1p(-(2.0 ** (-RET_DECAY_BASE - jnp.arange(h, dtype=F32))))
    idx = jnp.arange(c, dtype=F32)
    rel = idx[:, None] - idx[None, :]
    inner = jnp.where(rel >= 0, jnp.exp(log_g[:, None, None] * jnp.maximum(rel, 0.0)), 0.0)
    qd = jnp.broadcast_to(jnp.exp(log_g[:, None] * (idx + 1.0))[:, :, None], (h, c, HEAD_W))
    kd = jnp.broadcast_to(jnp.exp(log_g[:, None] * (c - 1.0 - idx))[:, :, None], (h, c, RET_QK))
    cd = jnp.broadcast_to(jnp.exp(log_g * c)[:, None, None], (h, RET_QK, HEAD_W))
    return inner, qd, kd, cd


def _retention(proj3, *, rows=512):
    b, s, _ = proj3.shape
    rows = _pick(s, rows)
    inner, qd, kd, cd = _retention_tables()
    c = RET_CHUNK
    return pl.pallas_call(
        functools.partial(_retention_kernel, rows=rows),
        grid=(b, N_HEADS // 2, s // rows),
        in_specs=[
            pl.BlockSpec((2, c, c), lambda bi, hp, ci: (hp, 0, 0)),
            pl.BlockSpec((2, c, HEAD_W), lambda bi, hp, ci: (hp, 0, 0)),
            pl.BlockSpec((2, c, RET_QK), lambda bi, hp, ci: (hp, 0, 0)),
            pl.BlockSpec((2, RET_QK, HEAD_W), lambda bi, hp, ci: (hp, 0, 0)),
            pl.BlockSpec((None, rows, 128), lambda bi, hp, ci: (bi, ci, OFF_RT_Q + hp)),
            pl.BlockSpec((None, rows, 128), lambda bi, hp, ci: (bi, ci, OFF_RT_K + hp)),
            pl.BlockSpec((None, rows, 2 * HEAD_W), lambda bi, hp, ci: (bi, ci, OFF_RT_V // 2 + hp)),
            pl.BlockSpec((None, rows, 2 * HEAD_W), lambda bi, hp, ci: (bi, ci, OFF_RT_G // 2 + hp)),
        ],
        out_specs=pl.BlockSpec((None, rows, 2 * HEAD_W), lambda bi, hp, ci: (bi, ci, hp)),
        out_shape=jax.ShapeDtypeStruct((b, s, BRANCH_W), BF16),
        scratch_shapes=[pltpu.VMEM((2, RET_QK, HEAD_W), F32)],
        compiler_params=_params(("parallel", "parallel", "arbitrary")),
        name="retention",
    )(inner, qd, kd, cd, proj3, proj3, proj3, proj3)


def _split3(x):
    hi = x.astype(BF16)
    r1 = x - hi.astype(F32)
    mid = r1.astype(BF16)
    lo = (r1 - mid.astype(F32)).astype(BF16)
    return hi, mid, lo


HG_SUB = 16


def _hgrn2_kernel(lb_ref, gn_ref, q_ref, f_ref, i_ref, gt_ref, o_ref,
                  state_ref, b_sc, k_sc, q_sc, *, rows):
    ci = pl.program_id(2)
    c, sc = HG_CHUNK, HG_SUB

    @pl.when(ci == 0)
    def _():
        state_ref[...] = jnp.zeros(state_ref.shape, F32)

    lb = lb_ref[...]
    gn = gn_ref[...]
    fr = f_ref[...]
    log_f = jnp.log(lb + (1.0 - lb) * jax.nn.sigmoid(fr))
    k_sc[...] = (1.0 - lb) * jax.nn.sigmoid(-fr)
    qraw = q_ref[...]
    q_sc[...] = qraw * jax.nn.sigmoid(qraw)
    ri = lax.broadcasted_iota(jnp.int32, (rows, rows), 0)
    cj = lax.broadcasted_iota(jnp.int32, (rows, rows), 1)
    tri = jnp.where(jnp.logical_and(ri >= cj, (ri // c) == (cj // c)), 1.0, 0.0).astype(BF16)
    hi, mid, lo = _split3(log_f)
    b_sc[...] = _dot(tri, hi) + _dot(tri, mid) + _dot(tri, lo)
    row_sub = lax.broadcasted_iota(jnp.int32, (sc, HEAD_W), 0)

    for n in range(rows // c):
        r0 = n * c
        parts = []
        for sub in range(c // sc):
            s0 = r0 + sub * sc
            b_i = b_sc[s0:s0 + sc, :]
            q_i = q_sc[s0:s0 + sc, :]
            acc = jnp.zeros((sc, HEAD_W), F32)
            for jj in range(sc):
                j = s0 + jj
                d = b_i - b_sc[j:j + 1, :]
                if jj > 0:
                    d = jnp.where(row_sub >= jj, d, NEG_INF)
                a = jnp.sum(q_i * k_sc[j:j + 1, :] * jnp.exp(d), axis=-1, keepdims=True)
                acc = acc + a * i_ref[j:j + 1, :]
            if sub > 0:
                anchor = b_sc[s0:s0 + 1, :]
                qs = (q_i * jnp.exp(b_i - anchor)).astype(BF16)
                ks = (k_sc[r0:s0, :] * jnp.exp(anchor - b_sc[r0:s0, :])).astype(BF16)
                att = _dot_nt(qs, ks)
                acc = acc + _dot(att.astype(BF16), i_ref[r0:s0, :].astype(BF16))
            parts.append(acc)
        o = jnp.concatenate(parts, axis=0)
        bc = b_sc[r0:r0 + c, :]
        state_t = state_ref[...]
        o = o + _dot_nt((q_sc[r0:r0 + c, :] * jnp.exp(bc)).astype(BF16), state_t.astype(BF16))
        blast = b_sc[r0 + c - 1:r0 + c, :]
        kd = (k_sc[r0:r0 + c, :] * jnp.exp(blast - bc)).astype(BF16)
        state_ref[...] = jnp.exp(blast) * state_t + _dot_tn(i_ref[r0:r0 + c, :].astype(BF16), kd)
        ms = jnp.mean(o * o, axis=-1, keepdims=True)
        gate = gt_ref[r0:r0 + c, :]
        y = o * lax.rsqrt(ms + EPS) * gn * (gate * jax.nn.sigmoid(gate))
        o_ref[r0:r0 + c, :] = y.astype(o_ref.dtype)


def _hgrn2(proj3, lb, gn, *, rows=256):
    b, s, _ = proj3.shape
    rows = _pick(s, rows)
    blk = lambda off: pl.BlockSpec((None, rows, HEAD_W), lambda bi, h, ci: (bi, ci, off + h))
    return pl.pallas_call(
        functools.partial(_hgrn2_kernel, rows=rows),
        grid=(b, N_HEADS, s // rows),
        in_specs=[
            pl.BlockSpec((None, 1, HEAD_W), lambda bi, h, ci: (h, 0, 0)),
            pl.BlockSpec((1, HEAD_W), lambda bi, h, ci: (0, 0)),
            blk(OFF_HG_Q), blk(OFF_HG_F), blk(OFF_HG_I), blk(OFF_HG_G),
        ],
        out_specs=pl.BlockSpec((None, rows, HEAD_W), lambda bi, h, ci: (bi, ci, h)),
        out_shape=jax.ShapeDtypeStruct((b, s, BRANCH_W), BF16),
        scratch_shapes=[pltpu.VMEM((HEAD_W, HEAD_W), F32)] + [pltpu.VMEM((rows, HEAD_W), F32)] * 3,
        compiler_params=_params(("parallel", "parallel", "arbitrary")),
        name="hgrn2",
    )(lb.reshape(N_HEADS, 1, HEAD_W), gn.reshape(1, HEAD_W), proj3, proj3, proj3, proj3)


def _dil_kernel(slope_ref, *refs, tile):
    n_g = len(DIL_RATES)
    groups = [refs[5 * g:5 * g + 5] for g in range(n_g)]
    o_ref, o_sc, l_sc = refs[5 * n_g:]
    h = pl.program_id(1)
    blk = pl.program_id(2)
    t = DIL_BLOCK
    d_cur = (lax.broadcasted_iota(jnp.int32, (t, t), 0)
             - lax.broadcasted_iota(jnp.int32, (t, t), 1)).astype(F32)
    d_prev = d_cur + float(t)

    for g, (q_ref, k_ref, v_ref, kh_ref, vh_ref) in enumerate(groups):
        rate = DIL_RATES[g]
        halo = t * rate
        slope = slope_ref[g * N_HEADS + h] * float(rate)
        bias_cur = jnp.where(d_cur >= 0, -slope * d_cur, NEG_INF)
        bias_prev = jnp.where(d_prev <= float(t), -slope * d_prev, NEG_INF)

        def rows(start, rate=rate):
            return pl.ds(start, t) if rate == 1 else pl.ds(start, t, stride=rate)

        def unit(u, carry, g=g, rate=rate, halo=halo, q_ref=q_ref, k_ref=k_ref, v_ref=v_ref,
                 kh_ref=kh_ref, vh_ref=vh_ref, bias_cur=bias_cur, bias_prev=bias_prev, rows=rows):
            r = u % rate
            mb = u // rate
            start = mb * halo + r
            q = (q_ref[rows(start), :] * (HEAD_W ** -0.5)).astype(BF16)
            kc = k_ref[rows(start), :].astype(BF16)
            vc = v_ref[rows(start), :].astype(BF16)
            kp = kh_ref[rows(r), :]
            vp = vh_ref[rows(r), :]
            prev_ok = blk > 0
            if tile > halo:
                pstart = jnp.maximum(start - halo, r)
                in_tile = mb > 0
                kp = jnp.where(in_tile, k_ref[rows(pstart), :], kp)
                vp = jnp.where(in_tile, v_ref[rows(pstart), :], vp)
                prev_ok = jnp.logical_or(prev_ok, in_tile)
            s_cur = _dot_nt(q, kc) + bias_cur
            s_prev = _dot_nt(q, kp.astype(BF16)) + jnp.where(prev_ok, bias_prev, NEG_INF)
            m = jnp.maximum(jnp.max(s_cur, axis=-1, keepdims=True),
                            jnp.max(s_prev, axis=-1, keepdims=True))
            p_cur = jnp.exp(s_cur - m)
            p_prev = jnp.exp(s_prev - m)
            l = jnp.sum(p_cur, axis=-1, keepdims=True) + jnp.sum(p_prev, axis=-1, keepdims=True)
            acc = _dot(p_cur.astype(BF16), vc) + _dot(p_prev.astype(BF16), vp.astype(BF16))
            o = acc / l
            lse = jnp.broadcast_to(m + jnp.log(l), (t, HEAD_W))
            if g > 0:
                lse_p = l_sc[rows(start), :]
                lse_n = jnp.maximum(lse, lse_p)
                lse_n = lse_n + jnp.log(jnp.exp(lse - lse_n) + jnp.exp(lse_p - lse_n))
                o = o_sc[rows(start), :] * jnp.exp(lse_p - lse_n) + o * jnp.exp(lse - lse_n)
                lse = lse_n
            o_sc[rows(start), :] = o
            if g < n_g - 1:
                l_sc[rows(start), :] = lse
            return carry

        lax.fori_loop(0, tile // t, unit, 0)

    o_ref[...] = o_sc[...].astype(o_ref.dtype)


def _dilated_attention(proj3, slopes4):
    b, s, _ = proj3.shape
    t = DIL_BLOCK
    tile = t * max(DIL_RATES)
    assert s % tile == 0
    in_specs, args = [], []
    for g, rate in enumerate(DIL_RATES):
        halo = t * rate
        per = tile // halo

        def cur(off, g=g):
            return pl.BlockSpec((None, tile, HEAD_W),
                                lambda bi, h, blk, sl: (bi, blk, off + g * N_HEADS + h))

        def front(off, g=g, halo=halo, per=per):
            return pl.BlockSpec((None, halo, HEAD_W),
                                lambda bi, h, blk, sl: (bi, jnp.maximum(blk * per - 1, 0), off + g * N_HEADS + h))

        in_specs += [cur(OFF_DL_Q), cur(OFF_DL_K), cur(OFF_DL_V), front(OFF_DL_K), front(OFF_DL_V)]
        args += [proj3] * 5
    grid_spec = pltpu.PrefetchScalarGridSpec(
        num_scalar_prefetch=1,
        grid=(b, N_HEADS, s // tile),
        in_specs=in_specs,
        out_specs=pl.BlockSpec((None, tile, HEAD_W), lambda bi, h, blk, sl: (bi, blk, h)),
        scratch_shapes=[pltpu.VMEM((tile, HEAD_W), F32), pltpu.VMEM((tile, HEAD_W), F32)],
    )
    return pl.pallas_call(
        functools.partial(_dil_kernel, tile=tile),
        grid_spec=grid_spec,
        out_shape=jax.ShapeDtypeStruct((b, s, BRANCH_W), BF16),
        compiler_params=_params(("parallel", "parallel", "parallel")),
        name="dilated",
    )(slopes4[:, 1:].T.reshape(-1), *args)


def _merge_kernel(ya_ref, yr_ref, yc_ref, yd_ref, w_ref, g0_ref, g1_ref, g2_ref, g3_ref, o_ref):
    acc = None
    for n, (y_ref, g_ref) in enumerate(((ya_ref, g0_ref), (yr_ref, g1_ref), (yc_ref, g2_ref), (yd_ref, g3_ref))):
        t = jax.nn.sigmoid(g_ref[...]) * _dot(y_ref[...], w_ref[n])
        acc = t if acc is None else acc + t
    o_ref[...] = acc.astype(o_ref.dtype)


def _merge(ys, w_branch, proj, d_model, *, tm=512, tn=512):
    m = proj.shape[0]
    tm, tn = _pick(m, tm), _pick(d_model, tn)
    gcols = d_model // tn
    goff = OFF_GATES * 128 // tn
    y_spec = pl.BlockSpec((tm, BRANCH_W), lambda j, i: (i, 0))

    def gate_spec(n):
        return pl.BlockSpec((tm, tn), lambda j, i: (i, goff + n * gcols + j))

    return pl.pallas_call(
        _merge_kernel,
        grid=(d_model // tn, m // tm),
        in_specs=[y_spec, y_spec, y_spec, y_spec,
                  pl.BlockSpec((N_BRANCH, BRANCH_W, tn), lambda j, i: (0, 0, j)),
                  gate_spec(0), gate_spec(1), gate_spec(2), gate_spec(3)],
        out_specs=pl.BlockSpec((tm, tn), lambda j, i: (i, j)),
        out_shape=jax.ShapeDtypeStruct((m, d_model), BF16),
        compiler_params=_params(("parallel", "parallel")),
        name="merge",
    )(*ys, w_branch, proj, proj, proj, proj)


def _convgate_kernel(a_ref, ap_ref, gl_ref, cw_ref, cb_ref, o_ref, *, ts):
    ti = pl.program_id(1)
    a = a_ref[...]
    prev = ap_ref[...]
    prev = jnp.where(ti > 0, prev, jnp.zeros_like(prev))
    row = lax.broadcasted_iota(jnp.int32, a.shape, 0)
    a1 = jnp.where(row == 0, prev[7:8, :], pltpu.roll(a, 1, 0))
    a2 = jnp.where(row == 0, prev[6:7, :],
                   jnp.where(row == 1, prev[7:8, :], pltpu.roll(a, 2, 0)))
    cw = cw_ref[...]
    conv = cb_ref[...] + cw[0:1, :] * a2 + cw[1:2, :] * a1 + cw[2:3, :] * a
    gelu = 0.5 * conv * (1.0 + lax.erf(conv * (2.0 ** -0.5)))
    o_ref[...] = (gelu * gl_ref[...]).astype(o_ref.dtype)


def _convgate(u3, conv_w, conv_b, *, ts=512, tc=512):
    b, s, two_ff = u3.shape
    d_ff = two_ff // 2
    ts, tc = _pick(s, ts), _pick(d_ff, tc)
    nc = d_ff // tc
    return pl.pallas_call(
        functools.partial(_convgate_kernel, ts=ts),
        grid=(b, s // ts, nc),
        in_specs=[
            pl.BlockSpec((None, ts, tc), lambda bi, ti, j: (bi, ti, j)),
            pl.BlockSpec((None, 8, tc), lambda bi, ti, j: (bi, jnp.maximum(ti * (ts // 8) - 1, 0), j)),
            pl.BlockSpec((None, ts, tc), lambda bi, ti, j: (bi, ti, nc + j)),
            pl.BlockSpec((CONV_W, tc), lambda bi, ti, j: (0, j)),
            pl.BlockSpec((1, tc), lambda bi, ti, j: (0, j)),
        ],
        out_specs=pl.BlockSpec((None, ts, tc), lambda bi, ti, j: (bi, ti, j)),
        out_shape=jax.ShapeDtypeStruct((b, s, d_ff), BF16),
        compiler_params=_params(("parallel", "parallel", "parallel")),
        name="convgate",
    )(u3, u3, u3, conv_w, conv_b.reshape(1, d_ff))


def _alibi_slopes():
    n = 4 * N_HEADS
    slopes = 2.0 ** (-8.0 * jnp.arange(1, n + 1, dtype=F32) / n)
    return slopes.reshape(N_HEADS, 4)


def kernel(x, w_in, diff_lambda, diff_norm_g, hgrn_lb_logits, hgrn_norm_g, w_branch, w_o,
           ln1_g, ln1_b, w_up, conv_w, conv_b, w_down, ln2_g, ln2_b):
    b, s, d = x.shape
    depth = w_in.shape[0]
    m = b * s
    alpha = (2 * depth) ** 0.25
    slopes4 = _alibi_slopes()
    p = jax.nn.softmax(hgrn_lb_logits.astype(F32), axis=0)
    lower_bounds = jnp.cumsum(p, axis=0) - p[0]

    h = x.reshape(m, d)
    hb = h.astype(BF16)
    for l in range(depth):
        lam_init = 0.8 - 0.6 * math.exp(-0.3 * l)
        proj = _matmul(hb, w_in[l].astype(BF16), out_dtype=F32, name="in_proj")
        proj3 = proj.reshape(b, s, -1)
        ya = _diff_attention(proj3, slopes4[:, 0], diff_lambda[l], diff_norm_g[l], lam_init)
        yr = _retention(proj3)
        yc = _hgrn2(proj3, lower_bounds[l], hgrn_norm_g[l])
        yd = _dilated_attention(proj3, slopes4)
        ys = [y.reshape(m, BRANCH_W) for y in (ya, yr, yc, yd)]
        merged = _merge(ys, w_branch[l].astype(BF16), proj, d)
        z = _matmul(merged, w_o[l].astype(BF16), out_dtype=F32, res=h, alpha=alpha, name="out_proj")
        h, hb = _layernorm(z, ln1_g[l], ln1_b[l])
        u = _matmul(hb, w_up[l].astype(BF16), out_dtype=F32, name="ffn_up")
        act = _convgate(u.reshape(b, s, -1), conv_w[l], conv_b[l])
        z = _matmul(act.reshape(m, -1), w_down[l].astype(BF16), out_dtype=F32, tn=512, res=h, alpha=alpha,
                    name="ffn_down")
        h, hb = _layernorm(z, ln2_g[l], ln2_b[l])
    return h.reshape(b, s, d)
```

```python
import functools
import math

import jax
import jax.numpy as jnp
from jax import lax
from jax.experimental import pallas as pl
from jax.experimental.pallas import tpu as pltpu

F32 = jnp.float32
BF16 = jnp.bfloat16

N_HEADS = 8
HEAD_W = 128
BRANCH_W = N_HEADS * HEAD_W
N_BRANCH = 4
DA_QK = 64
RET_QK = 64
RET_CHUNK = 128
RET_DECAY_BASE = 5.0
HG_CHUNK = 64
DIL_WINDOWS = (128, 512, 2048)
DIL_RATES = (1, 4, 16)
DIL_KEYS = 129
DIL_BLOCK = 128
DIL_UNROLL = 4
CONV_W = 3
EPS = 1e-5
NEG_INF = float("-inf")

_C = BRANCH_W // 128
OFF_DA_Q, OFF_DA_K, OFF_DA_V = 0, _C, 2 * _C
OFF_RT_Q, OFF_RT_K = 3 * _C, 3 * _C + _C // 2
OFF_RT_V, OFF_RT_G = 4 * _C, 5 * _C
OFF_HG_Q, OFF_HG_F, OFF_HG_I, OFF_HG_G = 6 * _C, 7 * _C, 8 * _C, 9 * _C
OFF_DL_Q, OFF_DL_K, OFF_DL_V = 10 * _C, 13 * _C, 16 * _C
OFF_GATES = 19 * _C

VMEM_LIMIT = 56 * 1024 * 1024


def _params(sem):
    return pltpu.CompilerParams(dimension_semantics=sem, vmem_limit_bytes=VMEM_LIMIT)


def _dot(a, b):
    return jnp.dot(a, b, preferred_element_type=F32)


def _dot_nt(a, b):
    return lax.dot_general(a, b, (((1,), (1,)), ((), ())), preferred_element_type=F32)


def _dot_tn(a, b):
    return lax.dot_general(a, b, (((0,), (0,)), ((), ())), preferred_element_type=F32)


def _pick(dim, pref):
    if dim <= pref:
        return dim
    t = pref
    while t >= 128:
        if dim % t == 0 and t % 128 == 0:
            return t
        t -= 128
    return dim


CAST_ROWS = 512


def _cast_weights(w_ref, wb_ref):
    rows = w_ref.shape[-2]
    step = min(rows, CAST_ROWS)
    for r0 in range(0, rows, step):
        r1 = min(rows, r0 + step)
        wb_ref[..., r0:r1, :] = w_ref[..., r0:r1, :].astype(BF16)


def _mm_kernel(*refs, alpha, has_res):
    if has_res:
        x_ref, w_ref, r_ref, o_ref, wb_ref = refs
    else:
        x_ref, w_ref, o_ref, wb_ref = refs

    @pl.when(pl.program_id(1) == 0)
    def _():
        _cast_weights(w_ref, wb_ref)

    acc = _dot(x_ref[...], wb_ref[...])
    if has_res:
        acc = alpha * r_ref[...] + acc
    o_ref[...] = acc.astype(o_ref.dtype)


def _matmul(x, w, layer, *, out_dtype, tm=512, tn=1024, res=None, alpha=1.0, name="mm"):
    m, kdim = x.shape
    n = w.shape[-1]
    tm, tn = _pick(m, tm), _pick(n, tn)
    in_specs = [
        pl.BlockSpec((tm, kdim), lambda j, i: (i, 0)),
        pl.BlockSpec((None, kdim, tn), lambda j, i: (layer, 0, j), pipeline_mode=pl.Buffered(1)),
    ]
    args = [x, w]
    if res is not None:
        in_specs.append(pl.BlockSpec((tm, tn), lambda j, i: (i, j)))
        args.append(res)
    return pl.pallas_call(
        functools.partial(_mm_kernel, alpha=alpha, has_res=res is not None),
        grid=(n // tn, m // tm),
        in_specs=in_specs,
        out_specs=pl.BlockSpec((tm, tn), lambda j, i: (i, j)),
        out_shape=jax.ShapeDtypeStruct((m, n), out_dtype),
        scratch_shapes=[pltpu.VMEM((kdim, tn), BF16)],
        compiler_params=_params(("parallel", "arbitrary")),
        name=name,
    )(*args)


def _ln_kernel(x_ref, g_ref, b_ref, o_ref, ob_ref):
    x = x_ref[...]
    mu = jnp.mean(x, axis=-1, keepdims=True)
    xc = x - mu
    var = jnp.mean(xc * xc, axis=-1, keepdims=True)
    y = xc * lax.rsqrt(var + EPS) * g_ref[...] + b_ref[...]
    o_ref[...] = y
    ob_ref[...] = y.astype(BF16)


def _layernorm(x, g, b, *, tm=256):
    m, d = x.shape
    tm = _pick(m, tm)
    return pl.pallas_call(
        _ln_kernel,
        grid=(m // tm,),
        in_specs=[
            pl.BlockSpec((tm, d), lambda i: (i, 0)),
            pl.BlockSpec((1, d), lambda i: (0, 0)),
            pl.BlockSpec((1, d), lambda i: (0, 0)),
        ],
        out_specs=[pl.BlockSpec((tm, d), lambda i: (i, 0)), pl.BlockSpec((tm, d), lambda i: (i, 0))],
        out_shape=[jax.ShapeDtypeStruct((m, d), F32), jax.ShapeDtypeStruct((m, d), BF16)],
        compiler_params=_params(("parallel",)),
        name="layernorm",
    )(x, g.reshape(1, d), b.reshape(1, d))


LOG2E = 1.4426950408889634


def _diff_attn_kernel(qi_ref, ki_ref, slope_ref, lam_ref, g_ref, q_ref, k_ref, v_ref, o_ref,
                      m1_ref, l1_ref, a1_ref, m2_ref, l2_ref, a2_ref, bias_sc, *, t, lam_init):
    h = pl.program_id(1)
    step = pl.program_id(2)
    qi = qi_ref[step]
    ki = ki_ref[step]
    slope2 = slope_ref[h] * LOG2E

    @pl.when(step == 0)
    def _():
        d = (lax.broadcasted_iota(jnp.int32, (t, t), 0)
             - lax.broadcasted_iota(jnp.int32, (t, t), 1)).astype(F32)
        base = -slope2 * d
        bias_sc[0] = base
        bias_sc[1] = jnp.where(d >= 0, base, NEG_INF)

    @pl.when(ki == 0)
    def _():
        for m_ref, l_ref, a_ref in ((m1_ref, l1_ref, a1_ref), (m2_ref, l2_ref, a2_ref)):
            m_ref[...] = jnp.full(m_ref.shape, NEG_INF, F32)
            l_ref[...] = jnp.zeros(l_ref.shape, F32)
            a_ref[...] = jnp.zeros(a_ref.shape, F32)

    block_bias = -slope2 * ((qi - ki) * t).astype(F32)
    q = (q_ref[...] * (DA_QK ** -0.5 * LOG2E)).astype(BF16)
    k = k_ref[...].astype(BF16)
    v = v_ref[...].astype(BF16)
    bias = bias_sc[(ki == qi).astype(jnp.int32)]
    for idx, (m_ref, l_ref, a_ref) in enumerate(((m1_ref, l1_ref, a1_ref), (m2_ref, l2_ref, a2_ref))):
        s = _dot_nt(q[:, idx * DA_QK:(idx + 1) * DA_QK], k[:, idx * DA_QK:(idx + 1) * DA_QK]) + bias
        m_prev = m_ref[...]
        m_new = jnp.maximum(m_prev, jnp.max(s, axis=-1, keepdims=True) + block_bias)
        corr = jnp.exp2(m_prev - m_new)
        p = jnp.exp2(s - jnp.tile(m_new - block_bias, (1, t // HEAD_W)))
        l_ref[...] = corr * l_ref[...] + jnp.sum(p, axis=-1, keepdims=True)
        a_ref[...] = corr * a_ref[...] + _dot(p.astype(BF16), v)
        m_ref[...] = m_new

    @pl.when(ki == qi)
    def _():
        lv = lam_ref[...]
        lam = (jnp.exp(jnp.sum(lv[0:1] * lv[1:2], axis=-1, keepdims=True))
               - jnp.exp(jnp.sum(lv[2:3] * lv[3:4], axis=-1, keepdims=True)) + lam_init)
        o = a1_ref[...] / l1_ref[...] - lam * (a2_ref[...] / l2_ref[...])
        ms = jnp.mean(o * o, axis=-1, keepdims=True)
        o = o * lax.rsqrt(ms + EPS) * g_ref[...] * (1.0 - lam_init)
        o_ref[...] = o.astype(o_ref.dtype)


def _diff_attention(proj3, slopes, lam_vec, g, lam_init, *, t=512):
    b, s, _ = proj3.shape
    t = _pick(s, t)
    n = s // t
    qi_list = [qi for qi in range(n) for _ in range(qi + 1)]
    ki_list = [ki for qi in range(n) for ki in range(qi + 1)]
    kern = functools.partial(_diff_attn_kernel, t=t, lam_init=lam_init)
    grid_spec = pltpu.PrefetchScalarGridSpec(
        num_scalar_prefetch=3,
        grid=(b, N_HEADS, len(qi_list)),
        in_specs=[
            pl.BlockSpec(lam_vec.shape, lambda bi, h, p, qa, ka, sl: (0, 0)),
            pl.BlockSpec((1, HEAD_W), lambda bi, h, p, qa, ka, sl: (0, 0)),
            pl.BlockSpec((None, t, HEAD_W), lambda bi, h, p, qa, ka, sl: (bi, qa[p], OFF_DA_Q + h)),
            pl.BlockSpec((None, t, HEAD_W), lambda bi, h, p, qa, ka, sl: (bi, ka[p], OFF_DA_K + h)),
            pl.BlockSpec((None, t, HEAD_W), lambda bi, h, p, qa, ka, sl: (bi, ka[p], OFF_DA_V + h)),
        ],
        out_specs=pl.BlockSpec((None, t, HEAD_W), lambda bi, h, p, qa, ka, sl: (bi, qa[p], h)),
        scratch_shapes=[pltpu.VMEM((t, HEAD_W), F32) for _ in range(6)] + [pltpu.VMEM((2, t, t), F32)],
    )
    return pl.pallas_call(
        kern,
        grid_spec=grid_spec,
        out_shape=jax.ShapeDtypeStruct((b, s, BRANCH_W), BF16),
        compiler_params=_params(("parallel", "parallel", "arbitrary")),
        name="diff_attention",
    )(jnp.asarray(qi_list, jnp.int32), jnp.asarray(ki_list, jnp.int32), slopes,
      lam_vec, g.reshape(1, HEAD_W), proj3, proj3, proj3)


def _retention_kernel(inner_ref, qd_ref, kd_ref, cd_ref, q_ref, k_ref, v_ref, g_ref, o_ref,
                      state_ref, *, rows):
    ci = pl.program_id(2)

    @pl.when(ci == 0)
    def _():
        state_ref[...] = jnp.zeros(state_ref.shape, F32)

    c = RET_CHUNK
    for p in range(2):
        inner = inner_ref[p]
        qdec = qd_ref[p]
        kdec = kd_ref[p]
        cdec = cd_ref[p]
        state = state_ref[p]
        for n in range(rows // c):
            r0 = n * c
            qc = (q_ref[r0:r0 + c, p * RET_QK:(p + 1) * RET_QK] * (RET_QK ** -0.5)).astype(BF16)
            kf = k_ref[r0:r0 + c, p * RET_QK:(p + 1) * RET_QK]
            vc = v_ref[r0:r0 + c, p * HEAD_W:(p + 1) * HEAD_W].astype(BF16)
            att = _dot_nt(qc, kf.astype(BF16)) * inner
            o = _dot(att.astype(BF16), vc) + _dot(qc, state.astype(BF16)) * qdec
            state = cdec * state + _dot_tn((kf * kdec).astype(BF16), vc)
            mu = jnp.mean(o, axis=-1, keepdims=True)
            oc = o - mu
            var = jnp.mean(oc * oc, axis=-1, keepdims=True)
            gate = g_ref[r0:r0 + c, p * HEAD_W:(p + 1) * HEAD_W]
            y = oc * lax.rsqrt(var + EPS) * (gate * jax.nn.sigmoid(gate))
            o_ref[r0:r0 + c, p * HEAD_W:(p + 1) * HEAD_W] = y.astype(o_ref.dtype)
        state_ref[p] = state


def _retention_tables():
    h = N_HEADS
    c = RET_CHUNK
    log_g = jnp.log1p(-(2.0 ** (-RET_DECAY_BASE - jnp.arange(h, dtype=F32))))
    idx = jnp.arange(c, dtype=F32)
    rel = idx[:, None] - idx[None, :]
    inner = jnp.where(rel >= 0, jnp.exp(log_g[:, None, None] * jnp.maximum(rel, 0.0)), 0.0)
    qd = jnp.broadcast_to(jnp.exp(log_g[:, None] * (idx + 1.0))[:, :, None], (h, c, HEAD_W))
    kd = jnp.broadcast_to(jnp.exp(log_g[:, None] * (c - 1.0 - idx))[:, :, None], (h, c, RET_QK))
    cd = jnp.broadcast_to(jnp.exp(log_g * c)[:, None, None], (h, RET_QK, HEAD_W))
    return inner, qd, kd, cd


def _retention(proj3, *, rows=512):
    b, s, _ = proj3.shape
    rows = _pick(s, rows)
    inner, qd, kd, cd = _retention_tables()
    c = RET_CHUNK
    return pl.pallas_call(
        functools.partial(_retention_kernel, rows=rows),
        grid=(b, N_HEADS // 2, s // rows),
        in_specs=[
            pl.BlockSpec((2, c, c), lambda bi, hp, ci: (hp, 0, 0)),
            pl.BlockSpec((2, c, HEAD_W), lambda bi, hp, ci: (hp, 0, 0)),
            pl.BlockSpec((2, c, RET_QK), lambda bi, hp, ci: (hp, 0, 0)),
            pl.BlockSpec((2, RET_QK, HEAD_W), lambda bi, hp, ci: (hp, 0, 0)),
            pl.BlockSpec((None, rows, 128), lambda bi, hp, ci: (bi, ci, OFF_RT_Q + hp)),
            pl.BlockSpec((None, rows, 128), lambda bi, hp, ci: (bi, ci, OFF_RT_K + hp)),
            pl.BlockSpec((None, rows, 2 * HEAD_W), lambda bi, hp, ci: (bi, ci, OFF_RT_V // 2 + hp)),
            pl.BlockSpec((None, rows, 2 * HEAD_W), lambda bi, hp, ci: (bi, ci, OFF_RT_G // 2 + hp)),
        ],
        out_specs=pl.BlockSpec((None, rows, 2 * HEAD_W), lambda bi, hp, ci: (bi, ci, hp)),
        out_shape=jax.ShapeDtypeStruct((b, s, BRANCH_W), BF16),
        scratch_shapes=[pltpu.VMEM((2, RET_QK, HEAD_W), F32)],
        compiler_params=_params(("parallel", "parallel", "arbitrary")),
        name="retention",
    )(inner, qd, kd, cd, proj3, proj3, proj3, proj3)


def _split3(x):
    hi = x.astype(BF16)
    r1 = x - hi.astype(F32)
    mid = r1.astype(BF16)
    lo = (r1 - mid.astype(F32)).astype(BF16)
    return hi, mid, lo


HG_SUB = 16


def _hgrn2_kernel(lb_ref, gn_ref, q_ref, f_ref, i_ref, gt_ref, o_ref,
                  state_ref, b_sc, k_sc, q_sc, *, rows):
    ci = pl.program_id(2)
    c, sc = HG_CHUNK, HG_SUB

    @pl.when(ci == 0)
    def _():
        state_ref[...] = jnp.zeros(state_ref.shape, F32)

    lb = lb_ref[...]
    gn = gn_ref[...]
    fr = f_ref[...]
    log_f = jnp.log(lb + (1.0 - lb) * jax.nn.sigmoid(fr))
    k_sc[...] = (1.0 - lb) * jax.nn.sigmoid(-fr)
    qraw = q_ref[...]
    q_sc[...] = qraw * jax.nn.sigmoid(qraw)
    ri = lax.broadcasted_iota(jnp.int32, (rows, rows), 0)
    cj = lax.broadcasted_iota(jnp.int32, (rows, rows), 1)
    tri = jnp.where(jnp.logical_and(ri >= cj, (ri // c) == (cj // c)), 1.0, 0.0).astype(BF16)
    hi, mid, lo = _split3(log_f)
    b_sc[...] = _dot(tri, hi) + _dot(tri, mid) + _dot(tri, lo)
    row_sub = lax.broadcasted_iota(jnp.int32, (sc, HEAD_W), 0)

    for n in range(rows // c):
        r0 = n * c
        parts = []
        for sub in range(c // sc):
            s0 = r0 + sub * sc
            b_i = b_sc[s0:s0 + sc, :]
            q_i = q_sc[s0:s0 + sc, :]
            acc = jnp.zeros((sc, HEAD_W), F32)
            for jj in range(sc):
                j = s0 + jj
                d = b_i - b_sc[j:j + 1, :]
                if jj > 0:
                    d = jnp.where(row_sub >= jj, d, NEG_INF)
                a = jnp.sum(q_i * k_sc[j:j + 1, :] * jnp.exp(d), axis=-1, keepdims=True)
                acc = acc + a * i_ref[j:j + 1, :]
            if sub > 0:
                anchor = b_sc[s0:s0 + 1, :]
                qs = (q_i * jnp.exp(b_i - anchor)).astype(BF16)
                ks = (k_sc[r0:s0, :] * jnp.exp(anchor - b_sc[r0:s0, :])).astype(BF16)
                att = _dot_nt(qs, ks)
                acc = acc + _dot(att.astype(BF16), i_ref[r0:s0, :].astype(BF16))
            parts.append(acc)
        o = jnp.concatenate(parts, axis=0)
        bc = b_sc[r0:r0 + c, :]
        state_t = state_ref[...]
        o = o + _dot_nt((q_sc[r0:r0 + c, :] * jnp.exp(bc)).astype(BF16), state_t.astype(BF16))
        blast = b_sc[r0 + c - 1:r0 + c, :]
        kd = (k_sc[r0:r0 + c, :] * jnp.exp(blast - bc)).astype(BF16)
        state_ref[...] = jnp.exp(blast) * state_t + _dot_tn(i_ref[r0:r0 + c, :].astype(BF16), kd)
        ms = jnp.mean(o * o, axis=-1, keepdims=True)
        gate = gt_ref[r0:r0 + c, :]
        y = o * lax.rsqrt(ms + EPS) * gn * (gate * jax.nn.sigmoid(gate))
        o_ref[r0:r0 + c, :] = y.astype(o_ref.dtype)


def _hgrn2(proj3, lb, gn, *, rows=256):
    b, s, _ = proj3.shape
    rows = _pick(s, rows)
    blk = lambda off: pl.BlockSpec((None, rows, HEAD_W), lambda bi, h, ci: (bi, ci, off + h))
    return pl.pallas_call(
        functools.partial(_hgrn2_kernel, rows=rows),
        grid=(b, N_HEADS, s // rows),
        in_specs=[
            pl.BlockSpec((None, 1, HEAD_W), lambda bi, h, ci: (h, 0, 0)),
            pl.BlockSpec((1, HEAD_W), lambda bi, h, ci: (0, 0)),
            blk(OFF_HG_Q), blk(OFF_HG_F), blk(OFF_HG_I), blk(OFF_HG_G),
        ],
        out_specs=pl.BlockSpec((None, rows, HEAD_W), lambda bi, h, ci: (bi, ci, h)),
        out_shape=jax.ShapeDtypeStruct((b, s, BRANCH_W), BF16),
        scratch_shapes=[pltpu.VMEM((HEAD_W, HEAD_W), F32)] + [pltpu.VMEM((rows, HEAD_W), F32)] * 3,
        compiler_params=_params(("parallel", "parallel", "arbitrary")),
        name="hgrn2",
    )(lb.reshape(N_HEADS, 1, HEAD_W), gn.reshape(1, HEAD_W), proj3, proj3, proj3, proj3)


def _dil_kernel(slope_ref, *refs, tile):
    n_g = len(DIL_RATES)
    groups = [refs[5 * g:5 * g + 5] for g in range(n_g)]
    o_ref, o_sc, l_sc = refs[5 * n_g:]
    h = pl.program_id(1)
    blk = pl.program_id(2)
    t = DIL_BLOCK
    d_cur = (lax.broadcasted_iota(jnp.int32, (t, t), 0)
             - lax.broadcasted_iota(jnp.int32, (t, t), 1)).astype(F32)
    d_prev = d_cur + float(t)

    for g, (q_ref, k_ref, v_ref, kh_ref, vh_ref) in enumerate(groups):
        rate = DIL_RATES[g]
        halo = t * rate
        slope = slope_ref[g * N_HEADS + h] * float(rate)
        bias_cur = jnp.where(d_cur >= 0, -slope * d_cur, NEG_INF)
        bias_prev = jnp.where(d_prev <= float(t), -slope * d_prev, NEG_INF)
        bias_both = jnp.concatenate([bias_prev, bias_cur], axis=1)
        bias_first = jnp.concatenate([jnp.full((t, t), NEG_INF, F32), bias_cur], axis=1)

        def rows(start, rate=rate):
            return pl.ds(start, t) if rate == 1 else pl.ds(start, t, stride=rate)

        def units(ug, carry, g=g, rate=rate, halo=halo, q_ref=q_ref, k_ref=k_ref, v_ref=v_ref,
                  kh_ref=kh_ref, vh_ref=vh_ref, bias_both=bias_both, bias_first=bias_first, rows=rows):
            starts, qs, ks, vs, biases = [], [], [], [], []
            for uu in range(DIL_UNROLL):
                u = ug * DIL_UNROLL + uu
                r = u % rate
                mb = u // rate
                start = mb * halo + r
                starts.append(start)
                qs.append((q_ref[rows(start), :] * (HEAD_W ** -0.5)).astype(BF16))
                kp = kh_ref[rows(r), :]
                vp = vh_ref[rows(r), :]
                prev_ok = blk > 0
                if tile > halo:
                    pstart = jnp.maximum(start - halo, r)
                    in_tile = mb > 0
                    kp = jnp.where(in_tile, k_ref[rows(pstart), :], kp)
                    vp = jnp.where(in_tile, v_ref[rows(pstart), :], vp)
                    prev_ok = jnp.logical_or(prev_ok, in_tile)
                ks.append(jnp.concatenate([kp, k_ref[rows(start), :]], axis=0).astype(BF16))
                vs.append(jnp.concatenate([vp, v_ref[rows(start), :]], axis=0).astype(BF16))
                biases.append(jnp.where(prev_ok, bias_both, bias_first))
            ss = [_dot_nt(q, k) + bias for q, k, bias in zip(qs, ks, biases)]
            ms = [jnp.max(s, axis=-1, keepdims=True) for s in ss]
            ps = [jnp.exp(s - m) for s, m in zip(ss, ms)]
            ls = [jnp.sum(p, axis=-1, keepdims=True) for p in ps]
            accs = [_dot(p.astype(BF16), v) for p, v in zip(ps, vs)]
            for start, m, l, acc in zip(starts, ms, ls, accs):
                o = acc / l
                lse = jnp.broadcast_to(m + jnp.log(l), (t, HEAD_W))
                if g > 0:
                    lse_p = l_sc[rows(start), :]
                    lse_n = jnp.maximum(lse, lse_p)
                    lse_n = lse_n + jnp.log(jnp.exp(lse - lse_n) + jnp.exp(lse_p - lse_n))
                    o = o_sc[rows(start), :] * jnp.exp(lse_p - lse_n) + o * jnp.exp(lse - lse_n)
                    lse = lse_n
                o_sc[rows(start), :] = o
                if g < n_g - 1:
                    l_sc[rows(start), :] = lse
            return carry

        lax.fori_loop(0, tile // t // DIL_UNROLL, units, 0)

    o_ref[...] = o_sc[...].astype(o_ref.dtype)


def _dilated_attention(proj3, slopes4):
    b, s, _ = proj3.shape
    t = DIL_BLOCK
    tile = t * max(DIL_RATES)
    assert s % tile == 0
    in_specs, args = [], []
    for g, rate in enumerate(DIL_RATES):
        halo = t * rate
        per = tile // halo

        def cur(off, g=g):
            return pl.BlockSpec((None, tile, HEAD_W),
                                lambda bi, h, blk, sl: (bi, blk, off + g * N_HEADS + h))

        def front(off, g=g, halo=halo, per=per):
            return pl.BlockSpec((None, halo, HEAD_W),
                                lambda bi, h, blk, sl: (bi, jnp.maximum(blk * per - 1, 0), off + g * N_HEADS + h))

        in_specs += [cur(OFF_DL_Q), cur(OFF_DL_K), cur(OFF_DL_V), front(OFF_DL_K), front(OFF_DL_V)]
        args += [proj3] * 5
    grid_spec = pltpu.PrefetchScalarGridSpec(
        num_scalar_prefetch=1,
        grid=(b, N_HEADS, s // tile),
        in_specs=in_specs,
        out_specs=pl.BlockSpec((None, tile, HEAD_W), lambda bi, h, blk, sl: (bi, blk, h)),
        scratch_shapes=[pltpu.VMEM((tile, HEAD_W), F32), pltpu.VMEM((tile, HEAD_W), F32)],
    )
    return pl.pallas_call(
        functools.partial(_dil_kernel, tile=tile),
        grid_spec=grid_spec,
        out_shape=jax.ShapeDtypeStruct((b, s, BRANCH_W), BF16),
        compiler_params=_params(("parallel", "parallel", "parallel")),
        name="dilated",
    )(slopes4[:, 1:].T.reshape(-1), *args)


def _merge_kernel(ya_ref, yr_ref, yc_ref, yd_ref, w_ref, g0_ref, g1_ref, g2_ref, g3_ref, o_ref, wb_ref):
    @pl.when(pl.program_id(1) == 0)
    def _():
        _cast_weights(w_ref, wb_ref)

    acc = None
    for n, (y_ref, g_ref) in enumerate(((ya_ref, g0_ref), (yr_ref, g1_ref), (yc_ref, g2_ref), (yd_ref, g3_ref))):
        t = jax.nn.sigmoid(g_ref[...]) * _dot(y_ref[...], wb_ref[n])
        acc = t if acc is None else acc + t
    o_ref[...] = acc.astype(o_ref.dtype)


def _merge(ys, w_branch, layer, proj, d_model, *, tm=512, tn=512):
    m = proj.shape[0]
    tm, tn = _pick(m, tm), _pick(d_model, tn)
    gcols = d_model // tn
    goff = OFF_GATES * 128 // tn
    y_spec = pl.BlockSpec((tm, BRANCH_W), lambda j, i: (i, 0))

    def gate_spec(n):
        return pl.BlockSpec((tm, tn), lambda j, i: (i, goff + n * gcols + j))

    return pl.pallas_call(
        _merge_kernel,
        grid=(d_model // tn, m // tm),
        in_specs=[y_spec, y_spec, y_spec, y_spec,
                  pl.BlockSpec((None, N_BRANCH, BRANCH_W, tn), lambda j, i: (layer, 0, 0, j),
                               pipeline_mode=pl.Buffered(1)),
                  gate_spec(0), gate_spec(1), gate_spec(2), gate_spec(3)],
        out_specs=pl.BlockSpec((tm, tn), lambda j, i: (i, j)),
        out_shape=jax.ShapeDtypeStruct((m, d_model), BF16),
        scratch_shapes=[pltpu.VMEM((N_BRANCH, BRANCH_W, tn), BF16)],
        compiler_params=_params(("parallel", "arbitrary")),
        name="merge",
    )(*ys, w_branch, proj, proj, proj, proj)


def _ffn_up_kernel(x_ref, wa_ref, wg_ref, cw_ref, cb_ref, o_ref, wab_ref, wgb_ref, carry_ref,
                   *, tiles_per_seq):
    i = pl.program_id(1)

    @pl.when(i == 0)
    def _():
        _cast_weights(wa_ref, wab_ref)
        _cast_weights(wg_ref, wgb_ref)

    @pl.when(i % tiles_per_seq == 0)
    def _():
        carry_ref[...] = jnp.zeros(carry_ref.shape, F32)

    x = x_ref[...]
    a = _dot(x, wab_ref[...])
    gl = _dot(x, wgb_ref[...])
    tm = a.shape[0]
    prev = carry_ref[...]
    row = lax.broadcasted_iota(jnp.int32, a.shape, 0)
    a1 = jnp.where(row == 0, prev[7:8, :], pltpu.roll(a, 1, 0))
    a2 = jnp.where(row == 0, prev[6:7, :],
                   jnp.where(row == 1, prev[7:8, :], pltpu.roll(a, 2, 0)))
    cw = cw_ref[...]
    conv = cb_ref[...] + cw[0:1, :] * a2 + cw[1:2, :] * a1 + cw[2:3, :] * a
    gelu = 0.5 * conv * (1.0 + lax.erf(conv * (2.0 ** -0.5)))
    o_ref[...] = (gelu * gl).astype(o_ref.dtype)
    carry_ref[...] = a[tm - 8:tm, :]


def _ffn_up(x, w_up, layer, conv_w, conv_b, seq, *, tm=1024, tn=256):
    m, kdim = x.shape
    d_ff = w_up.shape[-1] // 2
    tm, tn = _pick(seq, tm), _pick(d_ff, tn)
    nj = d_ff // tn
    w_spec = lambda off: pl.BlockSpec((None, kdim, tn), lambda j, i: (layer, 0, off + j),
                                      pipeline_mode=pl.Buffered(1))
    return pl.pallas_call(
        functools.partial(_ffn_up_kernel, tiles_per_seq=seq // tm),
        grid=(nj, m // tm),
        in_specs=[
            pl.BlockSpec((tm, kdim), lambda j, i: (i, 0)),
            w_spec(0), w_spec(nj),
            pl.BlockSpec((CONV_W, tn), lambda j, i: (0, j)),
            pl.BlockSpec((1, tn), lambda j, i: (0, j)),
        ],
        out_specs=pl.BlockSpec((tm, tn), lambda j, i: (i, j)),
        out_shape=jax.ShapeDtypeStruct((m, d_ff), BF16),
        scratch_shapes=[pltpu.VMEM((kdim, tn), BF16), pltpu.VMEM((kdim, tn), BF16),
                        pltpu.VMEM((8, tn), F32)],
        compiler_params=_params(("parallel", "arbitrary")),
        name="ffn_up",
    )(x, w_up, w_up, conv_w, conv_b.reshape(1, d_ff))


def _alibi_slopes():
    n = 4 * N_HEADS
    slopes = 2.0 ** (-8.0 * jnp.arange(1, n + 1, dtype=F32) / n)
    return slopes.reshape(N_HEADS, 4)


def kernel(x, w_in, diff_lambda, diff_norm_g, hgrn_lb_logits, hgrn_norm_g, w_branch, w_o,
           ln1_g, ln1_b, w_up, conv_w, conv_b, w_down, ln2_g, ln2_b):
    b, s, d = x.shape
    depth = w_in.shape[0]
    m = b * s
    alpha = (2 * depth) ** 0.25
    slopes4 = _alibi_slopes()
    p = jax.nn.softmax(hgrn_lb_logits.astype(F32), axis=0)
    lower_bounds = jnp.cumsum(p, axis=0) - p[0]

    h = x.reshape(m, d)
    hb = h.astype(BF16)
    for l in range(depth):
        lam_init = 0.8 - 0.6 * math.exp(-0.3 * l)
        proj = _matmul(hb, w_in, l, out_dtype=F32, name="in_proj")
        proj3 = proj.reshape(b, s, -1)
        ya = _diff_attention(proj3, slopes4[:, 0], diff_lambda[l], diff_norm_g[l], lam_init)
        yr = _retention(proj3)
        yc = _hgrn2(proj3, lower_bounds[l], hgrn_norm_g[l])
        yd = _dilated_attention(proj3, slopes4)
        ys = [y.reshape(m, BRANCH_W) for y in (ya, yr, yc, yd)]
        merged = _merge(ys, w_branch, l, proj, d)
        z = _matmul(merged, w_o, l, out_dtype=F32, res=h, alpha=alpha, name="out_proj")
        h, hb = _layernorm(z, ln1_g[l], ln1_b[l])
        act = _ffn_up(hb, w_up, l, conv_w[l], conv_b[l], s)
        z = _matmul(act, w_down, l, out_dtype=F32, tn=256, res=h, alpha=alpha, name="ffn_down")
        h, hb = _layernorm(z, ln2_g[l], ln2_b[l])
    return h.reshape(b, s, d)
```

```python
import functools
import math

import jax
import jax.numpy as jnp
from jax import lax
from jax.experimental import pallas as pl
from jax.experimental.pallas import tpu as pltpu

F32 = jnp.float32
BF16 = jnp.bfloat16

N_HEADS = 8
HEAD_W = 128
BRANCH_W = N_HEADS * HEAD_W
N_BRANCH = 4
DA_QK = 64
RET_QK = 64
RET_CHUNK = 128
RET_DECAY_BASE = 5.0
HG_CHUNK = 64
DIL_WINDOWS = (128, 512, 2048)
DIL_RATES = (1, 4, 16)
DIL_KEYS = 129
DIL_BLOCK = 128
DIL_UNROLL = 4
CONV_W = 3
EPS = 1e-5
NEG_INF = float("-inf")

_C = BRANCH_W // 128
OFF_DA_Q, OFF_DA_K, OFF_DA_V = 0, _C, 2 * _C
OFF_RT_Q, OFF_RT_K = 3 * _C, 3 * _C + _C // 2
OFF_RT_V, OFF_RT_G = 4 * _C, 5 * _C
OFF_HG_Q, OFF_HG_F, OFF_HG_I, OFF_HG_G = 6 * _C, 7 * _C, 8 * _C, 9 * _C
OFF_DL_Q, OFF_DL_K, OFF_DL_V = 10 * _C, 13 * _C, 16 * _C
OFF_GATES = 19 * _C

VMEM_LIMIT = 56 * 1024 * 1024


def _params(sem):
    return pltpu.CompilerParams(dimension_semantics=sem, vmem_limit_bytes=VMEM_LIMIT)


def _dot(a, b):
    return jnp.dot(a, b, preferred_element_type=F32)


def _dot_nt(a, b):
    return lax.dot_general(a, b, (((1,), (1,)), ((), ())), preferred_element_type=F32)


def _dot_tn(a, b):
    return lax.dot_general(a, b, (((0,), (0,)), ((), ())), preferred_element_type=F32)


def _pick(dim, pref):
    if dim <= pref:
        return dim
    t = pref
    while t >= 128:
        if dim % t == 0 and t % 128 == 0:
            return t
        t -= 128
    return dim


CAST_ROWS = 512


def _cast_weights(w_ref, wb_ref):
    rows = w_ref.shape[-2]
    step = min(rows, CAST_ROWS)
    for r0 in range(0, rows, step):
        r1 = min(rows, r0 + step)
        wb_ref[..., r0:r1, :] = w_ref[..., r0:r1, :].astype(BF16)


def _mm_kernel(*refs, alpha, has_res):
    if has_res:
        x_ref, w_ref, r_ref, o_ref, wb_ref = refs
    else:
        x_ref, w_ref, o_ref, wb_ref = refs

    if w_ref.dtype == BF16:
        wb_ref = w_ref
    else:
        @pl.when(pl.program_id(1) == 0)
        def _():
            _cast_weights(w_ref, wb_ref)

    acc = _dot(x_ref[...], wb_ref[...])
    if has_res:
        acc = alpha * r_ref[...] + acc
    o_ref[...] = acc.astype(o_ref.dtype)


def _matmul(x, w, layer, *, out_dtype, tm=512, tn=1024, res=None, alpha=1.0, name="mm"):
    m, kdim = x.shape
    n = w.shape[-1]
    tm, tn = _pick(m, tm), _pick(n, tn)
    cast_shape = (16, 128) if w.dtype == BF16 else (kdim, tn)
    in_specs = [
        pl.BlockSpec((tm, kdim), lambda j, i: (i, 0)),
        pl.BlockSpec((None, kdim, tn), lambda j, i: (layer, 0, j), pipeline_mode=pl.Buffered(1)),
    ]
    args = [x, w]
    if res is not None:
        in_specs.append(pl.BlockSpec((tm, tn), lambda j, i: (i, j)))
        args.append(res)
    return pl.pallas_call(
        functools.partial(_mm_kernel, alpha=alpha, has_res=res is not None),
        grid=(n // tn, m // tm),
        in_specs=in_specs,
        out_specs=pl.BlockSpec((tm, tn), lambda j, i: (i, j)),
        out_shape=jax.ShapeDtypeStruct((m, n), out_dtype),
        scratch_shapes=[pltpu.VMEM(cast_shape, BF16)],
        compiler_params=_params(("parallel", "arbitrary")),
        name=name,
    )(*args)


def _ln_kernel(x_ref, g_ref, b_ref, o_ref, ob_ref):
    x = x_ref[...]
    mu = jnp.mean(x, axis=-1, keepdims=True)
    xc = x - mu
    var = jnp.mean(xc * xc, axis=-1, keepdims=True)
    y = xc * lax.rsqrt(var + EPS) * g_ref[...] + b_ref[...]
    o_ref[...] = y
    ob_ref[...] = y.astype(BF16)


def _layernorm(x, g, b, *, tm=256):
    m, d = x.shape
    tm = _pick(m, tm)
    return pl.pallas_call(
        _ln_kernel,
        grid=(m // tm,),
        in_specs=[
            pl.BlockSpec((tm, d), lambda i: (i, 0)),
            pl.BlockSpec((1, d), lambda i: (0, 0)),
            pl.BlockSpec((1, d), lambda i: (0, 0)),
        ],
        out_specs=[pl.BlockSpec((tm, d), lambda i: (i, 0)), pl.BlockSpec((tm, d), lambda i: (i, 0))],
        out_shape=[jax.ShapeDtypeStruct((m, d), F32), jax.ShapeDtypeStruct((m, d), BF16)],
        compiler_params=_params(("parallel",)),
        name="layernorm",
    )(x, g.reshape(1, d), b.reshape(1, d))


LOG2E = 1.4426950408889634


def _split3_f32(x):
    hi = x.astype(BF16).astype(F32)
    mid = (x - hi).astype(BF16).astype(F32)
    lo = (x - hi - mid).astype(BF16).astype(F32)
    return hi, mid, lo


def _lane_fill(lane, first, vals, other):
    out = other
    for n, val in enumerate(vals):
        out = jnp.where(lane == first + n, val, out)
    return out


def _diff_attn_kernel(slope_ref, lam_ref, g_ref, q_ref, k_ref, v_ref, o_ref,
                      m1_ref, a1_ref, m2_ref, a2_ref, kaug_sc, mask_sc, sa_sc, sb_sc, *, t, lam_init):
    h = pl.program_id(1)
    qi = pl.program_id(2)
    slope2 = slope_ref[h] * LOG2E
    lane = lax.broadcasted_iota(jnp.int32, (t, HEAD_W), 1)
    aug_first = (DA_QK, 0)
    owns = (lane < DA_QK, lane >= DA_QK)
    accs = ((m1_ref, a1_ref), (m2_ref, a2_ref))

    @pl.when(qi == 0)
    def _():
        j = lax.broadcasted_iota(jnp.int32, (t, HEAD_W), 0).astype(F32)
        parts = _split3_f32(slope2 * j)
        for idx in range(2):
            kaug_sc[idx] = _lane_fill(lane, aug_first[idx], list(parts) + [1.0, 1.0, 1.0], 0.0)
        d = (lax.broadcasted_iota(jnp.int32, (t, t), 0) - lax.broadcasted_iota(jnp.int32, (t, t), 1))
        mask_sc[...] = jnp.where(d >= 0, 0.0, NEG_INF)

    for m_ref, a_ref in accs:
        m_ref[...] = jnp.full(m_ref.shape, NEG_INF, F32)
        a_ref[...] = jnp.zeros(a_ref.shape, F32)

    q = q_ref[...] * (DA_QK ** -0.5 * LOG2E)
    ones_col = jnp.where(lane == 0, 1.0, 0.0).astype(BF16)

    def scores(ki, s_ref):
        r0 = pl.multiple_of(ki * t, t)
        block_c = jnp.full((1, HEAD_W), -slope2 * ((qi - ki) * t).astype(F32), F32)
        c_parts = _split3_f32(block_c)
        k = k_ref[pl.ds(r0, t), :]
        for idx in range(2):
            q_aug = _lane_fill(lane[0:1], aug_first[idx], [1.0, 1.0, 1.0] + list(c_parts), 0.0)
            qq = jnp.where(owns[idx], q, q_aug).astype(BF16)
            kk = jnp.where(owns[idx], k, kaug_sc[idx]).astype(BF16)
            s_ref[idx] = _dot_nt(qq, kk)

    def update(ki, s_ref, masked):
        r0 = pl.multiple_of(ki * t, t)
        v_aug = jnp.concatenate([v_ref[pl.ds(r0, t), :].astype(BF16), ones_col], axis=1)
        for idx, (m_ref, a_ref) in enumerate(accs):
            s = s_ref[idx]
            if masked:
                s = s + mask_sc[...]
            m_prev = m_ref[...]
            m_new = jnp.maximum(m_prev, jnp.max(s, axis=-1, keepdims=True))
            corr = jnp.exp2(m_prev - m_new)
            p = jnp.exp2(s - jnp.tile(m_new, (1, t // HEAD_W)))
            a_ref[...] = jnp.tile(corr, (1, 2)) * a_ref[...] + _dot(p.astype(BF16), v_aug)
            m_ref[...] = m_new

    scores(0, sa_sc)

    def body(pair, carry):
        k0 = 2 * pair
        scores(k0 + 1, sb_sc)
        update(k0, sa_sc, False)
        scores(k0 + 2, sa_sc)
        update(k0 + 1, sb_sc, False)
        return carry

    lax.fori_loop(0, qi // 2, body, 0)

    @pl.when(qi % 2 == 0)
    def _():
        update(qi, sa_sc, True)

    @pl.when(qi % 2 == 1)
    def _():
        scores(qi, sb_sc)
        update(qi - 1, sa_sc, False)
        update(qi, sb_sc, True)

    lv = lam_ref[...]
    lam = (jnp.exp(jnp.sum(lv[0:1] * lv[1:2], axis=-1, keepdims=True))
           - jnp.exp(jnp.sum(lv[2:3] * lv[3:4], axis=-1, keepdims=True)) + lam_init)
    o1 = a1_ref[:, :HEAD_W] / a1_ref[:, HEAD_W:HEAD_W + 1]
    o2 = a2_ref[:, :HEAD_W] / a2_ref[:, HEAD_W:HEAD_W + 1]
    o = o1 - lam * o2
    ms = jnp.mean(o * o, axis=-1, keepdims=True)
    o = o * lax.rsqrt(ms + EPS) * g_ref[...] * (1.0 - lam_init)
    o_ref[...] = o.astype(o_ref.dtype)


def _diff_attention(proj3, slopes, lam_vec, g, lam_init, *, t=512):
    b, s, _ = proj3.shape
    t = _pick(s, t)
    kern = functools.partial(_diff_attn_kernel, t=t, lam_init=lam_init)
    grid_spec = pltpu.PrefetchScalarGridSpec(
        num_scalar_prefetch=1,
        grid=(b, N_HEADS, s // t),
        in_specs=[
            pl.BlockSpec(lam_vec.shape, lambda bi, h, qi, sl: (0, 0)),
            pl.BlockSpec((1, HEAD_W), lambda bi, h, qi, sl: (0, 0)),
            pl.BlockSpec((None, t, HEAD_W), lambda bi, h, qi, sl: (bi, qi, OFF_DA_Q + h)),
            pl.BlockSpec((None, s, HEAD_W), lambda bi, h, qi, sl: (bi, 0, OFF_DA_K + h)),
            pl.BlockSpec((None, s, HEAD_W), lambda bi, h, qi, sl: (bi, 0, OFF_DA_V + h)),
        ],
        out_specs=pl.BlockSpec((None, t, HEAD_W), lambda bi, h, qi, sl: (bi, qi, h)),
        scratch_shapes=[pltpu.VMEM((t, HEAD_W), F32), pltpu.VMEM((t, 2 * HEAD_W), F32)] * 2
        + [pltpu.VMEM((2, t, HEAD_W), F32), pltpu.VMEM((t, t), F32)]
        + [pltpu.VMEM((2, t, t), F32)] * 2,
    )
    return pl.pallas_call(
        kern,
        grid_spec=grid_spec,
        out_shape=jax.ShapeDtypeStruct((b, s, BRANCH_W), BF16),
        compiler_params=_params(("parallel", "parallel", "arbitrary")),
        name="diff_attention",
    )(slopes, lam_vec, g.reshape(1, HEAD_W), proj3, proj3, proj3)


def _retention_kernel(inner_ref, qd_ref, kd_ref, cd_ref, q_ref, k_ref, v_ref, g_ref, o_ref,
                      state_ref, *, rows):
    ci = pl.program_id(2)

    @pl.when(ci == 0)
    def _():
        state_ref[...] = jnp.zeros(state_ref.shape, F32)

    c = RET_CHUNK
    for p in range(2):
        inner = inner_ref[p]
        qdec = qd_ref[p]
        kdec = kd_ref[p]
        cdec = cd_ref[p]
        state = state_ref[p]
        for n in range(rows // c):
            r0 = n * c
            qc = (q_ref[r0:r0 + c, p * RET_QK:(p + 1) * RET_QK] * (RET_QK ** -0.5)).astype(BF16)
            kf = k_ref[r0:r0 + c, p * RET_QK:(p + 1) * RET_QK]
            vc = v_ref[r0:r0 + c, p * HEAD_W:(p + 1) * HEAD_W].astype(BF16)
            att = _dot_nt(qc, kf.astype(BF16)) * inner
            o = _dot(att.astype(BF16), vc) + _dot(qc, state.astype(BF16)) * qdec
            state = cdec * state + _dot_tn((kf * kdec).astype(BF16), vc)
            mu = jnp.mean(o, axis=-1, keepdims=True)
            oc = o - mu
            var = jnp.mean(oc * oc, axis=-1, keepdims=True)
            gate = g_ref[r0:r0 + c, p * HEAD_W:(p + 1) * HEAD_W]
            y = oc * lax.rsqrt(var + EPS) * (gate * jax.nn.sigmoid(gate))
            o_ref[r0:r0 + c, p * HEAD_W:(p + 1) * HEAD_W] = y.astype(o_ref.dtype)
        state_ref[p] = state


def _retention_tables():
    h = N_HEADS
    c = RET_CHUNK
    log_g = jnp.log1p(-(2.0 ** (-RET_DECAY_BASE - jnp.arange(h, dtype=F32))))
    idx = jnp.arange(c, dtype=F32)
    rel = idx[:, None] - idx[None, :]
    inner = jnp.where(rel >= 0, jnp.exp(log_g[:, None, None] * jnp.maximum(rel, 0.0)), 0.0)
    qd = jnp.broadcast_to(jnp.exp(log_g[:, None] * (idx + 1.0))[:, :, None], (h, c, HEAD_W))
    kd = jnp.broadcast_to(jnp.exp(log_g[:, None] * (c - 1.0 - idx))[:, :, None], (h, c, RET_QK))
    cd = jnp.broadcast_to(jnp.exp(log_g * c)[:, None, None], (h, RET_QK, HEAD_W))
    return inner, qd, kd, cd


def _retention(proj3, *, rows=512):
    b, s, _ = proj3.shape
    rows = _pick(s, rows)
    inner, qd, kd, cd = _retention_tables()
    c = RET_CHUNK
    return pl.pallas_call(
        functools.partial(_retention_kernel, rows=rows),
        grid=(b, N_HEADS // 2, s // rows),
        in_specs=[
            pl.BlockSpec((2, c, c), lambda bi, hp, ci: (hp, 0, 0)),
            pl.BlockSpec((2, c, HEAD_W), lambda bi, hp, ci: (hp, 0, 0)),
            pl.BlockSpec((2, c, RET_QK), lambda bi, hp, ci: (hp, 0, 0)),
            pl.BlockSpec((2, RET_QK, HEAD_W), lambda bi, hp, ci: (hp, 0, 0)),
            pl.BlockSpec((None, rows, 128), lambda bi, hp, ci: (bi, ci, OFF_RT_Q + hp)),
            pl.BlockSpec((None, rows, 128), lambda bi, hp, ci: (bi, ci, OFF_RT_K + hp)),
            pl.BlockSpec((None, rows, 2 * HEAD_W), lambda bi, hp, ci: (bi, ci, OFF_RT_V // 2 + hp)),
            pl.BlockSpec((None, rows, 2 * HEAD_W), lambda bi, hp, ci: (bi, ci, OFF_RT_G // 2 + hp)),
        ],
        out_specs=pl.BlockSpec((None, rows, 2 * HEAD_W), lambda bi, hp, ci: (bi, ci, hp)),
        out_shape=jax.ShapeDtypeStruct((b, s, BRANCH_W), BF16),
        scratch_shapes=[pltpu.VMEM((2, RET_QK, HEAD_W), F32)],
        compiler_params=_params(("parallel", "parallel", "arbitrary")),
        name="retention",
    )(inner, qd, kd, cd, proj3, proj3, proj3, proj3)


def _split3(x):
    hi = x.astype(BF16)
    r1 = x - hi.astype(F32)
    mid = r1.astype(BF16)
    lo = (r1 - mid.astype(F32)).astype(BF16)
    return hi, mid, lo


HG_SUB = 16


def _hgrn2_kernel(lb_ref, gn_ref, q_ref, f_ref, i_ref, gt_ref, o_ref,
                  state_ref, b_sc, k_sc, q_sc, *, rows):
    ci = pl.program_id(2)
    c, sc = HG_CHUNK, HG_SUB

    @pl.when(ci == 0)
    def _():
        state_ref[...] = jnp.zeros(state_ref.shape, F32)

    lb = lb_ref[...]
    gn = gn_ref[...]
    fr = f_ref[...]
    log_f = jnp.log(lb + (1.0 - lb) * jax.nn.sigmoid(fr))
    k_sc[...] = (1.0 - lb) * jax.nn.sigmoid(-fr)
    qraw = q_ref[...]
    q_sc[...] = qraw * jax.nn.sigmoid(qraw)
    ri = lax.broadcasted_iota(jnp.int32, (rows, rows), 0)
    cj = lax.broadcasted_iota(jnp.int32, (rows, rows), 1)
    tri = jnp.where(jnp.logical_and(ri >= cj, (ri // c) == (cj // c)), 1.0, 0.0).astype(BF16)
    hi, mid, lo = _split3(log_f)
    b_sc[...] = _dot(tri, hi) + _dot(tri, mid) + _dot(tri, lo)
    row_sub = lax.broadcasted_iota(jnp.int32, (sc, HEAD_W), 0)

    for n in range(rows // c):
        r0 = n * c
        parts = []
        for sub in range(c // sc):
            s0 = r0 + sub * sc
            b_i = b_sc[s0:s0 + sc, :]
            q_i = q_sc[s0:s0 + sc, :]
            acc = jnp.zeros((sc, HEAD_W), F32)
            for jj in range(sc):
                j = s0 + jj
                d = b_i - b_sc[j:j + 1, :]
                if jj > 0:
                    d = jnp.where(row_sub >= jj, d, NEG_INF)
                a = jnp.sum(q_i * k_sc[j:j + 1, :] * jnp.exp(d), axis=-1, keepdims=True)
                acc = acc + a * i_ref[j:j + 1, :]
            if sub > 0:
                anchor = b_sc[s0:s0 + 1, :]
                qs = (q_i * jnp.exp(b_i - anchor)).astype(BF16)
                ks = (k_sc[r0:s0, :] * jnp.exp(anchor - b_sc[r0:s0, :])).astype(BF16)
                att = _dot_nt(qs, ks)
                acc = acc + _dot(att.astype(BF16), i_ref[r0:s0, :].astype(BF16))
            parts.append(acc)
        o = jnp.concatenate(parts, axis=0)
        bc = b_sc[r0:r0 + c, :]
        state_t = state_ref[...]
        o = o + _dot_nt((q_sc[r0:r0 + c, :] * jnp.exp(bc)).astype(BF16), state_t.astype(BF16))
        blast = b_sc[r0 + c - 1:r0 + c, :]
        kd = (k_sc[r0:r0 + c, :] * jnp.exp(blast - bc)).astype(BF16)
        state_ref[...] = jnp.exp(blast) * state_t + _dot_tn(i_ref[r0:r0 + c, :].astype(BF16), kd)
        ms = jnp.mean(o * o, axis=-1, keepdims=True)
        gate = gt_ref[r0:r0 + c, :]
        y = o * lax.rsqrt(ms + EPS) * gn * (gate * jax.nn.sigmoid(gate))
        o_ref[r0:r0 + c, :] = y.astype(o_ref.dtype)


def _hgrn2(proj3, lb, gn, *, rows=256):
    b, s, _ = proj3.shape
    rows = _pick(s, rows)
    blk = lambda off: pl.BlockSpec((None, rows, HEAD_W), lambda bi, h, ci: (bi, ci, off + h))
    return pl.pallas_call(
        functools.partial(_hgrn2_kernel, rows=rows),
        grid=(b, N_HEADS, s // rows),
        in_specs=[
            pl.BlockSpec((None, 1, HEAD_W), lambda bi, h, ci: (h, 0, 0)),
            pl.BlockSpec((1, HEAD_W), lambda bi, h, ci: (0, 0)),
            blk(OFF_HG_Q), blk(OFF_HG_F), blk(OFF_HG_I), blk(OFF_HG_G),
        ],
        out_specs=pl.BlockSpec((None, rows, HEAD_W), lambda bi, h, ci: (bi, ci, h)),
        out_shape=jax.ShapeDtypeStruct((b, s, BRANCH_W), BF16),
        scratch_shapes=[pltpu.VMEM((HEAD_W, HEAD_W), F32)] + [pltpu.VMEM((rows, HEAD_W), F32)] * 3,
        compiler_params=_params(("parallel", "parallel", "arbitrary")),
        name="hgrn2",
    )(lb.reshape(N_HEADS, 1, HEAD_W), gn.reshape(1, HEAD_W), proj3, proj3, proj3, proj3)


def _dil_kernel(slope_ref, *refs, tile):
    n_g = len(DIL_RATES)
    groups = [refs[5 * g:5 * g + 5] for g in range(n_g)]
    o_ref, o_sc, l_sc = refs[5 * n_g:]
    h = pl.program_id(1)
    blk = pl.program_id(2)
    t = DIL_BLOCK
    d_cur = (lax.broadcasted_iota(jnp.int32, (t, t), 0)
             - lax.broadcasted_iota(jnp.int32, (t, t), 1)).astype(F32)
    d_prev = d_cur + float(t)

    for g, (q_ref, k_ref, v_ref, kh_ref, vh_ref) in enumerate(groups):
        rate = DIL_RATES[g]
        halo = t * rate
        slope = slope_ref[g * N_HEADS + h] * float(rate)
        bias_cur = jnp.where(d_cur >= 0, -slope * d_cur, NEG_INF)
        bias_prev = jnp.where(d_prev <= float(t), -slope * d_prev, NEG_INF)
        bias_both = jnp.concatenate([bias_prev, bias_cur], axis=1)
        bias_first = jnp.concatenate([jnp.full((t, t), NEG_INF, F32), bias_cur], axis=1)

        def rows(start, rate=rate):
            return pl.ds(start, t) if rate == 1 else pl.ds(start, t, stride=rate)

        def units(ug, carry, g=g, rate=rate, halo=halo, q_ref=q_ref, k_ref=k_ref, v_ref=v_ref,
                  kh_ref=kh_ref, vh_ref=vh_ref, bias_both=bias_both, bias_first=bias_first, rows=rows):
            starts, qs, ks, vs, biases = [], [], [], [], []
            for uu in range(DIL_UNROLL):
                u = ug * DIL_UNROLL + uu
                r = u % rate
                mb = u // rate
                start = mb * halo + r
                starts.append(start)
                qs.append((q_ref[rows(start), :] * (HEAD_W ** -0.5)).astype(BF16))
                kp = kh_ref[rows(r), :]
                vp = vh_ref[rows(r), :]
                prev_ok = blk > 0
                if tile > halo:
                    pstart = jnp.maximum(start - halo, r)
                    in_tile = mb > 0
                    kp = jnp.where(in_tile, k_ref[rows(pstart), :], kp)
                    vp = jnp.where(in_tile, v_ref[rows(pstart), :], vp)
                    prev_ok = jnp.logical_or(prev_ok, in_tile)
                ks.append(jnp.concatenate([kp, k_ref[rows(start), :]], axis=0).astype(BF16))
                vs.append(jnp.concatenate([vp, v_ref[rows(start), :]], axis=0).astype(BF16))
                biases.append(jnp.where(prev_ok, bias_both, bias_first))
            ss = [_dot_nt(q, k) + bias for q, k, bias in zip(qs, ks, biases)]
            ms = [jnp.max(s, axis=-1, keepdims=True) for s in ss]
            ps = [jnp.exp(s - m) for s, m in zip(ss, ms)]
            ls = [jnp.sum(p, axis=-1, keepdims=True) for p in ps]
            accs = [_dot(p.astype(BF16), v) for p, v in zip(ps, vs)]
            for start, m, l, acc in zip(starts, ms, ls, accs):
                o = acc / l
                lse = jnp.broadcast_to(m + jnp.log(l), (t, HEAD_W))
                if g > 0:
                    lse_p = l_sc[rows(start), :]
                    lse_n = jnp.maximum(lse, lse_p)
                    lse_n = lse_n + jnp.log(jnp.exp(lse - lse_n) + jnp.exp(lse_p - lse_n))
                    o = o_sc[rows(start), :] * jnp.exp(lse_p - lse_n) + o * jnp.exp(lse - lse_n)
                    lse = lse_n
                o_sc[rows(start), :] = o
                if g < n_g - 1:
                    l_sc[rows(start), :] = lse
            return carry

        lax.fori_loop(0, tile // t // DIL_UNROLL, units, 0)

    o_ref[...] = o_sc[...].astype(o_ref.dtype)


def _dilated_attention(proj3, slopes4):
    b, s, _ = proj3.shape
    t = DIL_BLOCK
    tile = t * max(DIL_RATES)
    assert s % tile == 0
    in_specs, args = [], []
    for g, rate in enumerate(DIL_RATES):
        halo = t * rate
        per = tile // halo

        def cur(off, g=g):
            return pl.BlockSpec((None, tile, HEAD_W),
                                lambda bi, h, blk, sl: (bi, blk, off + g * N_HEADS + h))

        def front(off, g=g, halo=halo, per=per):
            return pl.BlockSpec((None, halo, HEAD_W),
                                lambda bi, h, blk, sl: (bi, jnp.maximum(blk * per - 1, 0), off + g * N_HEADS + h))

        in_specs += [cur(OFF_DL_Q), cur(OFF_DL_K), cur(OFF_DL_V), front(OFF_DL_K), front(OFF_DL_V)]
        args += [proj3] * 5
    grid_spec = pltpu.PrefetchScalarGridSpec(
        num_scalar_prefetch=1,
        grid=(b, N_HEADS, s // tile),
        in_specs=in_specs,
        out_specs=pl.BlockSpec((None, tile, HEAD_W), lambda bi, h, blk, sl: (bi, blk, h)),
        scratch_shapes=[pltpu.VMEM((tile, HEAD_W), F32), pltpu.VMEM((tile, HEAD_W), F32)],
    )
    return pl.pallas_call(
        functools.partial(_dil_kernel, tile=tile),
        grid_spec=grid_spec,
        out_shape=jax.ShapeDtypeStruct((b, s, BRANCH_W), BF16),
        compiler_params=_params(("parallel", "parallel", "parallel")),
        name="dilated",
    )(slopes4[:, 1:].T.reshape(-1), *args)


def _merge_kernel(ya_ref, yr_ref, yc_ref, yd_ref, w_ref, g0_ref, g1_ref, g2_ref, g3_ref, o_ref, wb_ref):
    @pl.when(pl.program_id(1) == 0)
    def _():
        _cast_weights(w_ref, wb_ref)

    acc = None
    for n, (y_ref, g_ref) in enumerate(((ya_ref, g0_ref), (yr_ref, g1_ref), (yc_ref, g2_ref), (yd_ref, g3_ref))):
        t = jax.nn.sigmoid(g_ref[...]) * _dot(y_ref[...], wb_ref[n])
        acc = t if acc is None else acc + t
    o_ref[...] = acc.astype(o_ref.dtype)


def _merge(ys, w_branch, layer, proj, d_model, *, tm=512, tn=512):
    m = proj.shape[0]
    tm, tn = _pick(m, tm), _pick(d_model, tn)
    gcols = d_model // tn
    goff = OFF_GATES * 128 // tn
    y_spec = pl.BlockSpec((tm, BRANCH_W), lambda j, i: (i, 0))

    def gate_spec(n):
        return pl.BlockSpec((tm, tn), lambda j, i: (i, goff + n * gcols + j))

    return pl.pallas_call(
        _merge_kernel,
        grid=(d_model // tn, m // tm),
        in_specs=[y_spec, y_spec, y_spec, y_spec,
                  pl.BlockSpec((None, N_BRANCH, BRANCH_W, tn), lambda j, i: (layer, 0, 0, j),
                               pipeline_mode=pl.Buffered(1)),
                  gate_spec(0), gate_spec(1), gate_spec(2), gate_spec(3)],
        out_specs=pl.BlockSpec((tm, tn), lambda j, i: (i, j)),
        out_shape=jax.ShapeDtypeStruct((m, d_model), BF16),
        scratch_shapes=[pltpu.VMEM((N_BRANCH, BRANCH_W, tn), BF16)],
        compiler_params=_params(("parallel", "arbitrary")),
        name="merge",
    )(*ys, w_branch, proj, proj, proj, proj)


def _ffn_up_kernel(x_ref, wa_ref, wg_ref, cw_ref, cb_ref, o_ref, wab_ref, wgb_ref, carry_ref,
                   *, tiles_per_seq):
    i = pl.program_id(1)

    @pl.when(i == 0)
    def _():
        _cast_weights(wa_ref, wab_ref)
        _cast_weights(wg_ref, wgb_ref)

    @pl.when(i % tiles_per_seq == 0)
    def _():
        carry_ref[...] = jnp.zeros(carry_ref.shape, F32)

    x = x_ref[...]
    a = _dot(x, wab_ref[...])
    gl = _dot(x, wgb_ref[...])
    tm = a.shape[0]
    prev = carry_ref[...]
    row = lax.broadcasted_iota(jnp.int32, a.shape, 0)
    a1 = jnp.where(row == 0, prev[7:8, :], pltpu.roll(a, 1, 0))
    a2 = jnp.where(row == 0, prev[6:7, :],
                   jnp.where(row == 1, prev[7:8, :], pltpu.roll(a, 2, 0)))
    cw = cw_ref[...]
    conv = cb_ref[...] + cw[0:1, :] * a2 + cw[1:2, :] * a1 + cw[2:3, :] * a
    gelu = 0.5 * conv * (1.0 + lax.erf(conv * (2.0 ** -0.5)))
    o_ref[...] = (gelu * gl).astype(o_ref.dtype)
    carry_ref[...] = a[tm - 8:tm, :]


def _ffn_up(x, w_up, layer, conv_w, conv_b, seq, *, tm=1024, tn=256):
    m, kdim = x.shape
    d_ff = w_up.shape[-1] // 2
    tm, tn = _pick(seq, tm), _pick(d_ff, tn)
    nj = d_ff // tn
    w_spec = lambda off: pl.BlockSpec((None, kdim, tn), lambda j, i: (layer, 0, off + j),
                                      pipeline_mode=pl.Buffered(1))
    return pl.pallas_call(
        functools.partial(_ffn_up_kernel, tiles_per_seq=seq // tm),
        grid=(nj, m // tm),
        in_specs=[
            pl.BlockSpec((tm, kdim), lambda j, i: (i, 0)),
            w_spec(0), w_spec(nj),
            pl.BlockSpec((CONV_W, tn), lambda j, i: (0, j)),
            pl.BlockSpec((1, tn), lambda j, i: (0, j)),
        ],
        out_specs=pl.BlockSpec((tm, tn), lambda j, i: (i, j)),
        out_shape=jax.ShapeDtypeStruct((m, d_ff), BF16),
        scratch_shapes=[pltpu.VMEM((kdim, tn), BF16), pltpu.VMEM((kdim, tn), BF16),
                        pltpu.VMEM((8, tn), F32)],
        compiler_params=_params(("parallel", "arbitrary")),
        name="ffn_up",
    )(x, w_up, w_up, conv_w, conv_b.reshape(1, d_ff))


def _alibi_slopes():
    n = 4 * N_HEADS
    slopes = 2.0 ** (-8.0 * jnp.arange(1, n + 1, dtype=F32) / n)
    return slopes.reshape(N_HEADS, 4)


def kernel(x, w_in, diff_lambda, diff_norm_g, hgrn_lb_logits, hgrn_norm_g, w_branch, w_o,
           ln1_g, ln1_b, w_up, conv_w, conv_b, w_down, ln2_g, ln2_b):
    b, s, d = x.shape
    depth = w_in.shape[0]
    m = b * s
    alpha = (2 * depth) ** 0.25
    slopes4 = _alibi_slopes()
    p = jax.nn.softmax(hgrn_lb_logits.astype(F32), axis=0)
    lower_bounds = jnp.cumsum(p, axis=0) - p[0]

    w_down_b = w_down.astype(BF16)
    h = x.reshape(m, d)
    hb = h.astype(BF16)
    for l in range(depth):
        lam_init = 0.8 - 0.6 * math.exp(-0.3 * l)
        proj = _matmul(hb, w_in, l, out_dtype=F32, name="in_proj")
        proj3 = proj.reshape(b, s, -1)
        ya = _diff_attention(proj3, slopes4[:, 0], diff_lambda[l], diff_norm_g[l], lam_init)
        yr = _retention(proj3)
        yc = _hgrn2(proj3, lower_bounds[l], hgrn_norm_g[l])
        yd = _dilated_attention(proj3, slopes4)
        ys = [y.reshape(m, BRANCH_W) for y in (ya, yr, yc, yd)]
        merged = _merge(ys, w_branch, l, proj, d)
        z = _matmul(merged, w_o, l, out_dtype=F32, res=h, alpha=alpha, name="out_proj")
        h, hb = _layernorm(z, ln1_g[l], ln1_b[l])
        act = _ffn_up(hb, w_up, l, conv_w[l], conv_b[l], s)
        z = _matmul(act, w_down_b, l, out_dtype=F32, tn=512, res=h, alpha=alpha, name="ffn_down")
        h, hb = _layernorm(z, ln2_g[l], ln2_b[l])
    return h.reshape(b, s, d)
```

```python
import functools
import math

import jax
import jax.numpy as jnp
from jax import lax
from jax.experimental import pallas as pl
from jax.experimental.pallas import tpu as pltpu

F32 = jnp.float32
BF16 = jnp.bfloat16

N_HEADS = 8
HEAD_W = 128
BRANCH_W = N_HEADS * HEAD_W
N_BRANCH = 4
DA_QK = 64
RET_QK = 64
RET_CHUNK = 128
RET_DECAY_BASE = 5.0
HG_CHUNK = 64
DIL_WINDOWS = (128, 512, 2048)
DIL_RATES = (1, 4, 16)
DIL_KEYS = 129
DIL_BLOCK = 128
DIL_UNROLL = 4
CONV_W = 3
EPS = 1e-5
NEG_INF = float("-inf")

_C = BRANCH_W // 128
OFF_DA_Q, OFF_DA_K, OFF_DA_V = 0, _C, 2 * _C
OFF_RT_Q, OFF_RT_K = 3 * _C, 3 * _C + _C // 2
OFF_RT_V, OFF_RT_G = 4 * _C, 5 * _C
OFF_HG_Q, OFF_HG_F, OFF_HG_I, OFF_HG_G = 6 * _C, 7 * _C, 8 * _C, 9 * _C
OFF_DL_Q, OFF_DL_K, OFF_DL_V = 10 * _C, 13 * _C, 16 * _C
OFF_GATES = 19 * _C

VMEM_LIMIT = 56 * 1024 * 1024


def _params(sem):
    return pltpu.CompilerParams(dimension_semantics=sem, vmem_limit_bytes=VMEM_LIMIT)


def _dot(a, b):
    return jnp.dot(a, b, preferred_element_type=F32)


def _dot_nt(a, b):
    return lax.dot_general(a, b, (((1,), (1,)), ((), ())), preferred_element_type=F32)


def _dot_tn(a, b):
    return lax.dot_general(a, b, (((0,), (0,)), ((), ())), preferred_element_type=F32)


def _pick(dim, pref):
    if dim <= pref:
        return dim
    t = pref
    while t >= 128:
        if dim % t == 0 and t % 128 == 0:
            return t
        t -= 128
    return dim


CAST_ROWS = 512


def _cast_weights(w_ref, wb_ref):
    rows = w_ref.shape[-2]
    step = min(rows, CAST_ROWS)
    for r0 in range(0, rows, step):
        r1 = min(rows, r0 + step)
        wb_ref[..., r0:r1, :] = w_ref[..., r0:r1, :].astype(BF16)


def _mm_kernel(*refs, alpha, has_res):
    if has_res:
        x_ref, w_ref, r_ref, o_ref, wb_ref = refs
    else:
        x_ref, w_ref, o_ref, wb_ref = refs

    if w_ref.dtype == BF16:
        wb_ref = w_ref
    else:
        @pl.when(pl.program_id(1) == 0)
        def _():
            _cast_weights(w_ref, wb_ref)

    acc = _dot(x_ref[...], wb_ref[...])
    if has_res:
        acc = alpha * r_ref[...] + acc
    o_ref[...] = acc.astype(o_ref.dtype)


def _matmul(x, w, layer, *, out_dtype, tm=512, tn=1024, res=None, alpha=1.0, name="mm"):
    m, kdim = x.shape
    n = w.shape[-1]
    tm, tn = _pick(m, tm), _pick(n, tn)
    cast_shape = (16, 128) if w.dtype == BF16 else (kdim, tn)
    in_specs = [
        pl.BlockSpec((tm, kdim), lambda j, i: (i, 0)),
        pl.BlockSpec((None, kdim, tn), lambda j, i: (layer, 0, j), pipeline_mode=pl.Buffered(1)),
    ]
    args = [x, w]
    if res is not None:
        in_specs.append(pl.BlockSpec((tm, tn), lambda j, i: (i, j)))
        args.append(res)
    return pl.pallas_call(
        functools.partial(_mm_kernel, alpha=alpha, has_res=res is not None),
        grid=(n // tn, m // tm),
        in_specs=in_specs,
        out_specs=pl.BlockSpec((tm, tn), lambda j, i: (i, j)),
        out_shape=jax.ShapeDtypeStruct((m, n), out_dtype),
        scratch_shapes=[pltpu.VMEM(cast_shape, BF16)],
        compiler_params=_params(("parallel", "arbitrary")),
        name=name,
    )(*args)


def _ln_kernel(x_ref, g_ref, b_ref, o_ref, ob_ref):
    x = x_ref[...]
    mu = jnp.mean(x, axis=-1, keepdims=True)
    xc = x - mu
    var = jnp.mean(xc * xc, axis=-1, keepdims=True)
    y = xc * lax.rsqrt(var + EPS) * g_ref[...] + b_ref[...]
    o_ref[...] = y
    ob_ref[...] = y.astype(BF16)


def _layernorm(x, g, b, *, tm=256):
    m, d = x.shape
    tm = _pick(m, tm)
    return pl.pallas_call(
        _ln_kernel,
        grid=(m // tm,),
        in_specs=[
            pl.BlockSpec((tm, d), lambda i: (i, 0)),
            pl.BlockSpec((1, d), lambda i: (0, 0)),
            pl.BlockSpec((1, d), lambda i: (0, 0)),
        ],
        out_specs=[pl.BlockSpec((tm, d), lambda i: (i, 0)), pl.BlockSpec((tm, d), lambda i: (i, 0))],
        out_shape=[jax.ShapeDtypeStruct((m, d), F32), jax.ShapeDtypeStruct((m, d), BF16)],
        compiler_params=_params(("parallel",)),
        name="layernorm",
    )(x, g.reshape(1, d), b.reshape(1, d))


LOG2E = 1.4426950408889634


def _split3_f32(x):
    hi = x.astype(BF16).astype(F32)
    mid = (x - hi).astype(BF16).astype(F32)
    lo = (x - hi - mid).astype(BF16).astype(F32)
    return hi, mid, lo


def _lane_fill(lane, first, vals, other):
    out = other
    for n, val in enumerate(vals):
        out = jnp.where(lane == first + n, val, out)
    return out


def _diff_attn_kernel(slope_ref, lam_ref, g_ref, q_ref, k_ref, v_ref, o_ref,
                      m1_ref, a1_ref, m2_ref, a2_ref, kaug_sc, mask_sc, sa_sc, sb_sc, *, t, lam_init):
    h = pl.program_id(1)
    qi = pl.program_id(2)
    slope2 = slope_ref[h] * LOG2E
    lane = lax.broadcasted_iota(jnp.int32, (t, HEAD_W), 1)
    aug_first = (DA_QK, 0)
    owns = (lane < DA_QK, lane >= DA_QK)
    accs = ((m1_ref, a1_ref), (m2_ref, a2_ref))

    @pl.when(qi == 0)
    def _():
        j = lax.broadcasted_iota(jnp.int32, (t, HEAD_W), 0).astype(F32)
        parts = _split3_f32(slope2 * j)
        for idx in range(2):
            kaug_sc[idx] = _lane_fill(lane, aug_first[idx], list(parts) + [1.0, 1.0, 1.0], 0.0)
        d = (lax.broadcasted_iota(jnp.int32, (t, t), 0) - lax.broadcasted_iota(jnp.int32, (t, t), 1))
        mask_sc[...] = jnp.where(d >= 0, 0.0, NEG_INF)

    for m_ref, a_ref in accs:
        m_ref[...] = jnp.full(m_ref.shape, NEG_INF, F32)
        a_ref[...] = jnp.zeros(a_ref.shape, F32)

    q = q_ref[...] * (DA_QK ** -0.5 * LOG2E)
    ones_col = jnp.where(lane == 0, 1.0, 0.0).astype(BF16)

    def scores(ki, s_ref):
        r0 = pl.multiple_of(ki * t, t)
        block_c = jnp.full((1, HEAD_W), -slope2 * ((qi - ki) * t).astype(F32), F32)
        c_parts = _split3_f32(block_c)
        k = k_ref[pl.ds(r0, t), :]
        for idx in range(2):
            q_aug = _lane_fill(lane[0:1], aug_first[idx], [1.0, 1.0, 1.0] + list(c_parts), 0.0)
            qq = jnp.where(owns[idx], q, q_aug).astype(BF16)
            kk = jnp.where(owns[idx], k, kaug_sc[idx]).astype(BF16)
            s_ref[idx] = _dot_nt(qq, kk)

    def update(ki, s_ref, masked):
        r0 = pl.multiple_of(ki * t, t)
        v_aug = jnp.concatenate([v_ref[pl.ds(r0, t), :].astype(BF16), ones_col], axis=1)
        for idx, (m_ref, a_ref) in enumerate(accs):
            s = s_ref[idx]
            if masked:
                s = s + mask_sc[...]
            m_prev = m_ref[...]
            m_new = jnp.maximum(m_prev, jnp.max(s, axis=-1, keepdims=True))
            corr = jnp.exp2(m_prev - m_new)
            p = jnp.exp2(s - jnp.tile(m_new, (1, t // HEAD_W)))
            a_ref[...] = jnp.tile(corr, (1, 2)) * a_ref[...] + _dot(p.astype(BF16), v_aug)
            m_ref[...] = m_new

    scores(0, sa_sc)

    def body(pair, carry):
        k0 = 2 * pair
        scores(k0 + 1, sb_sc)
        update(k0, sa_sc, False)
        scores(k0 + 2, sa_sc)
        update(k0 + 1, sb_sc, False)
        return carry

    lax.fori_loop(0, qi // 2, body, 0)

    @pl.when(qi % 2 == 0)
    def _():
        update(qi, sa_sc, True)

    @pl.when(qi % 2 == 1)
    def _():
        scores(qi, sb_sc)
        update(qi - 1, sa_sc, False)
        update(qi, sb_sc, True)

    lv = lam_ref[...]
    lam = (jnp.exp(jnp.sum(lv[0:1] * lv[1:2], axis=-1, keepdims=True))
           - jnp.exp(jnp.sum(lv[2:3] * lv[3:4], axis=-1, keepdims=True)) + lam_init)
    o1 = a1_ref[:, :HEAD_W] / a1_ref[:, HEAD_W:HEAD_W + 1]
    o2 = a2_ref[:, :HEAD_W] / a2_ref[:, HEAD_W:HEAD_W + 1]
    o = o1 - lam * o2
    ms = jnp.mean(o * o, axis=-1, keepdims=True)
    o = o * lax.rsqrt(ms + EPS) * g_ref[...] * (1.0 - lam_init)
    o_ref[...] = o.astype(o_ref.dtype)


def _diff_attention(proj3, slopes, lam_vec, g, lam_init, *, t=512):
    b, s, _ = proj3.shape
    t = _pick(s, t)
    kern = functools.partial(_diff_attn_kernel, t=t, lam_init=lam_init)
    grid_spec = pltpu.PrefetchScalarGridSpec(
        num_scalar_prefetch=1,
        grid=(b, N_HEADS, s // t),
        in_specs=[
            pl.BlockSpec(lam_vec.shape, lambda bi, h, qi, sl: (0, 0)),
            pl.BlockSpec((1, HEAD_W), lambda bi, h, qi, sl: (0, 0)),
            pl.BlockSpec((None, t, HEAD_W), lambda bi, h, qi, sl: (bi, qi, OFF_DA_Q + h)),
            pl.BlockSpec((None, s, HEAD_W), lambda bi, h, qi, sl: (bi, 0, OFF_DA_K + h)),
            pl.BlockSpec((None, s, HEAD_W), lambda bi, h, qi, sl: (bi, 0, OFF_DA_V + h)),
        ],
        out_specs=pl.BlockSpec((None, t, HEAD_W), lambda bi, h, qi, sl: (bi, qi, h)),
        scratch_shapes=[pltpu.VMEM((t, HEAD_W), F32), pltpu.VMEM((t, 2 * HEAD_W), F32)] * 2
        + [pltpu.VMEM((2, t, HEAD_W), F32), pltpu.VMEM((t, t), F32)]
        + [pltpu.VMEM((2, t, t), F32)] * 2,
    )
    return pl.pallas_call(
        kern,
        grid_spec=grid_spec,
        out_shape=jax.ShapeDtypeStruct((b, s, BRANCH_W), BF16),
        compiler_params=_params(("parallel", "parallel", "arbitrary")),
        name="diff_attention",
    )(slopes, lam_vec, g.reshape(1, HEAD_W), proj3, proj3, proj3)


def _retention_kernel(inner_ref, qd_ref, kd_ref, cd_ref, q_ref, k_ref, v_ref, g_ref, o_ref,
                      state_ref, *, rows):
    ci = pl.program_id(2)

    @pl.when(ci == 0)
    def _():
        state_ref[...] = jnp.zeros(state_ref.shape, F32)

    c = RET_CHUNK
    for p in range(2):
        inner = inner_ref[p]
        qdec = qd_ref[p]
        kdec = kd_ref[p]
        cdec = cd_ref[p]
        state = state_ref[p]
        for n in range(rows // c):
            r0 = n * c
            qc = (q_ref[r0:r0 + c, p * RET_QK:(p + 1) * RET_QK] * (RET_QK ** -0.5)).astype(BF16)
            kf = k_ref[r0:r0 + c, p * RET_QK:(p + 1) * RET_QK]
            vc = v_ref[r0:r0 + c, p * HEAD_W:(p + 1) * HEAD_W].astype(BF16)
            att = _dot_nt(qc, kf.astype(BF16)) * inner
            o = _dot(att.astype(BF16), vc) + _dot(qc, state.astype(BF16)) * qdec
            state = cdec * state + _dot_tn((kf * kdec).astype(BF16), vc)
            mu = jnp.mean(o, axis=-1, keepdims=True)
            oc = o - mu
            var = jnp.mean(oc * oc, axis=-1, keepdims=True)
            gate = g_ref[r0:r0 + c, p * HEAD_W:(p + 1) * HEAD_W]
            y = oc * lax.rsqrt(var + EPS) * (gate * jax.nn.sigmoid(gate))
            o_ref[r0:r0 + c, p * HEAD_W:(p + 1) * HEAD_W] = y.astype(o_ref.dtype)
        state_ref[p] = state


def _retention_tables():
    h = N_HEADS
    c = RET_CHUNK
    log_g = jnp.log1p(-(2.0 ** (-RET_DECAY_BASE - jnp.arange(h, dtype=F32))))
    idx = jnp.arange(c, dtype=F32)
    rel = idx[:, None] - idx[None, :]
    inner = jnp.where(rel >= 0, jnp.exp(log_g[:, None, None] * jnp.maximum(rel, 0.0)), 0.0)
    qd = jnp.broadcast_to(jnp.exp(log_g[:, None] * (idx + 1.0))[:, :, None], (h, c, HEAD_W))
    kd = jnp.broadcast_to(jnp.exp(log_g[:, None] * (c - 1.0 - idx))[:, :, None], (h, c, RET_QK))
    cd = jnp.broadcast_to(jnp.exp(log_g * c)[:, None, None], (h, RET_QK, HEAD_W))
    return inner, qd, kd, cd


def _retention(proj3, *, rows=512):
    b, s, _ = proj3.shape
    rows = _pick(s, rows)
    inner, qd, kd, cd = _retention_tables()
    c = RET_CHUNK
    return pl.pallas_call(
        functools.partial(_retention_kernel, rows=rows),
        grid=(b, N_HEADS // 2, s // rows),
        in_specs=[
            pl.BlockSpec((2, c, c), lambda bi, hp, ci: (hp, 0, 0)),
            pl.BlockSpec((2, c, HEAD_W), lambda bi, hp, ci: (hp, 0, 0)),
            pl.BlockSpec((2, c, RET_QK), lambda bi, hp, ci: (hp, 0, 0)),
            pl.BlockSpec((2, RET_QK, HEAD_W), lambda bi, hp, ci: (hp, 0, 0)),
            pl.BlockSpec((None, rows, 128), lambda bi, hp, ci: (bi, ci, OFF_RT_Q + hp)),
            pl.BlockSpec((None, rows, 128), lambda bi, hp, ci: (bi, ci, OFF_RT_K + hp)),
            pl.BlockSpec((None, rows, 2 * HEAD_W), lambda bi, hp, ci: (bi, ci, OFF_RT_V // 2 + hp)),
            pl.BlockSpec((None, rows, 2 * HEAD_W), lambda bi, hp, ci: (bi, ci, OFF_RT_G // 2 + hp)),
        ],
        out_specs=pl.BlockSpec((None, rows, 2 * HEAD_W), lambda bi, hp, ci: (bi, ci, hp)),
        out_shape=jax.ShapeDtypeStruct((b, s, BRANCH_W), BF16),
        scratch_shapes=[pltpu.VMEM((2, RET_QK, HEAD_W), F32)],
        compiler_params=_params(("parallel", "parallel", "arbitrary")),
        name="retention",
    )(inner, qd, kd, cd, proj3, proj3, proj3, proj3)


def _split3(x):
    hi = x.astype(BF16)
    r1 = x - hi.astype(F32)
    mid = r1.astype(BF16)
    lo = (r1 - mid.astype(F32)).astype(BF16)
    return hi, mid, lo


HG_SUB = 16


def _hgrn2_kernel(lb_ref, gn_ref, q_ref, f_ref, i_ref, gt_ref, o_ref,
                  state_ref, b_sc, k_sc, q_sc, *, rows):
    ci = pl.program_id(2)
    c, sc = HG_CHUNK, HG_SUB

    @pl.when(ci == 0)
    def _():
        state_ref[...] = jnp.zeros(state_ref.shape, F32)

    lb = lb_ref[...]
    gn = gn_ref[...]
    fr = f_ref[...]
    log_f = jnp.log(lb + (1.0 - lb) * jax.nn.sigmoid(fr))
    k_sc[...] = (1.0 - lb) * jax.nn.sigmoid(-fr)
    qraw = q_ref[...]
    q_sc[...] = qraw * jax.nn.sigmoid(qraw)
    ri = lax.broadcasted_iota(jnp.int32, (rows, rows), 0)
    cj = lax.broadcasted_iota(jnp.int32, (rows, rows), 1)
    tri = jnp.where(jnp.logical_and(ri >= cj, (ri // c) == (cj // c)), 1.0, 0.0).astype(BF16)
    hi, mid, lo = _split3(log_f)
    b_sc[...] = _dot(tri, hi) + _dot(tri, mid) + _dot(tri, lo)
    row_sub = lax.broadcasted_iota(jnp.int32, (sc, HEAD_W), 0)

    for n in range(rows // c):
        r0 = n * c
        parts = []
        for sub in range(c // sc):
            s0 = r0 + sub * sc
            b_i = b_sc[s0:s0 + sc, :]
            q_i = q_sc[s0:s0 + sc, :]
            acc = jnp.zeros((sc, HEAD_W), F32)
            for jj in range(sc):
                j = s0 + jj
                d = b_i - b_sc[j:j + 1, :]
                if jj > 0:
                    d = jnp.where(row_sub >= jj, d, NEG_INF)
                a = jnp.sum(q_i * k_sc[j:j + 1, :] * jnp.exp(d), axis=-1, keepdims=True)
                acc = acc + a * i_ref[j:j + 1, :]
            if sub > 0:
                anchor = b_sc[s0:s0 + 1, :]
                qs = (q_i * jnp.exp(b_i - anchor)).astype(BF16)
                ks = (k_sc[r0:s0, :] * jnp.exp(anchor - b_sc[r0:s0, :])).astype(BF16)
                att = _dot_nt(qs, ks)
                acc = acc + _dot(att.astype(BF16), i_ref[r0:s0, :].astype(BF16))
            parts.append(acc)
        o = jnp.concatenate(parts, axis=0)
        bc = b_sc[r0:r0 + c, :]
        state_t = state_ref[...]
        o = o + _dot_nt((q_sc[r0:r0 + c, :] * jnp.exp(bc)).astype(BF16), state_t.astype(BF16))
        blast = b_sc[r0 + c - 1:r0 + c, :]
        kd = (k_sc[r0:r0 + c, :] * jnp.exp(blast - bc)).astype(BF16)
        state_ref[...] = jnp.exp(blast) * state_t + _dot_tn(i_ref[r0:r0 + c, :].astype(BF16), kd)
        ms = jnp.mean(o * o, axis=-1, keepdims=True)
        gate = gt_ref[r0:r0 + c, :]
        y = o * lax.rsqrt(ms + EPS) * gn * (gate * jax.nn.sigmoid(gate))
        o_ref[r0:r0 + c, :] = y.astype(o_ref.dtype)


def _hgrn2(proj3, lb, gn, *, rows=256):
    b, s, _ = proj3.shape
    rows = _pick(s, rows)
    blk = lambda off: pl.BlockSpec((None, rows, HEAD_W), lambda bi, h, ci: (bi, ci, off + h))
    return pl.pallas_call(
        functools.partial(_hgrn2_kernel, rows=rows),
        grid=(b, N_HEADS, s // rows),
        in_specs=[
            pl.BlockSpec((None, 1, HEAD_W), lambda bi, h, ci: (h, 0, 0)),
            pl.BlockSpec((1, HEAD_W), lambda bi, h, ci: (0, 0)),
            blk(OFF_HG_Q), blk(OFF_HG_F), blk(OFF_HG_I), blk(OFF_HG_G),
        ],
        out_specs=pl.BlockSpec((None, rows, HEAD_W), lambda bi, h, ci: (bi, ci, h)),
        out_shape=jax.ShapeDtypeStruct((b, s, BRANCH_W), BF16),
        scratch_shapes=[pltpu.VMEM((HEAD_W, HEAD_W), F32)] + [pltpu.VMEM((rows, HEAD_W), F32)] * 3,
        compiler_params=_params(("parallel", "parallel", "arbitrary")),
        name="hgrn2",
    )(lb.reshape(N_HEADS, 1, HEAD_W), gn.reshape(1, HEAD_W), proj3, proj3, proj3, proj3)


def _dil_kernel(slope_ref, *refs, tile):
    n_g = len(DIL_RATES)
    groups = [refs[5 * g:5 * g + 5] for g in range(n_g)]
    o_ref, o_sc, l_sc = refs[5 * n_g:]
    h = pl.program_id(1)
    blk = pl.program_id(2)
    t = DIL_BLOCK
    d_cur = (lax.broadcasted_iota(jnp.int32, (t, t), 0)
             - lax.broadcasted_iota(jnp.int32, (t, t), 1)).astype(F32)
    d_prev = d_cur + float(t)

    for g, (q_ref, k_ref, v_ref, kh_ref, vh_ref) in enumerate(groups):
        rate = DIL_RATES[g]
        halo = t * rate
        slope = slope_ref[g * N_HEADS + h] * float(rate)
        bias_cur = jnp.where(d_cur >= 0, -slope * d_cur, NEG_INF)
        bias_prev = jnp.where(d_prev <= float(t), -slope * d_prev, NEG_INF)
        bias_both = jnp.concatenate([bias_prev, bias_cur], axis=1)
        bias_first = jnp.concatenate([jnp.full((t, t), NEG_INF, F32), bias_cur], axis=1)

        def rows(start, rate=rate):
            return pl.ds(start, t) if rate == 1 else pl.ds(start, t, stride=rate)

        def units(ug, carry, g=g, rate=rate, halo=halo, q_ref=q_ref, k_ref=k_ref, v_ref=v_ref,
                  kh_ref=kh_ref, vh_ref=vh_ref, bias_both=bias_both, bias_first=bias_first, rows=rows):
            starts, qs, ks, vs, biases = [], [], [], [], []
            for uu in range(DIL_UNROLL):
                u = ug * DIL_UNROLL + uu
                r = u % rate
                mb = u // rate
                start = mb * halo + r
                starts.append(start)
                qs.append((q_ref[rows(start), :] * (HEAD_W ** -0.5)).astype(BF16))
                kp = kh_ref[rows(r), :]
                vp = vh_ref[rows(r), :]
                prev_ok = blk > 0
                if tile > halo:
                    pstart = jnp.maximum(start - halo, r)
                    in_tile = mb > 0
                    kp = jnp.where(in_tile, k_ref[rows(pstart), :], kp)
                    vp = jnp.where(in_tile, v_ref[rows(pstart), :], vp)
                    prev_ok = jnp.logical_or(prev_ok, in_tile)
                ks.append(jnp.concatenate([kp, k_ref[rows(start), :]], axis=0).astype(BF16))
                vs.append(jnp.concatenate([vp, v_ref[rows(start), :]], axis=0).astype(BF16))
                biases.append(jnp.where(prev_ok, bias_both, bias_first))
            ss = [_dot_nt(q, k) + bias for q, k, bias in zip(qs, ks, biases)]
            ms = [jnp.max(s, axis=-1, keepdims=True) for s in ss]
            ps = [jnp.exp(s - m) for s, m in zip(ss, ms)]
            ls = [jnp.sum(p, axis=-1, keepdims=True) for p in ps]
            accs = [_dot(p.astype(BF16), v) for p, v in zip(ps, vs)]
            for start, m, l, acc in zip(starts, ms, ls, accs):
                o = acc / l
                lse = jnp.broadcast_to(m + jnp.log(l), (t, HEAD_W))
                if g > 0:
                    lse_p = l_sc[rows(start), :]
                    lse_n = jnp.maximum(lse, lse_p)
                    lse_n = lse_n + jnp.log(jnp.exp(lse - lse_n) + jnp.exp(lse_p - lse_n))
                    o = o_sc[rows(start), :] * jnp.exp(lse_p - lse_n) + o * jnp.exp(lse - lse_n)
                    lse = lse_n
                o_sc[rows(start), :] = o
                if g < n_g - 1:
                    l_sc[rows(start), :] = lse
            return carry

        lax.fori_loop(0, tile // t // DIL_UNROLL, units, 0)

    o_ref[...] = o_sc[...].astype(o_ref.dtype)


def _dilated_attention(proj3, slopes4):
    b, s, _ = proj3.shape
    t = DIL_BLOCK
    tile = t * max(DIL_RATES)
    assert s % tile == 0
    in_specs, args = [], []
    for g, rate in enumerate(DIL_RATES):
        halo = t * rate
        per = tile // halo

        def cur(off, g=g):
            return pl.BlockSpec((None, tile, HEAD_W),
                                lambda bi, h, blk, sl: (bi, blk, off + g * N_HEADS + h))

        def front(off, g=g, halo=halo, per=per):
            return pl.BlockSpec((None, halo, HEAD_W),
                                lambda bi, h, blk, sl: (bi, jnp.maximum(blk * per - 1, 0), off + g * N_HEADS + h))

        in_specs += [cur(OFF_DL_Q), cur(OFF_DL_K), cur(OFF_DL_V), front(OFF_DL_K), front(OFF_DL_V)]
        args += [proj3] * 5
    grid_spec = pltpu.PrefetchScalarGridSpec(
        num_scalar_prefetch=1,
        grid=(b, N_HEADS, s // tile),
        in_specs=in_specs,
        out_specs=pl.BlockSpec((None, tile, HEAD_W), lambda bi, h, blk, sl: (bi, blk, h)),
        scratch_shapes=[pltpu.VMEM((tile, HEAD_W), F32), pltpu.VMEM((tile, HEAD_W), F32)],
    )
    return pl.pallas_call(
        functools.partial(_dil_kernel, tile=tile),
        grid_spec=grid_spec,
        out_shape=jax.ShapeDtypeStruct((b, s, BRANCH_W), BF16),
        compiler_params=_params(("parallel", "parallel", "parallel")),
        name="dilated",
    )(slopes4[:, 1:].T.reshape(-1), *args)


def _merge_kernel(ya_ref, yr_ref, yc_ref, yd_ref, w_ref, g0_ref, g1_ref, g2_ref, g3_ref, o_ref, wb_ref):
    @pl.when(pl.program_id(1) == 0)
    def _():
        _cast_weights(w_ref, wb_ref)

    acc = None
    for n, (y_ref, g_ref) in enumerate(((ya_ref, g0_ref), (yr_ref, g1_ref), (yc_ref, g2_ref), (yd_ref, g3_ref))):
        t = jax.nn.sigmoid(g_ref[...]) * _dot(y_ref[...], wb_ref[n])
        acc = t if acc is None else acc + t
    o_ref[...] = acc.astype(o_ref.dtype)


def _merge(ys, w_branch, layer, proj, d_model, *, tm=512, tn=1024):
    m = proj.shape[0]
    tm, tn = _pick(m, tm), _pick(d_model, tn)
    gcols = d_model // tn
    goff = OFF_GATES * 128 // tn
    y_spec = pl.BlockSpec((tm, BRANCH_W), lambda j, i: (i, 0))

    def gate_spec(n):
        return pl.BlockSpec((tm, tn), lambda j, i: (i, goff + n * gcols + j))

    return pl.pallas_call(
        _merge_kernel,
        grid=(d_model // tn, m // tm),
        in_specs=[y_spec, y_spec, y_spec, y_spec,
                  pl.BlockSpec((None, N_BRANCH, BRANCH_W, tn), lambda j, i: (layer, 0, 0, j),
                               pipeline_mode=pl.Buffered(1)),
                  gate_spec(0), gate_spec(1), gate_spec(2), gate_spec(3)],
        out_specs=pl.BlockSpec((tm, tn), lambda j, i: (i, j)),
        out_shape=jax.ShapeDtypeStruct((m, d_model), BF16),
        scratch_shapes=[pltpu.VMEM((N_BRANCH, BRANCH_W, tn), BF16)],
        compiler_params=_params(("parallel", "arbitrary")),
        name="merge",
    )(*ys, w_branch, proj, proj, proj, proj)


FFN_SUB_ROWS = 512


def _ffn_up_kernel(x_ref, wa_ref, wg_ref, cw_ref, cb_ref, o_ref, wab_ref, wgb_ref, carry_ref,
                   *, tiles_per_seq):
    i = pl.program_id(1)

    @pl.when(i == 0)
    def _():
        _cast_weights(wa_ref, wab_ref)
        _cast_weights(wg_ref, wgb_ref)

    @pl.when(i % tiles_per_seq == 0)
    def _():
        carry_ref[...] = jnp.zeros(carry_ref.shape, F32)

    tm, tn = o_ref.shape
    sub = min(tm, FFN_SUB_ROWS)
    cw = cw_ref[...]
    cb = cb_ref[...]
    row = lax.broadcasted_iota(jnp.int32, (sub, tn), 0)
    prev = carry_ref[...]
    for r0 in range(0, tm, sub):
        x = x_ref[r0:r0 + sub, :]
        a = _dot(x, wab_ref[...])
        gl = _dot(x, wgb_ref[...])
        a1 = jnp.where(row == 0, prev[7:8, :], pltpu.roll(a, 1, 0))
        a2 = jnp.where(row == 0, prev[6:7, :],
                       jnp.where(row == 1, prev[7:8, :], pltpu.roll(a, 2, 0)))
        conv = cb + cw[0:1, :] * a2 + cw[1:2, :] * a1 + cw[2:3, :] * a
        gelu = 0.5 * conv * (1.0 + lax.erf(conv * (2.0 ** -0.5)))
        o_ref[r0:r0 + sub, :] = (gelu * gl).astype(o_ref.dtype)
        prev = a[sub - 8:sub, :]
    carry_ref[...] = prev


def _ffn_up(x, w_up, layer, conv_w, conv_b, seq, *, tm=2048, tn=256):
    m, kdim = x.shape
    d_ff = w_up.shape[-1] // 2
    tm, tn = _pick(seq, tm), _pick(d_ff, tn)
    nj = d_ff // tn
    w_spec = lambda off: pl.BlockSpec((None, kdim, tn), lambda j, i: (layer, 0, off + j),
                                      pipeline_mode=pl.Buffered(1))
    return pl.pallas_call(
        functools.partial(_ffn_up_kernel, tiles_per_seq=seq // tm),
        grid=(nj, m // tm),
        in_specs=[
            pl.BlockSpec((tm, kdim), lambda j, i: (i, 0)),
            w_spec(0), w_spec(nj),
            pl.BlockSpec((CONV_W, tn), lambda j, i: (0, j)),
            pl.BlockSpec((1, tn), lambda j, i: (0, j)),
        ],
        out_specs=pl.BlockSpec((tm, tn), lambda j, i: (i, j)),
        out_shape=jax.ShapeDtypeStruct((m, d_ff), BF16),
        scratch_shapes=[pltpu.VMEM((kdim, tn), BF16), pltpu.VMEM((kdim, tn), BF16),
                        pltpu.VMEM((8, tn), F32)],
        compiler_params=_params(("parallel", "arbitrary")),
        name="ffn_up",
    )(x, w_up, w_up, conv_w, conv_b.reshape(1, d_ff))


def _alibi_slopes():
    n = 4 * N_HEADS
    slopes = 2.0 ** (-8.0 * jnp.arange(1, n + 1, dtype=F32) / n)
    return slopes.reshape(N_HEADS, 4)


def kernel(x, w_in, diff_lambda, diff_norm_g, hgrn_lb_logits, hgrn_norm_g, w_branch, w_o,
           ln1_g, ln1_b, w_up, conv_w, conv_b, w_down, ln2_g, ln2_b):
    b, s, d = x.shape
    depth = w_in.shape[0]
    m = b * s
    alpha = (2 * depth) ** 0.25
    slopes4 = _alibi_slopes()
    p = jax.nn.softmax(hgrn_lb_logits.astype(F32), axis=0)
    lower_bounds = jnp.cumsum(p, axis=0) - p[0]

    w_down_b = w_down.astype(BF16)
    h = x.reshape(m, d)
    hb = h.astype(BF16)
    for l in range(depth):
        lam_init = 0.8 - 0.6 * math.exp(-0.3 * l)
        proj = _matmul(hb, w_in, l, out_dtype=F32, tm=1024, name="in_proj")
        proj3 = proj.reshape(b, s, -1)
        ya = _diff_attention(proj3, slopes4[:, 0], diff_lambda[l], diff_norm_g[l], lam_init)
        yr = _retention(proj3)
        yc = _hgrn2(proj3, lower_bounds[l], hgrn_norm_g[l])
        yd = _dilated_attention(proj3, slopes4)
        ys = [y.reshape(m, BRANCH_W) for y in (ya, yr, yc, yd)]
        merged = _merge(ys, w_branch, l, proj, d)
        z = _matmul(merged, w_o, l, out_dtype=F32, res=h, alpha=alpha, name="out_proj")
        h, hb = _layernorm(z, ln1_g[l], ln1_b[l])
        act = _ffn_up(hb, w_up, l, conv_w[l], conv_b[l], s)
        z = _matmul(act, w_down_b, l, out_dtype=F32, tn=512, res=h, alpha=alpha, name="ffn_down")
        h, hb = _layernorm(z, ln2_g[l], ln2_b[l])
    return h.reshape(b, s, d)
```

```python
import functools
import math

import jax
import jax.numpy as jnp
from jax import lax
from jax.experimental import pallas as pl
from jax.experimental.pallas import tpu as pltpu

F32 = jnp.float32
BF16 = jnp.bfloat16

N_HEADS = 8
HEAD_W = 128
BRANCH_W = N_HEADS * HEAD_W
N_BRANCH = 4
DA_QK = 64
RET_QK = 64
RET_CHUNK = 128
RET_DECAY_BASE = 5.0
HG_CHUNK = 64
DIL_WINDOWS = (128, 512, 2048)
DIL_RATES = (1, 4, 16)
DIL_KEYS = 129
DIL_BLOCK = 128
DIL_UNROLL = 4
CONV_W = 3
EPS = 1e-5
NEG_INF = float("-inf")

_C = BRANCH_W // 128
OFF_DA_Q, OFF_DA_K, OFF_DA_V = 0, _C, 2 * _C
OFF_RT_Q, OFF_RT_K = 3 * _C, 3 * _C + _C // 2
OFF_RT_V, OFF_RT_G = 4 * _C, 5 * _C
OFF_HG_Q, OFF_HG_F, OFF_HG_I, OFF_HG_G = 6 * _C, 7 * _C, 8 * _C, 9 * _C
OFF_DL_Q, OFF_DL_K, OFF_DL_V = 10 * _C, 13 * _C, 16 * _C
OFF_GATES = 19 * _C

VMEM_LIMIT = 56 * 1024 * 1024


def _params(sem):
    return pltpu.CompilerParams(dimension_semantics=sem, vmem_limit_bytes=VMEM_LIMIT)


def _dot(a, b):
    return jnp.dot(a, b, preferred_element_type=F32)


def _dot_nt(a, b):
    return lax.dot_general(a, b, (((1,), (1,)), ((), ())), preferred_element_type=F32)


def _dot_tn(a, b):
    return lax.dot_general(a, b, (((0,), (0,)), ((), ())), preferred_element_type=F32)


def _pick(dim, pref):
    if dim <= pref:
        return dim
    t = pref
    while t >= 128:
        if dim % t == 0 and t % 128 == 0:
            return t
        t -= 128
    return dim


CAST_ROWS = 512


def _cast_weights(w_ref, wb_ref):
    rows = w_ref.shape[-2]
    step = min(rows, CAST_ROWS)
    for r0 in range(0, rows, step):
        r1 = min(rows, r0 + step)
        wb_ref[..., r0:r1, :] = w_ref[..., r0:r1, :].astype(BF16)


def _mm_kernel(*refs, alpha, has_res):
    if has_res:
        x_ref, w_ref, r_ref, o_ref, wb_ref = refs
    else:
        x_ref, w_ref, o_ref, wb_ref = refs

    if w_ref.dtype == BF16:
        wb_ref = w_ref
    else:
        @pl.when(pl.program_id(1) == 0)
        def _():
            _cast_weights(w_ref, wb_ref)

    acc = _dot(x_ref[...], wb_ref[...])
    if has_res:
        acc = alpha * r_ref[...] + acc
    o_ref[...] = acc.astype(o_ref.dtype)


def _matmul(x, w, layer, *, out_dtype, tm=512, tn=1024, res=None, alpha=1.0, name="mm"):
    m, kdim = x.shape
    n = w.shape[-1]
    tm, tn = _pick(m, tm), _pick(n, tn)
    cast_shape = (16, 128) if w.dtype == BF16 else (kdim, tn)
    in_specs = [
        pl.BlockSpec((tm, kdim), lambda j, i: (i, 0)),
        pl.BlockSpec((None, kdim, tn), lambda j, i: (layer, 0, j), pipeline_mode=pl.Buffered(1)),
    ]
    args = [x, w]
    if res is not None:
        in_specs.append(pl.BlockSpec((tm, tn), lambda j, i: (i, j)))
        args.append(res)
    return pl.pallas_call(
        functools.partial(_mm_kernel, alpha=alpha, has_res=res is not None),
        grid=(n // tn, m // tm),
        in_specs=in_specs,
        out_specs=pl.BlockSpec((tm, tn), lambda j, i: (i, j)),
        out_shape=jax.ShapeDtypeStruct((m, n), out_dtype),
        scratch_shapes=[pltpu.VMEM(cast_shape, BF16)],
        compiler_params=_params(("parallel", "arbitrary")),
        name=name,
    )(*args)


def _ln_kernel(x_ref, g_ref, b_ref, o_ref, ob_ref):
    x = x_ref[...]
    mu = jnp.mean(x, axis=-1, keepdims=True)
    xc = x - mu
    var = jnp.mean(xc * xc, axis=-1, keepdims=True)
    y = xc * lax.rsqrt(var + EPS) * g_ref[...] + b_ref[...]
    o_ref[...] = y
    ob_ref[...] = y.astype(BF16)


def _layernorm(x, g, b, *, tm=256):
    m, d = x.shape
    tm = _pick(m, tm)
    return pl.pallas_call(
        _ln_kernel,
        grid=(m // tm,),
        in_specs=[
            pl.BlockSpec((tm, d), lambda i: (i, 0)),
            pl.BlockSpec((1, d), lambda i: (0, 0)),
            pl.BlockSpec((1, d), lambda i: (0, 0)),
        ],
        out_specs=[pl.BlockSpec((tm, d), lambda i: (i, 0)), pl.BlockSpec((tm, d), lambda i: (i, 0))],
        out_shape=[jax.ShapeDtypeStruct((m, d), F32), jax.ShapeDtypeStruct((m, d), BF16)],
        compiler_params=_params(("parallel",)),
        name="layernorm",
    )(x, g.reshape(1, d), b.reshape(1, d))


LOG2E = 1.4426950408889634


def _split3_f32(x):
    hi = x.astype(BF16).astype(F32)
    mid = (x - hi).astype(BF16).astype(F32)
    lo = (x - hi - mid).astype(BF16).astype(F32)
    return hi, mid, lo


def _lane_fill(lane, first, vals, other):
    out = other
    for n, val in enumerate(vals):
        out = jnp.where(lane == first + n, val, out)
    return out


def _diff_attn_kernel(slope_ref, lam_ref, g_ref, q_ref, k_ref, v_ref, o_ref,
                      m1_ref, a1_ref, m2_ref, a2_ref, kaug_sc, mask_sc, sa_sc, sb_sc, *, t, lam_init):
    h = pl.program_id(1)
    qi = pl.program_id(2)
    slope2 = slope_ref[h] * LOG2E
    lane = lax.broadcasted_iota(jnp.int32, (t, HEAD_W), 1)
    aug_first = (DA_QK, 0)
    owns = (lane < DA_QK, lane >= DA_QK)
    accs = ((m1_ref, a1_ref), (m2_ref, a2_ref))

    @pl.when(qi == 0)
    def _():
        j = lax.broadcasted_iota(jnp.int32, (t, HEAD_W), 0).astype(F32)
        parts = _split3_f32(slope2 * j)
        for idx in range(2):
            kaug_sc[idx] = _lane_fill(lane, aug_first[idx], list(parts) + [1.0, 1.0, 1.0], 0.0)
        d = (lax.broadcasted_iota(jnp.int32, (t, t), 0) - lax.broadcasted_iota(jnp.int32, (t, t), 1))
        mask_sc[...] = jnp.where(d >= 0, 0.0, NEG_INF)

    for m_ref, a_ref in accs:
        m_ref[...] = jnp.full(m_ref.shape, NEG_INF, F32)
        a_ref[...] = jnp.zeros(a_ref.shape, F32)

    q = q_ref[...] * (DA_QK ** -0.5 * LOG2E)
    ones_col = jnp.where(lane == 0, 1.0, 0.0).astype(BF16)

    def scores(ki, s_ref):
        r0 = pl.multiple_of(ki * t, t)
        block_c = jnp.full((1, HEAD_W), -slope2 * ((qi - ki) * t).astype(F32), F32)
        c_parts = _split3_f32(block_c)
        k = k_ref[pl.ds(r0, t), :]
        for idx in range(2):
            q_aug = _lane_fill(lane[0:1], aug_first[idx], [1.0, 1.0, 1.0] + list(c_parts), 0.0)
            qq = jnp.where(owns[idx], q, q_aug).astype(BF16)
            kk = jnp.where(owns[idx], k, kaug_sc[idx]).astype(BF16)
            s_ref[idx] = _dot_nt(qq, kk)

    def update(ki, s_ref, masked):
        r0 = pl.multiple_of(ki * t, t)
        v_aug = jnp.concatenate([v_ref[pl.ds(r0, t), :].astype(BF16), ones_col], axis=1)
        for idx, (m_ref, a_ref) in enumerate(accs):
            s = s_ref[idx]
            if masked:
                s = s + mask_sc[...]
            m_prev = m_ref[...]
            m_new = jnp.maximum(m_prev, jnp.max(s, axis=-1, keepdims=True))
            corr = jnp.exp2(m_prev - m_new)
            p = jnp.exp2(s - jnp.tile(m_new, (1, t // HEAD_W)))
            a_ref[...] = jnp.tile(corr, (1, 2)) * a_ref[...] + _dot(p.astype(BF16), v_aug)
            m_ref[...] = m_new

    scores(0, sa_sc)

    def body(pair, carry):
        k0 = 2 * pair
        scores(k0 + 1, sb_sc)
        update(k0, sa_sc, False)
        scores(k0 + 2, sa_sc)
        update(k0 + 1, sb_sc, False)
        return carry

    lax.fori_loop(0, qi // 2, body, 0)

    @pl.when(qi % 2 == 0)
    def _():
        update(qi, sa_sc, True)

    @pl.when(qi % 2 == 1)
    def _():
        scores(qi, sb_sc)
        update(qi - 1, sa_sc, False)
        update(qi, sb_sc, True)

    lv = lam_ref[...]
    lam = (jnp.exp(jnp.sum(lv[0:1] * lv[1:2], axis=-1, keepdims=True))
           - jnp.exp(jnp.sum(lv[2:3] * lv[3:4], axis=-1, keepdims=True)) + lam_init)
    o1 = a1_ref[:, :HEAD_W] / a1_ref[:, HEAD_W:HEAD_W + 1]
    o2 = a2_ref[:, :HEAD_W] / a2_ref[:, HEAD_W:HEAD_W + 1]
    o = o1 - lam * o2
    ms = jnp.mean(o * o, axis=-1, keepdims=True)
    o = o * lax.rsqrt(ms + EPS) * g_ref[...] * (1.0 - lam_init)
    o_ref[...] = o.astype(o_ref.dtype)


def _diff_attention(proj3, slopes, lam_vec, g, lam_init, *, t=512):
    b, s, _ = proj3.shape
    t = _pick(s, t)
    kern = functools.partial(_diff_attn_kernel, t=t, lam_init=lam_init)
    grid_spec = pltpu.PrefetchScalarGridSpec(
        num_scalar_prefetch=1,
        grid=(b, N_HEADS, s // t),
        in_specs=[
            pl.BlockSpec(lam_vec.shape, lambda bi, h, qi, sl: (0, 0)),
            pl.BlockSpec((1, HEAD_W), lambda bi, h, qi, sl: (0, 0)),
            pl.BlockSpec((None, t, HEAD_W), lambda bi, h, qi, sl: (bi, qi, OFF_DA_Q + h)),
            pl.BlockSpec((None, s, HEAD_W), lambda bi, h, qi, sl: (bi, 0, OFF_DA_K + h)),
            pl.BlockSpec((None, s, HEAD_W), lambda bi, h, qi, sl: (bi, 0, OFF_DA_V + h)),
        ],
        out_specs=pl.BlockSpec((None, t, HEAD_W), lambda bi, h, qi, sl: (bi, qi, h)),
        scratch_shapes=[pltpu.VMEM((t, HEAD_W), F32), pltpu.VMEM((t, 2 * HEAD_W), F32)] * 2
        + [pltpu.VMEM((2, t, HEAD_W), F32), pltpu.VMEM((t, t), F32)]
        + [pltpu.VMEM((2, t, t), F32)] * 2,
    )
    return pl.pallas_call(
        kern,
        grid_spec=grid_spec,
        out_shape=jax.ShapeDtypeStruct((b, s, BRANCH_W), BF16),
        compiler_params=_params(("parallel", "parallel", "arbitrary")),
        name="diff_attention",
    )(slopes, lam_vec, g.reshape(1, HEAD_W), proj3, proj3, proj3)


def _retention_kernel(inner_ref, qd_ref, kd_ref, cd_ref, q_ref, k_ref, v_ref, g_ref, o_ref,
                      state_ref, *, rows):
    ci = pl.program_id(2)

    @pl.when(ci == 0)
    def _():
        state_ref[...] = jnp.zeros(state_ref.shape, F32)

    c = RET_CHUNK
    for p in range(2):
        inner = inner_ref[p]
        qdec = qd_ref[p]
        kdec = kd_ref[p]
        cdec = cd_ref[p]
        state = state_ref[p]
        for n in range(rows // c):
            r0 = n * c
            qc = (q_ref[r0:r0 + c, p * RET_QK:(p + 1) * RET_QK] * (RET_QK ** -0.5)).astype(BF16)
            kf = k_ref[r0:r0 + c, p * RET_QK:(p + 1) * RET_QK]
            vc = v_ref[r0:r0 + c, p * HEAD_W:(p + 1) * HEAD_W].astype(BF16)
            att = _dot_nt(qc, kf.astype(BF16)) * inner
            o = _dot(att.astype(BF16), vc) + _dot(qc, state.astype(BF16)) * qdec
            state = cdec * state + _dot_tn((kf * kdec).astype(BF16), vc)
            mu = jnp.mean(o, axis=-1, keepdims=True)
            oc = o - mu
            var = jnp.mean(oc * oc, axis=-1, keepdims=True)
            gate = g_ref[r0:r0 + c, p * HEAD_W:(p + 1) * HEAD_W]
            y = oc * lax.rsqrt(var + EPS) * (gate * jax.nn.sigmoid(gate))
            o_ref[r0:r0 + c, p * HEAD_W:(p + 1) * HEAD_W] = y.astype(o_ref.dtype)
        state_ref[p] = state


def _retention_tables():
    h = N_HEADS
    c = RET_CHUNK
    log_g = jnp.log1p(-(2.0 ** (-RET_DECAY_BASE - jnp.arange(h, dtype=F32))))
    idx = jnp.arange(c, dtype=F32)
    rel = idx[:, None] - idx[None, :]
    inner = jnp.where(rel >= 0, jnp.exp(log_g[:, None, None] * jnp.maximum(rel, 0.0)), 0.0)
    qd = jnp.broadcast_to(jnp.exp(log_g[:, None] * (idx + 1.0))[:, :, None], (h, c, HEAD_W))
    kd = jnp.broadcast_to(jnp.exp(log_g[:, None] * (c - 1.0 - idx))[:, :, None], (h, c, RET_QK))
    cd = jnp.broadcast_to(jnp.exp(log_g * c)[:, None, None], (h, RET_QK, HEAD_W))
    return inner, qd, kd, cd


def _retention(proj3, *, rows=512):
    b, s, _ = proj3.shape
    rows = _pick(s, rows)
    inner, qd, kd, cd = _retention_tables()
    c = RET_CHUNK
    return pl.pallas_call(
        functools.partial(_retention_kernel, rows=rows),
        grid=(b, N_HEADS // 2, s // rows),
        in_specs=[
            pl.BlockSpec((2, c, c), lambda bi, hp, ci: (hp, 0, 0)),
            pl.BlockSpec((2, c, HEAD_W), lambda bi, hp, ci: (hp, 0, 0)),
            pl.BlockSpec((2, c, RET_QK), lambda bi, hp, ci: (hp, 0, 0)),
            pl.BlockSpec((2, RET_QK, HEAD_W), lambda bi, hp, ci: (hp, 0, 0)),
            pl.BlockSpec((None, rows, 128), lambda bi, hp, ci: (bi, ci, OFF_RT_Q + hp)),
            pl.BlockSpec((None, rows, 128), lambda bi, hp, ci: (bi, ci, OFF_RT_K + hp)),
            pl.BlockSpec((None, rows, 2 * HEAD_W), lambda bi, hp, ci: (bi, ci, OFF_RT_V // 2 + hp)),
            pl.BlockSpec((None, rows, 2 * HEAD_W), lambda bi, hp, ci: (bi, ci, OFF_RT_G // 2 + hp)),
        ],
        out_specs=pl.BlockSpec((None, rows, 2 * HEAD_W), lambda bi, hp, ci: (bi, ci, hp)),
        out_shape=jax.ShapeDtypeStruct((b, s, BRANCH_W), BF16),
        scratch_shapes=[pltpu.VMEM((2, RET_QK, HEAD_W), F32)],
        compiler_params=_params(("parallel", "parallel", "arbitrary")),
        name="retention",
    )(inner, qd, kd, cd, proj3, proj3, proj3, proj3)


def _split3(x):
    hi = x.astype(BF16)
    r1 = x - hi.astype(F32)
    mid = r1.astype(BF16)
    lo = (r1 - mid.astype(F32)).astype(BF16)
    return hi, mid, lo


HG_SUB = 16


def _hgrn2_kernel(lb_ref, gn_ref, q_ref, f_ref, i_ref, gt_ref, o_ref,
                  state_ref, b_sc, k_sc, q_sc, *, rows):
    ci = pl.program_id(2)
    c, sc = HG_CHUNK, HG_SUB

    @pl.when(ci == 0)
    def _():
        state_ref[...] = jnp.zeros(state_ref.shape, F32)

    lb = lb_ref[...]
    gn = gn_ref[...]
    fr = f_ref[...]
    log_f = jnp.log(lb + (1.0 - lb) * jax.nn.sigmoid(fr))
    k_sc[...] = (1.0 - lb) * jax.nn.sigmoid(-fr)
    qraw = q_ref[...]
    q_sc[...] = qraw * jax.nn.sigmoid(qraw)
    ri = lax.broadcasted_iota(jnp.int32, (rows, rows), 0)
    cj = lax.broadcasted_iota(jnp.int32, (rows, rows), 1)
    tri = jnp.where(jnp.logical_and(ri >= cj, (ri // c) == (cj // c)), 1.0, 0.0).astype(BF16)
    hi, mid, lo = _split3(log_f)
    b_sc[...] = _dot(tri, hi) + _dot(tri, mid) + _dot(tri, lo)
    hs = sc // 2
    row_half = lax.broadcasted_iota(jnp.int32, (hs, HEAD_W), 0)
    row_chunk = lax.broadcasted_iota(jnp.int32, (c, HEAD_W), 0)

    for n in range(rows // c):
        r0 = n * c
        parts = []
        for sub in range(c // sc):
            s0 = r0 + sub * sc
            b_i = b_sc[s0:s0 + sc, :]
            q_i = q_sc[s0:s0 + sc, :]
            halves = [(b_i[:hs], q_i[:hs]), (b_i[hs:], q_i[hs:])]
            acc_h = [jnp.zeros((hs, HEAD_W), F32), jnp.zeros((hs, HEAD_W), F32)]
            for jj in range(sc):
                j = s0 + jj
                b_j, k_j, v_j = b_sc[j:j + 1, :], k_sc[j:j + 1, :], i_ref[j:j + 1, :]
                for hh, (b_h, q_h) in enumerate(halves):
                    first = jj - hh * hs
                    if first >= hs:
                        continue
                    d = b_h - b_j
                    if first > 0:
                        d = jnp.where(row_half >= first, d, NEG_INF)
                    a = jnp.sum(q_h * k_j * jnp.exp(d), axis=-1, keepdims=True)
                    acc_h[hh] = acc_h[hh] + a * v_j
            parts.append(jnp.concatenate(acc_h, axis=0))
        o = jnp.concatenate(parts, axis=0)
        bc = b_sc[r0:r0 + c, :]
        qc = q_sc[r0:r0 + c, :]
        kc = k_sc[r0:r0 + c, :]
        vc = i_ref[r0:r0 + c, :].astype(BF16)
        q_segs, k_segs = [], []
        for sub in range(1, c // sc):
            anchor = b_sc[r0 + sub * sc:r0 + sub * sc + 1, :]
            in_sub = jnp.logical_and(row_chunk >= sub * sc, row_chunk < (sub + 1) * sc)
            q_segs.append(qc * jnp.exp(jnp.where(in_sub, bc - anchor, NEG_INF)))
            k_segs.append(kc * jnp.exp(jnp.where(row_chunk < sub * sc, anchor - bc, NEG_INF)))
        att = _dot_nt(jnp.concatenate(q_segs, axis=1).astype(BF16),
                      jnp.concatenate(k_segs, axis=1).astype(BF16))
        state_t = state_ref[...]
        o = (o + _dot(att.astype(BF16), vc)
             + _dot_nt((qc * jnp.exp(bc)).astype(BF16), state_t.astype(BF16)))
        blast = b_sc[r0 + c - 1:r0 + c, :]
        kd = (kc * jnp.exp(blast - bc)).astype(BF16)
        state_ref[...] = jnp.exp(blast) * state_t + _dot_tn(vc, kd)
        ms = jnp.mean(o * o, axis=-1, keepdims=True)
        gate = gt_ref[r0:r0 + c, :]
        y = o * lax.rsqrt(ms + EPS) * gn * (gate * jax.nn.sigmoid(gate))
        o_ref[r0:r0 + c, :] = y.astype(o_ref.dtype)


def _hgrn2(proj3, lb, gn, *, rows=256):
    b, s, _ = proj3.shape
    rows = _pick(s, rows)
    blk = lambda off: pl.BlockSpec((None, rows, HEAD_W), lambda bi, h, ci: (bi, ci, off + h))
    return pl.pallas_call(
        functools.partial(_hgrn2_kernel, rows=rows),
        grid=(b, N_HEADS, s // rows),
        in_specs=[
            pl.BlockSpec((None, 1, HEAD_W), lambda bi, h, ci: (h, 0, 0)),
            pl.BlockSpec((1, HEAD_W), lambda bi, h, ci: (0, 0)),
            blk(OFF_HG_Q), blk(OFF_HG_F), blk(OFF_HG_I), blk(OFF_HG_G),
        ],
        out_specs=pl.BlockSpec((None, rows, HEAD_W), lambda bi, h, ci: (bi, ci, h)),
        out_shape=jax.ShapeDtypeStruct((b, s, BRANCH_W), BF16),
        scratch_shapes=[pltpu.VMEM((HEAD_W, HEAD_W), F32)] + [pltpu.VMEM((rows, HEAD_W), F32)] * 3,
        compiler_params=_params(("parallel", "parallel", "arbitrary")),
        name="hgrn2",
    )(lb.reshape(N_HEADS, 1, HEAD_W), gn.reshape(1, HEAD_W), proj3, proj3, proj3, proj3)


def _dil_kernel(slope_ref, *refs, tile):
    n_g = len(DIL_RATES)
    groups = [refs[5 * g:5 * g + 5] for g in range(n_g)]
    o_ref, o_sc, l_sc = refs[5 * n_g:]
    h = pl.program_id(1)
    blk = pl.program_id(2)
    t = DIL_BLOCK
    d_cur = (lax.broadcasted_iota(jnp.int32, (t, t), 0)
             - lax.broadcasted_iota(jnp.int32, (t, t), 1)).astype(F32)
    d_prev = d_cur + float(t)

    for g, (q_ref, k_ref, v_ref, kh_ref, vh_ref) in enumerate(groups):
        rate = DIL_RATES[g]
        halo = t * rate
        slope = slope_ref[g * N_HEADS + h] * float(rate)
        bias_cur = jnp.where(d_cur >= 0, -slope * d_cur, NEG_INF)
        bias_prev = jnp.where(d_prev <= float(t), -slope * d_prev, NEG_INF)
        bias_both = jnp.concatenate([bias_prev, bias_cur], axis=1)
        bias_first = jnp.concatenate([jnp.full((t, t), NEG_INF, F32), bias_cur], axis=1)

        def rows(start, rate=rate):
            return pl.ds(start, t) if rate == 1 else pl.ds(start, t, stride=rate)

        def units(ug, carry, g=g, rate=rate, halo=halo, q_ref=q_ref, k_ref=k_ref, v_ref=v_ref,
                  kh_ref=kh_ref, vh_ref=vh_ref, bias_both=bias_both, bias_first=bias_first, rows=rows):
            starts, qs, ks, vs, biases = [], [], [], [], []
            for uu in range(DIL_UNROLL):
                u = ug * DIL_UNROLL + uu
                r = u % rate
                mb = u // rate
                start = mb * halo + r
                starts.append(start)
                qs.append((q_ref[rows(start), :] * (HEAD_W ** -0.5)).astype(BF16))
                kp = kh_ref[rows(r), :]
                vp = vh_ref[rows(r), :]
                prev_ok = blk > 0
                if tile > halo:
                    pstart = jnp.maximum(start - halo, r)
                    in_tile = mb > 0
                    kp = jnp.where(in_tile, k_ref[rows(pstart), :], kp)
                    vp = jnp.where(in_tile, v_ref[rows(pstart), :], vp)
                    prev_ok = jnp.logical_or(prev_ok, in_tile)
                ks.append(jnp.concatenate([kp, k_ref[rows(start), :]], axis=0).astype(BF16))
                vs.append(jnp.concatenate([vp, v_ref[rows(start), :]], axis=0).astype(BF16))
                biases.append(jnp.where(prev_ok, bias_both, bias_first))
            ss = [_dot_nt(q, k) + bias for q, k, bias in zip(qs, ks, biases)]
            ms = [jnp.max(s, axis=-1, keepdims=True) for s in ss]
            ps = [jnp.exp(s - m) for s, m in zip(ss, ms)]
            ls = [jnp.sum(p, axis=-1, keepdims=True) for p in ps]
            accs = [_dot(p.astype(BF16), v) for p, v in zip(ps, vs)]
            for start, m, l, acc in zip(starts, ms, ls, accs):
                o = acc / l
                lse = jnp.broadcast_to(m + jnp.log(l), (t, HEAD_W))
                if g > 0:
                    lse_p = l_sc[rows(start), :]
                    lse_n = jnp.maximum(lse, lse_p)
                    lse_n = lse_n + jnp.log(jnp.exp(lse - lse_n) + jnp.exp(lse_p - lse_n))
                    o = o_sc[rows(start), :] * jnp.exp(lse_p - lse_n) + o * jnp.exp(lse - lse_n)
                    lse = lse_n
                o_sc[rows(start), :] = o
                if g < n_g - 1:
                    l_sc[rows(start), :] = lse
            return carry

        lax.fori_loop(0, tile // t // DIL_UNROLL, units, 0)

    o_ref[...] = o_sc[...].astype(o_ref.dtype)


def _dilated_attention(proj3, slopes4):
    b, s, _ = proj3.shape
    t = DIL_BLOCK
    tile = t * max(DIL_RATES)
    assert s % tile == 0
    in_specs, args = [], []
    for g, rate in enumerate(DIL_RATES):
        halo = t * rate
        per = tile // halo

        def cur(off, g=g):
            return pl.BlockSpec((None, tile, HEAD_W),
                                lambda bi, h, blk, sl: (bi, blk, off + g * N_HEADS + h))

        def front(off, g=g, halo=halo, per=per):
            return pl.BlockSpec((None, halo, HEAD_W),
                                lambda bi, h, blk, sl: (bi, jnp.maximum(blk * per - 1, 0), off + g * N_HEADS + h))

        in_specs += [cur(OFF_DL_Q), cur(OFF_DL_K), cur(OFF_DL_V), front(OFF_DL_K), front(OFF_DL_V)]
        args += [proj3] * 5
    grid_spec = pltpu.PrefetchScalarGridSpec(
        num_scalar_prefetch=1,
        grid=(b, N_HEADS, s // tile),
        in_specs=in_specs,
        out_specs=pl.BlockSpec((None, tile, HEAD_W), lambda bi, h, blk, sl: (bi, blk, h)),
        scratch_shapes=[pltpu.VMEM((tile, HEAD_W), F32), pltpu.VMEM((tile, HEAD_W), F32)],
    )
    return pl.pallas_call(
        functools.partial(_dil_kernel, tile=tile),
        grid_spec=grid_spec,
        out_shape=jax.ShapeDtypeStruct((b, s, BRANCH_W), BF16),
        compiler_params=_params(("parallel", "parallel", "parallel")),
        name="dilated",
    )(slopes4[:, 1:].T.reshape(-1), *args)


def _merge_kernel(ya_ref, yr_ref, yc_ref, yd_ref, w_ref, g0_ref, g1_ref, g2_ref, g3_ref, o_ref, wb_ref):
    @pl.when(pl.program_id(1) == 0)
    def _():
        _cast_weights(w_ref, wb_ref)

    acc = None
    for n, (y_ref, g_ref) in enumerate(((ya_ref, g0_ref), (yr_ref, g1_ref), (yc_ref, g2_ref), (yd_ref, g3_ref))):
        t = jax.nn.sigmoid(g_ref[...]) * _dot(y_ref[...], wb_ref[n])
        acc = t if acc is None else acc + t
    o_ref[...] = acc.astype(o_ref.dtype)


def _merge(ys, w_branch, layer, proj, d_model, *, tm=512, tn=1024):
    m = proj.shape[0]
    tm, tn = _pick(m, tm), _pick(d_model, tn)
    gcols = d_model // tn
    goff = OFF_GATES * 128 // tn
    y_spec = pl.BlockSpec((tm, BRANCH_W), lambda j, i: (i, 0))

    def gate_spec(n):
        return pl.BlockSpec((tm, tn), lambda j, i: (i, goff + n * gcols + j))

    return pl.pallas_call(
        _merge_kernel,
        grid=(d_model // tn, m // tm),
        in_specs=[y_spec, y_spec, y_spec, y_spec,
                  pl.BlockSpec((None, N_BRANCH, BRANCH_W, tn), lambda j, i: (layer, 0, 0, j),
                               pipeline_mode=pl.Buffered(1)),
                  gate_spec(0), gate_spec(1), gate_spec(2), gate_spec(3)],
        out_specs=pl.BlockSpec((tm, tn), lambda j, i: (i, j)),
        out_shape=jax.ShapeDtypeStruct((m, d_model), BF16),
        scratch_shapes=[pltpu.VMEM((N_BRANCH, BRANCH_W, tn), BF16)],
        compiler_params=_params(("parallel", "arbitrary")),
        name="merge",
    )(*ys, w_branch, proj, proj, proj, proj)


FFN_SUB_ROWS = 512


def _ffn_up_kernel(x_ref, wa_ref, wg_ref, cw_ref, cb_ref, o_ref, wab_ref, wgb_ref, carry_ref,
                   *, tiles_per_seq):
    i = pl.program_id(1)

    @pl.when(i == 0)
    def _():
        _cast_weights(wa_ref, wab_ref)
        _cast_weights(wg_ref, wgb_ref)

    @pl.when(i % tiles_per_seq == 0)
    def _():
        carry_ref[...] = jnp.zeros(carry_ref.shape, F32)

    tm, tn = o_ref.shape
    sub = min(tm, FFN_SUB_ROWS)
    cw = cw_ref[...]
    cb = cb_ref[...]
    row = lax.broadcasted_iota(jnp.int32, (sub, tn), 0)
    prev = carry_ref[...]
    for r0 in range(0, tm, sub):
        x = x_ref[r0:r0 + sub, :]
        a = _dot(x, wab_ref[...])
        gl = _dot(x, wgb_ref[...])
        a1 = jnp.where(row == 0, prev[7:8, :], pltpu.roll(a, 1, 0))
        a2 = jnp.where(row == 0, prev[6:7, :],
                       jnp.where(row == 1, prev[7:8, :], pltpu.roll(a, 2, 0)))
        conv = cb + cw[0:1, :] * a2 + cw[1:2, :] * a1 + cw[2:3, :] * a
        gelu = 0.5 * conv * (1.0 + lax.erf(conv * (2.0 ** -0.5)))
        o_ref[r0:r0 + sub, :] = (gelu * gl).astype(o_ref.dtype)
        prev = a[sub - 8:sub, :]
    carry_ref[...] = prev


def _ffn_up(x, w_up, layer, conv_w, conv_b, seq, *, tm=2048, tn=256):
    m, kdim = x.shape
    d_ff = w_up.shape[-1] // 2
    tm, tn = _pick(seq, tm), _pick(d_ff, tn)
    nj = d_ff // tn
    w_spec = lambda off: pl.BlockSpec((None, kdim, tn), lambda j, i: (layer, 0, off + j),
                                      pipeline_mode=pl.Buffered(1))
    return pl.pallas_call(
        functools.partial(_ffn_up_kernel, tiles_per_seq=seq // tm),
        grid=(nj, m // tm),
        in_specs=[
            pl.BlockSpec((tm, kdim), lambda j, i: (i, 0)),
            w_spec(0), w_spec(nj),
            pl.BlockSpec((CONV_W, tn), lambda j, i: (0, j)),
            pl.BlockSpec((1, tn), lambda j, i: (0, j)),
        ],
        out_specs=pl.BlockSpec((tm, tn), lambda j, i: (i, j)),
        out_shape=jax.ShapeDtypeStruct((m, d_ff), BF16),
        scratch_shapes=[pltpu.VMEM((kdim, tn), BF16), pltpu.VMEM((kdim, tn), BF16),
                        pltpu.VMEM((8, tn), F32)],
        compiler_params=_params(("parallel", "arbitrary")),
        name="ffn_up",
    )(x, w_up, w_up, conv_w, conv_b.reshape(1, d_ff))


def _alibi_slopes():
    n = 4 * N_HEADS
    slopes = 2.0 ** (-8.0 * jnp.arange(1, n + 1, dtype=F32) / n)
    return slopes.reshape(N_HEADS, 4)


def kernel(x, w_in, diff_lambda, diff_norm_g, hgrn_lb_logits, hgrn_norm_g, w_branch, w_o,
           ln1_g, ln1_b, w_up, conv_w, conv_b, w_down, ln2_g, ln2_b):
    b, s, d = x.shape
    depth = w_in.shape[0]
    m = b * s
    alpha = (2 * depth) ** 0.25
    slopes4 = _alibi_slopes()
    p = jax.nn.softmax(hgrn_lb_logits.astype(F32), axis=0)
    lower_bounds = jnp.cumsum(p, axis=0) - p[0]

    w_down_b = w_down.astype(BF16)
    h = x.reshape(m, d)
    hb = h.astype(BF16)
    for l in range(depth):
        lam_init = 0.8 - 0.6 * math.exp(-0.3 * l)
        proj = _matmul(hb, w_in, l, out_dtype=F32, tm=1024, name="in_proj")
        proj3 = proj.reshape(b, s, -1)
        ya = _diff_attention(proj3, slopes4[:, 0], diff_lambda[l], diff_norm_g[l], lam_init)
        yr = _retention(proj3)
        yc = _hgrn2(proj3, lower_bounds[l], hgrn_norm_g[l])
        yd = _dilated_attention(proj3, slopes4)
        ys = [y.reshape(m, BRANCH_W) for y in (ya, yr, yc, yd)]
        merged = _merge(ys, w_branch, l, proj, d)
        z = _matmul(merged, w_o, l, out_dtype=F32, res=h, alpha=alpha, name="out_proj")
        h, hb = _layernorm(z, ln1_g[l], ln1_b[l])
        act = _ffn_up(hb, w_up, l, conv_w[l], conv_b[l], s)
        z = _matmul(act, w_down_b, l, out_dtype=F32, tn=512, res=h, alpha=alpha, name="ffn_down")
        h, hb = _layernorm(z, ln2_g[l], ln2_b[l])
    return h.reshape(b, s, d)
```

```python
import functools
import math

import jax
import jax.numpy as jnp
from jax import lax
from jax.experimental import pallas as pl
from jax.experimental.pallas import tpu as pltpu

F32 = jnp.float32
BF16 = jnp.bfloat16

N_HEADS = 8
HEAD_W = 128
BRANCH_W = N_HEADS * HEAD_W
N_BRANCH = 4
DA_QK = 64
RET_QK = 64
RET_CHUNK = 128
RET_DECAY_BASE = 5.0
HG_CHUNK = 64
DIL_WINDOWS = (128, 512, 2048)
DIL_RATES = (1, 4, 16)
DIL_KEYS = 129
DIL_BLOCK = 128
DIL_UNROLL = 4
CONV_W = 3
EPS = 1e-5
NEG_INF = float("-inf")

_C = BRANCH_W // 128
OFF_DA_Q, OFF_DA_K, OFF_DA_V = 0, _C, 2 * _C
OFF_RT_Q, OFF_RT_K = 3 * _C, 3 * _C + _C // 2
OFF_RT_V, OFF_RT_G = 4 * _C, 5 * _C
OFF_HG_Q, OFF_HG_F, OFF_HG_I, OFF_HG_G = 6 * _C, 7 * _C, 8 * _C, 9 * _C
OFF_DL_Q, OFF_DL_K, OFF_DL_V = 10 * _C, 13 * _C, 16 * _C
OFF_GATES = 19 * _C

VMEM_LIMIT = 56 * 1024 * 1024


def _params(sem):
    return pltpu.CompilerParams(dimension_semantics=sem, vmem_limit_bytes=VMEM_LIMIT)


def _dot(a, b):
    return jnp.dot(a, b, preferred_element_type=F32)


def _dot_nt(a, b):
    return lax.dot_general(a, b, (((1,), (1,)), ((), ())), preferred_element_type=F32)


def _dot_tn(a, b):
    return lax.dot_general(a, b, (((0,), (0,)), ((), ())), preferred_element_type=F32)


def _pick(dim, pref):
    if dim <= pref:
        return dim
    t = pref
    while t >= 128:
        if dim % t == 0 and t % 128 == 0:
            return t
        t -= 128
    return dim


CAST_ROWS = 512


def _cast_weights(w_ref, wb_ref):
    rows = w_ref.shape[-2]
    step = min(rows, CAST_ROWS)
    for r0 in range(0, rows, step):
        r1 = min(rows, r0 + step)
        wb_ref[..., r0:r1, :] = w_ref[..., r0:r1, :].astype(BF16)


def _stage_weights(w_hbm, layer, first_col, n_blocks, stage_ref, wb_ref, sem):
    j = pl.program_id(0)
    tn = stage_ref.shape[-1]

    def copy(jj):
        col = pl.multiple_of(first_col + jj * tn, tn)
        return pltpu.make_async_copy(w_hbm.at[layer, :, pl.ds(col, tn)], stage_ref, sem)

    @pl.when(pl.program_id(1) == 0)
    def _():
        @pl.when(j == 0)
        def _():
            copy(0).start()

        copy(j).wait()
        _cast_weights(stage_ref, wb_ref)

        @pl.when(j + 1 < n_blocks)
        def _():
            copy(j + 1).start()


def _mm_kernel(*refs, alpha, has_res, layer, n_blocks):
    if has_res:
        x_ref, w_ref, r_ref, o_ref, *scratch = refs
    else:
        x_ref, w_ref, o_ref, *scratch = refs

    if scratch:
        stage_ref, wb_ref, sem = scratch
        _stage_weights(w_ref, layer, 0, n_blocks, stage_ref, wb_ref, sem)
    else:
        wb_ref = w_ref

    acc = _dot(x_ref[...], wb_ref[...])
    if has_res:
        acc = alpha * r_ref[...] + acc
    o_ref[...] = acc.astype(o_ref.dtype)


def _matmul(x, w, layer, *, out_dtype, tm=512, tn=1024, res=None, alpha=1.0, name="mm"):
    m, kdim = x.shape
    n = w.shape[-1]
    tm, tn = _pick(m, tm), _pick(n, tn)
    if w.dtype == BF16:
        w_spec = pl.BlockSpec((None, kdim, tn), lambda j, i: (layer, 0, j), pipeline_mode=pl.Buffered(1))
        scratch = []
    else:
        w_spec = pl.BlockSpec(memory_space=pl.ANY)
        scratch = [pltpu.VMEM((kdim, tn), F32), pltpu.VMEM((kdim, tn), BF16), pltpu.SemaphoreType.DMA(())]
    in_specs = [pl.BlockSpec((tm, kdim), lambda j, i: (i, 0)), w_spec]
    args = [x, w]
    if res is not None:
        in_specs.append(pl.BlockSpec((tm, tn), lambda j, i: (i, j)))
        args.append(res)
    return pl.pallas_call(
        functools.partial(_mm_kernel, alpha=alpha, has_res=res is not None, layer=layer, n_blocks=n // tn),
        grid=(n // tn, m // tm),
        in_specs=in_specs,
        out_specs=pl.BlockSpec((tm, tn), lambda j, i: (i, j)),
        out_shape=jax.ShapeDtypeStruct((m, n), out_dtype),
        scratch_shapes=scratch,
        compiler_params=_params(("arbitrary", "arbitrary")),
        name=name,
    )(*args)


def _ln_kernel(x_ref, g_ref, b_ref, o_ref, ob_ref):
    x = x_ref[...]
    mu = jnp.mean(x, axis=-1, keepdims=True)
    xc = x - mu
    var = jnp.mean(xc * xc, axis=-1, keepdims=True)
    y = xc * lax.rsqrt(var + EPS) * g_ref[...] + b_ref[...]
    o_ref[...] = y
    ob_ref[...] = y.astype(BF16)


def _layernorm(x, g, b, *, tm=256):
    m, d = x.shape
    tm = _pick(m, tm)
    return pl.pallas_call(
        _ln_kernel,
        grid=(m // tm,),
        in_specs=[
            pl.BlockSpec((tm, d), lambda i: (i, 0)),
            pl.BlockSpec((1, d), lambda i: (0, 0)),
            pl.BlockSpec((1, d), lambda i: (0, 0)),
        ],
        out_specs=[pl.BlockSpec((tm, d), lambda i: (i, 0)), pl.BlockSpec((tm, d), lambda i: (i, 0))],
        out_shape=[jax.ShapeDtypeStruct((m, d), F32), jax.ShapeDtypeStruct((m, d), BF16)],
        compiler_params=_params(("parallel",)),
        name="layernorm",
    )(x, g.reshape(1, d), b.reshape(1, d))


LOG2E = 1.4426950408889634


def _split3_f32(x):
    hi = x.astype(BF16).astype(F32)
    mid = (x - hi).astype(BF16).astype(F32)
    lo = (x - hi - mid).astype(BF16).astype(F32)
    return hi, mid, lo


def _lane_fill(lane, first, vals, other):
    out = other
    for n, val in enumerate(vals):
        out = jnp.where(lane == first + n, val, out)
    return out


def _diff_attn_kernel(slope_ref, lam_ref, g_ref, q_ref, k_ref, v_ref, o_ref,
                      m1_ref, a1_ref, m2_ref, a2_ref, kaug_sc, mask_sc, sa_sc, sb_sc, *, t, lam_init):
    h = pl.program_id(1)
    qi = pl.program_id(2)
    slope2 = slope_ref[h] * LOG2E
    lane = lax.broadcasted_iota(jnp.int32, (t, HEAD_W), 1)
    aug_first = (DA_QK, 0)
    owns = (lane < DA_QK, lane >= DA_QK)
    accs = ((m1_ref, a1_ref), (m2_ref, a2_ref))

    @pl.when(qi == 0)
    def _():
        j = lax.broadcasted_iota(jnp.int32, (t, HEAD_W), 0).astype(F32)
        parts = _split3_f32(slope2 * j)
        for idx in range(2):
            kaug_sc[idx] = _lane_fill(lane, aug_first[idx], list(parts) + [1.0, 1.0, 1.0], 0.0)
        d = (lax.broadcasted_iota(jnp.int32, (t, t), 0) - lax.broadcasted_iota(jnp.int32, (t, t), 1))
        mask_sc[...] = jnp.where(d >= 0, 0.0, NEG_INF)

    for m_ref, a_ref in accs:
        m_ref[...] = jnp.full(m_ref.shape, NEG_INF, F32)
        a_ref[...] = jnp.zeros(a_ref.shape, F32)

    q = q_ref[...] * (DA_QK ** -0.5 * LOG2E)
    ones_col = jnp.where(lane == 0, 1.0, 0.0).astype(BF16)

    def scores(ki, s_ref):
        r0 = pl.multiple_of(ki * t, t)
        block_c = jnp.full((1, HEAD_W), -slope2 * ((qi - ki) * t).astype(F32), F32)
        c_parts = _split3_f32(block_c)
        k = k_ref[pl.ds(r0, t), :]
        for idx in range(2):
            q_aug = _lane_fill(lane[0:1], aug_first[idx], [1.0, 1.0, 1.0] + list(c_parts), 0.0)
            qq = jnp.where(owns[idx], q, q_aug).astype(BF16)
            kk = jnp.where(owns[idx], k, kaug_sc[idx]).astype(BF16)
            s_ref[idx] = _dot_nt(qq, kk)

    def update(ki, s_ref, masked):
        r0 = pl.multiple_of(ki * t, t)
        v_aug = jnp.concatenate([v_ref[pl.ds(r0, t), :].astype(BF16), ones_col], axis=1)
        for idx, (m_ref, a_ref) in enumerate(accs):
            s = s_ref[idx]
            if masked:
                s = s + mask_sc[...]
            m_prev = m_ref[...]
            m_new = jnp.maximum(m_prev, jnp.max(s, axis=-1, keepdims=True))
            corr = jnp.exp2(m_prev - m_new)
            p = jnp.exp2(s - jnp.tile(m_new, (1, t // HEAD_W)))
            a_ref[...] = jnp.tile(corr, (1, 2)) * a_ref[...] + _dot(p.astype(BF16), v_aug)
            m_ref[...] = m_new

    scores(0, sa_sc)

    def body(pair, carry):
        k0 = 2 * pair
        scores(k0 + 1, sb_sc)
        update(k0, sa_sc, False)
        scores(k0 + 2, sa_sc)
        update(k0 + 1, sb_sc, False)
        return carry

    lax.fori_loop(0, qi // 2, body, 0)

    @pl.when(qi % 2 == 0)
    def _():
        update(qi, sa_sc, True)

    @pl.when(qi % 2 == 1)
    def _():
        scores(qi, sb_sc)
        update(qi - 1, sa_sc, False)
        update(qi, sb_sc, True)

    lv = lam_ref[...]
    lam = (jnp.exp(jnp.sum(lv[0:1] * lv[1:2], axis=-1, keepdims=True))
           - jnp.exp(jnp.sum(lv[2:3] * lv[3:4], axis=-1, keepdims=True)) + lam_init)
    o1 = a1_ref[:, :HEAD_W] / a1_ref[:, HEAD_W:HEAD_W + 1]
    o2 = a2_ref[:, :HEAD_W] / a2_ref[:, HEAD_W:HEAD_W + 1]
    o = o1 - lam * o2
    ms = jnp.mean(o * o, axis=-1, keepdims=True)
    o = o * lax.rsqrt(ms + EPS) * g_ref[...] * (1.0 - lam_init)
    o_ref[...] = o.astype(o_ref.dtype)


def _diff_attention(proj3, slopes, lam_vec, g, lam_init, *, t=512):
    b, s, _ = proj3.shape
    t = _pick(s, t)
    kern = functools.partial(_diff_attn_kernel, t=t, lam_init=lam_init)
    grid_spec = pltpu.PrefetchScalarGridSpec(
        num_scalar_prefetch=1,
        grid=(b, N_HEADS, s // t),
        in_specs=[
            pl.BlockSpec(lam_vec.shape, lambda bi, h, qi, sl: (0, 0)),
            pl.BlockSpec((1, HEAD_W), lambda bi, h, qi, sl: (0, 0)),
            pl.BlockSpec((None, t, HEAD_W), lambda bi, h, qi, sl: (bi, qi, OFF_DA_Q + h)),
            pl.BlockSpec((None, s, HEAD_W), lambda bi, h, qi, sl: (bi, 0, OFF_DA_K + h)),
            pl.BlockSpec((None, s, HEAD_W), lambda bi, h, qi, sl: (bi, 0, OFF_DA_V + h)),
        ],
        out_specs=pl.BlockSpec((None, t, HEAD_W), lambda bi, h, qi, sl: (bi, qi, h)),
        scratch_shapes=[pltpu.VMEM((t, HEAD_W), F32), pltpu.VMEM((t, 2 * HEAD_W), F32)] * 2
        + [pltpu.VMEM((2, t, HEAD_W), F32), pltpu.VMEM((t, t), F32)]
        + [pltpu.VMEM((2, t, t), F32)] * 2,
    )
    return pl.pallas_call(
        kern,
        grid_spec=grid_spec,
        out_shape=jax.ShapeDtypeStruct((b, s, BRANCH_W), BF16),
        compiler_params=_params(("parallel", "parallel", "arbitrary")),
        name="diff_attention",
    )(slopes, lam_vec, g.reshape(1, HEAD_W), proj3, proj3, proj3)


def _retention_kernel(inner_ref, qd_ref, kd_ref, cd_ref, q_ref, k_ref, v_ref, g_ref, o_ref,
                      state_ref, *, rows):
    ci = pl.program_id(2)

    @pl.when(ci == 0)
    def _():
        state_ref[...] = jnp.zeros(state_ref.shape, F32)

    c = RET_CHUNK
    for p in range(2):
        inner = inner_ref[p]
        qdec = qd_ref[p]
        kdec = kd_ref[p]
        cdec = cd_ref[p]
        state = state_ref[p]
        for n in range(rows // c):
            r0 = n * c
            qc = (q_ref[r0:r0 + c, p * RET_QK:(p + 1) * RET_QK] * (RET_QK ** -0.5)).astype(BF16)
            kf = k_ref[r0:r0 + c, p * RET_QK:(p + 1) * RET_QK]
            vc = v_ref[r0:r0 + c, p * HEAD_W:(p + 1) * HEAD_W].astype(BF16)
            att = _dot_nt(qc, kf.astype(BF16)) * inner
            o = _dot(att.astype(BF16), vc) + _dot(qc, state.astype(BF16)) * qdec
            state = cdec * state + _dot_tn((kf * kdec).astype(BF16), vc)
            mu = jnp.mean(o, axis=-1, keepdims=True)
            oc = o - mu
            var = jnp.mean(oc * oc, axis=-1, keepdims=True)
            gate = g_ref[r0:r0 + c, p * HEAD_W:(p + 1) * HEAD_W]
            y = oc * lax.rsqrt(var + EPS) * (gate * jax.nn.sigmoid(gate))
            o_ref[r0:r0 + c, p * HEAD_W:(p + 1) * HEAD_W] = y.astype(o_ref.dtype)
        state_ref[p] = state


def _retention_tables():
    h = N_HEADS
    c = RET_CHUNK
    log_g = jnp.log1p(-(2.0 ** (-RET_DECAY_BASE - jnp.arange(h, dtype=F32))))
    idx = jnp.arange(c, dtype=F32)
    rel = idx[:, None] - idx[None, :]
    inner = jnp.where(rel >= 0, jnp.exp(log_g[:, None, None] * jnp.maximum(rel, 0.0)), 0.0)
    qd = jnp.broadcast_to(jnp.exp(log_g[:, None] * (idx + 1.0))[:, :, None], (h, c, HEAD_W))
    kd = jnp.broadcast_to(jnp.exp(log_g[:, None] * (c - 1.0 - idx))[:, :, None], (h, c, RET_QK))
    cd = jnp.broadcast_to(jnp.exp(log_g * c)[:, None, None], (h, RET_QK, HEAD_W))
    return inner, qd, kd, cd


def _retention(proj3, *, rows=512):
    b, s, _ = proj3.shape
    rows = _pick(s, rows)
    inner, qd, kd, cd = _retention_tables()
    c = RET_CHUNK
    return pl.pallas_call(
        functools.partial(_retention_kernel, rows=rows),
        grid=(b, N_HEADS // 2, s // rows),
        in_specs=[
            pl.BlockSpec((2, c, c), lambda bi, hp, ci: (hp, 0, 0)),
            pl.BlockSpec((2, c, HEAD_W), lambda bi, hp, ci: (hp, 0, 0)),
            pl.BlockSpec((2, c, RET_QK), lambda bi, hp, ci: (hp, 0, 0)),
            pl.BlockSpec((2, RET_QK, HEAD_W), lambda bi, hp, ci: (hp, 0, 0)),
            pl.BlockSpec((None, rows, 128), lambda bi, hp, ci: (bi, ci, OFF_RT_Q + hp)),
            pl.BlockSpec((None, rows, 128), lambda bi, hp, ci: (bi, ci, OFF_RT_K + hp)),
            pl.BlockSpec((None, rows, 2 * HEAD_W), lambda bi, hp, ci: (bi, ci, OFF_RT_V // 2 + hp)),
            pl.BlockSpec((None, rows, 2 * HEAD_W), lambda bi, hp, ci: (bi, ci, OFF_RT_G // 2 + hp)),
        ],
        out_specs=pl.BlockSpec((None, rows, 2 * HEAD_W), lambda bi, hp, ci: (bi, ci, hp)),
        out_shape=jax.ShapeDtypeStruct((b, s, BRANCH_W), BF16),
        scratch_shapes=[pltpu.VMEM((2, RET_QK, HEAD_W), F32)],
        compiler_params=_params(("parallel", "parallel", "arbitrary")),
        name="retention",
    )(inner, qd, kd, cd, proj3, proj3, proj3, proj3)


def _split3(x):
    hi = x.astype(BF16)
    r1 = x - hi.astype(F32)
    mid = r1.astype(BF16)
    lo = (r1 - mid.astype(F32)).astype(BF16)
    return hi, mid, lo


HG_SUB = 16


def _hgrn2_kernel(lb_ref, gn_ref, q_ref, f_ref, i_ref, gt_ref, o_ref,
                  state_ref, b_sc, k_sc, q_sc, *, rows):
    ci = pl.program_id(2)
    c, sc = HG_CHUNK, HG_SUB

    @pl.when(ci == 0)
    def _():
        state_ref[...] = jnp.zeros(state_ref.shape, F32)

    lb = lb_ref[...]
    gn = gn_ref[...]
    fr = f_ref[...]
    log_f = jnp.log(lb + (1.0 - lb) * jax.nn.sigmoid(fr))
    k_sc[...] = (1.0 - lb) * jax.nn.sigmoid(-fr)
    qraw = q_ref[...]
    q_sc[...] = qraw * jax.nn.sigmoid(qraw)
    ri = lax.broadcasted_iota(jnp.int32, (rows, rows), 0)
    cj = lax.broadcasted_iota(jnp.int32, (rows, rows), 1)
    tri = jnp.where(jnp.logical_and(ri >= cj, (ri // c) == (cj // c)), 1.0, 0.0).astype(BF16)
    hi, mid, lo = _split3(log_f)
    b_sc[...] = _dot(tri, hi) + _dot(tri, mid) + _dot(tri, lo)
    hs = sc // 2
    row_half = lax.broadcasted_iota(jnp.int32, (hs, HEAD_W), 0)
    row_chunk = lax.broadcasted_iota(jnp.int32, (c, HEAD_W), 0)

    for n in range(rows // c):
        r0 = n * c
        parts = []
        for sub in range(c // sc):
            s0 = r0 + sub * sc
            b_i = b_sc[s0:s0 + sc, :]
            q_i = q_sc[s0:s0 + sc, :]
            halves = [(b_i[:hs], q_i[:hs]), (b_i[hs:], q_i[hs:])]
            acc_h = [jnp.zeros((hs, HEAD_W), F32), jnp.zeros((hs, HEAD_W), F32)]
            for jj in range(sc):
                j = s0 + jj
                b_j, k_j, v_j = b_sc[j:j + 1, :], k_sc[j:j + 1, :], i_ref[j:j + 1, :]
                for hh, (b_h, q_h) in enumerate(halves):
                    first = jj - hh * hs
                    if first >= hs:
                        continue
                    d = b_h - b_j
                    if first > 0:
                        d = jnp.where(row_half >= first, d, NEG_INF)
                    a = jnp.sum(q_h * k_j * jnp.exp(d), axis=-1, keepdims=True)
                    acc_h[hh] = acc_h[hh] + a * v_j
            parts.append(jnp.concatenate(acc_h, axis=0))
        o = jnp.concatenate(parts, axis=0)
        bc = b_sc[r0:r0 + c, :]
        qc = q_sc[r0:r0 + c, :]
        kc = k_sc[r0:r0 + c, :]
        vc = i_ref[r0:r0 + c, :].astype(BF16)
        q_segs, k_segs = [], []
        for sub in range(1, c // sc):
            anchor = b_sc[r0 + sub * sc:r0 + sub * sc + 1, :]
            in_sub = jnp.logical_and(row_chunk >= sub * sc, row_chunk < (sub + 1) * sc)
            q_segs.append(qc * jnp.exp(jnp.where(in_sub, bc - anchor, NEG_INF)))
            k_segs.append(kc * jnp.exp(jnp.where(row_chunk < sub * sc, anchor - bc, NEG_INF)))
        att = _dot_nt(jnp.concatenate(q_segs, axis=1).astype(BF16),
                      jnp.concatenate(k_segs, axis=1).astype(BF16))
        state_t = state_ref[...]
        o = (o + _dot(att.astype(BF16), vc)
             + _dot_nt((qc * jnp.exp(bc)).astype(BF16), state_t.astype(BF16)))
        blast = b_sc[r0 + c - 1:r0 + c, :]
        kd = (kc * jnp.exp(blast - bc)).astype(BF16)
        state_ref[...] = jnp.exp(blast) * state_t + _dot_tn(vc, kd)
        ms = jnp.mean(o * o, axis=-1, keepdims=True)
        gate = gt_ref[r0:r0 + c, :]
        y = o * lax.rsqrt(ms + EPS) * gn * (gate * jax.nn.sigmoid(gate))
        o_ref[r0:r0 + c, :] = y.astype(o_ref.dtype)


def _hgrn2(proj3, lb, gn, *, rows=256):
    b, s, _ = proj3.shape
    rows = _pick(s, rows)
    blk = lambda off: pl.BlockSpec((None, rows, HEAD_W), lambda bi, h, ci: (bi, ci, off + h))
    return pl.pallas_call(
        functools.partial(_hgrn2_kernel, rows=rows),
        grid=(b, N_HEADS, s // rows),
        in_specs=[
            pl.BlockSpec((None, 1, HEAD_W), lambda bi, h, ci: (h, 0, 0)),
            pl.BlockSpec((1, HEAD_W), lambda bi, h, ci: (0, 0)),
            blk(OFF_HG_Q), blk(OFF_HG_F), blk(OFF_HG_I), blk(OFF_HG_G),
        ],
        out_specs=pl.BlockSpec((None, rows, HEAD_W), lambda bi, h, ci: (bi, ci, h)),
        out_shape=jax.ShapeDtypeStruct((b, s, BRANCH_W), BF16),
        scratch_shapes=[pltpu.VMEM((HEAD_W, HEAD_W), F32)] + [pltpu.VMEM((rows, HEAD_W), F32)] * 3,
        compiler_params=_params(("parallel", "parallel", "arbitrary")),
        name="hgrn2",
    )(lb.reshape(N_HEADS, 1, HEAD_W), gn.reshape(1, HEAD_W), proj3, proj3, proj3, proj3)


def _dil_kernel(slope_ref, *refs, tile):
    n_g = len(DIL_RATES)
    groups = [refs[5 * g:5 * g + 5] for g in range(n_g)]
    o_ref, o_sc, l_sc = refs[5 * n_g:]
    h = pl.program_id(1)
    blk = pl.program_id(2)
    t = DIL_BLOCK
    d_cur = (lax.broadcasted_iota(jnp.int32, (t, t), 0)
             - lax.broadcasted_iota(jnp.int32, (t, t), 1)).astype(F32)
    d_prev = d_cur + float(t)

    for g, (q_ref, k_ref, v_ref, kh_ref, vh_ref) in enumerate(groups):
        rate = DIL_RATES[g]
        halo = t * rate
        slope = slope_ref[g * N_HEADS + h] * float(rate)
        bias_cur = jnp.where(d_cur >= 0, -slope * d_cur, NEG_INF)
        bias_prev = jnp.where(d_prev <= float(t), -slope * d_prev, NEG_INF)
        bias_both = jnp.concatenate([bias_prev, bias_cur], axis=1)
        bias_first = jnp.concatenate([jnp.full((t, t), NEG_INF, F32), bias_cur], axis=1)

        def rows(start, rate=rate):
            return pl.ds(start, t) if rate == 1 else pl.ds(start, t, stride=rate)

        def units(ug, carry, g=g, rate=rate, halo=halo, q_ref=q_ref, k_ref=k_ref, v_ref=v_ref,
                  kh_ref=kh_ref, vh_ref=vh_ref, bias_both=bias_both, bias_first=bias_first, rows=rows):
            starts, qs, ks, vs, biases = [], [], [], [], []
            for uu in range(DIL_UNROLL):
                u = ug * DIL_UNROLL + uu
                r = u % rate
                mb = u // rate
                start = mb * halo + r
                starts.append(start)
                qs.append((q_ref[rows(start), :] * (HEAD_W ** -0.5)).astype(BF16))
                kp = kh_ref[rows(r), :]
                vp = vh_ref[rows(r), :]
                prev_ok = blk > 0
                if tile > halo:
                    pstart = jnp.maximum(start - halo, r)
                    in_tile = mb > 0
                    kp = jnp.where(in_tile, k_ref[rows(pstart), :], kp)
                    vp = jnp.where(in_tile, v_ref[rows(pstart), :], vp)
                    prev_ok = jnp.logical_or(prev_ok, in_tile)
                ks.append(jnp.concatenate([kp, k_ref[rows(start), :]], axis=0).astype(BF16))
                vs.append(jnp.concatenate([vp, v_ref[rows(start), :]], axis=0).astype(BF16))
                biases.append(jnp.where(prev_ok, bias_both, bias_first))
            ss = [_dot_nt(q, k) + bias for q, k, bias in zip(qs, ks, biases)]
            ms = [jnp.max(s, axis=-1, keepdims=True) for s in ss]
            ps = [jnp.exp(s - m) for s, m in zip(ss, ms)]
            ls = [jnp.sum(p, axis=-1, keepdims=True) for p in ps]
            accs = [_dot(p.astype(BF16), v) for p, v in zip(ps, vs)]
            for start, m, l, acc in zip(starts, ms, ls, accs):
                o = acc / l
                lse = jnp.broadcast_to(m + jnp.log(l), (t, HEAD_W))
                if g > 0:
                    lse_p = l_sc[rows(start), :]
                    lse_n = jnp.maximum(lse, lse_p)
                    lse_n = lse_n + jnp.log(jnp.exp(lse - lse_n) + jnp.exp(lse_p - lse_n))
                    o = o_sc[rows(start), :] * jnp.exp(lse_p - lse_n) + o * jnp.exp(lse - lse_n)
                    lse = lse_n
                o_sc[rows(start), :] = o
                if g < n_g - 1:
                    l_sc[rows(start), :] = lse
            return carry

        lax.fori_loop(0, tile // t // DIL_UNROLL, units, 0)

    o_ref[...] = o_sc[...].astype(o_ref.dtype)


def _dilated_attention(proj3, slopes4):
    b, s, _ = proj3.shape
    t = DIL_BLOCK
    tile = t * max(DIL_RATES)
    assert s % tile == 0
    in_specs, args = [], []
    for g, rate in enumerate(DIL_RATES):
        halo = t * rate
        per = tile // halo

        def cur(off, g=g):
            return pl.BlockSpec((None, tile, HEAD_W),
                                lambda bi, h, blk, sl: (bi, blk, off + g * N_HEADS + h))

        def front(off, g=g, halo=halo, per=per):
            return pl.BlockSpec((None, halo, HEAD_W),
                                lambda bi, h, blk, sl: (bi, jnp.maximum(blk * per - 1, 0), off + g * N_HEADS + h))

        in_specs += [cur(OFF_DL_Q), cur(OFF_DL_K), cur(OFF_DL_V), front(OFF_DL_K), front(OFF_DL_V)]
        args += [proj3] * 5
    grid_spec = pltpu.PrefetchScalarGridSpec(
        num_scalar_prefetch=1,
        grid=(b, N_HEADS, s // tile),
        in_specs=in_specs,
        out_specs=pl.BlockSpec((None, tile, HEAD_W), lambda bi, h, blk, sl: (bi, blk, h)),
        scratch_shapes=[pltpu.VMEM((tile, HEAD_W), F32), pltpu.VMEM((tile, HEAD_W), F32)],
    )
    return pl.pallas_call(
        functools.partial(_dil_kernel, tile=tile),
        grid_spec=grid_spec,
        out_shape=jax.ShapeDtypeStruct((b, s, BRANCH_W), BF16),
        compiler_params=_params(("parallel", "parallel", "parallel")),
        name="dilated",
    )(slopes4[:, 1:].T.reshape(-1), *args)


def _merge_kernel(ya_ref, yr_ref, yc_ref, yd_ref, w_ref, g0_ref, g1_ref, g2_ref, g3_ref, o_ref, wb_ref):
    @pl.when(pl.program_id(1) == 0)
    def _():
        _cast_weights(w_ref, wb_ref)

    acc = None
    for n, (y_ref, g_ref) in enumerate(((ya_ref, g0_ref), (yr_ref, g1_ref), (yc_ref, g2_ref), (yd_ref, g3_ref))):
        t = jax.nn.sigmoid(g_ref[...]) * _dot(y_ref[...], wb_ref[n])
        acc = t if acc is None else acc + t
    o_ref[...] = acc.astype(o_ref.dtype)


def _merge(ys, w_branch, layer, proj, d_model, *, tm=512, tn=1024):
    m = proj.shape[0]
    tm, tn = _pick(m, tm), _pick(d_model, tn)
    gcols = d_model // tn
    goff = OFF_GATES * 128 // tn
    y_spec = pl.BlockSpec((tm, BRANCH_W), lambda j, i: (i, 0))

    def gate_spec(n):
        return pl.BlockSpec((tm, tn), lambda j, i: (i, goff + n * gcols + j))

    return pl.pallas_call(
        _merge_kernel,
        grid=(d_model // tn, m // tm),
        in_specs=[y_spec, y_spec, y_spec, y_spec,
                  pl.BlockSpec((None, N_BRANCH, BRANCH_W, tn), lambda j, i: (layer, 0, 0, j),
                               pipeline_mode=pl.Buffered(1)),
                  gate_spec(0), gate_spec(1), gate_spec(2), gate_spec(3)],
        out_specs=pl.BlockSpec((tm, tn), lambda j, i: (i, j)),
        out_shape=jax.ShapeDtypeStruct((m, d_model), BF16),
        scratch_shapes=[pltpu.VMEM((N_BRANCH, BRANCH_W, tn), BF16)],
        compiler_params=_params(("parallel", "arbitrary")),
        name="merge",
    )(*ys, w_branch, proj, proj, proj, proj)


FFN_SUB_ROWS = 512


def _ffn_up_kernel(x_ref, w_hbm, cw_ref, cb_ref, o_ref, sa_ref, sg_ref, wab_ref, wgb_ref, sems,
                   carry_ref, *, tiles_per_seq, layer, n_blocks):
    i = pl.program_id(1)
    tn = o_ref.shape[1]
    _stage_weights(w_hbm, layer, 0, n_blocks, sa_ref, wab_ref, sems.at[0])
    _stage_weights(w_hbm, layer, n_blocks * tn, n_blocks, sg_ref, wgb_ref, sems.at[1])

    @pl.when(i % tiles_per_seq == 0)
    def _():
        carry_ref[...] = jnp.zeros(carry_ref.shape, F32)

    tm, tn = o_ref.shape
    sub = min(tm, FFN_SUB_ROWS)
    cw = cw_ref[...]
    cb = cb_ref[...]
    row = lax.broadcasted_iota(jnp.int32, (sub, tn), 0)
    prev = carry_ref[...]
    for r0 in range(0, tm, sub):
        x = x_ref[r0:r0 + sub, :]
        a = _dot(x, wab_ref[...])
        gl = _dot(x, wgb_ref[...])
        a1 = jnp.where(row == 0, prev[7:8, :], pltpu.roll(a, 1, 0))
        a2 = jnp.where(row == 0, prev[6:7, :],
                       jnp.where(row == 1, prev[7:8, :], pltpu.roll(a, 2, 0)))
        conv = cb + cw[0:1, :] * a2 + cw[1:2, :] * a1 + cw[2:3, :] * a
        gelu = 0.5 * conv * (1.0 + lax.erf(conv * (2.0 ** -0.5)))
        o_ref[r0:r0 + sub, :] = (gelu * gl).astype(o_ref.dtype)
        prev = a[sub - 8:sub, :]
    carry_ref[...] = prev


def _ffn_up(x, w_up, layer, conv_w, conv_b, seq, *, tm=2048, tn=256):
    m, kdim = x.shape
    d_ff = w_up.shape[-1] // 2
    tm, tn = _pick(seq, tm), _pick(d_ff, tn)
    nj = d_ff // tn
    return pl.pallas_call(
        functools.partial(_ffn_up_kernel, tiles_per_seq=seq // tm, layer=layer, n_blocks=nj),
        grid=(nj, m // tm),
        in_specs=[
            pl.BlockSpec((tm, kdim), lambda j, i: (i, 0)),
            pl.BlockSpec(memory_space=pl.ANY),
            pl.BlockSpec((CONV_W, tn), lambda j, i: (0, j)),
            pl.BlockSpec((1, tn), lambda j, i: (0, j)),
        ],
        out_specs=pl.BlockSpec((tm, tn), lambda j, i: (i, j)),
        out_shape=jax.ShapeDtypeStruct((m, d_ff), BF16),
        scratch_shapes=[pltpu.VMEM((kdim, tn), F32), pltpu.VMEM((kdim, tn), F32),
                        pltpu.VMEM((kdim, tn), BF16), pltpu.VMEM((kdim, tn), BF16),
                        pltpu.SemaphoreType.DMA((2,)), pltpu.VMEM((8, tn), F32)],
        compiler_params=_params(("arbitrary", "arbitrary")),
        name="ffn_up",
    )(x, w_up, conv_w, conv_b.reshape(1, d_ff))


def _alibi_slopes():
    n = 4 * N_HEADS
    slopes = 2.0 ** (-8.0 * jnp.arange(1, n + 1, dtype=F32) / n)
    return slopes.reshape(N_HEADS, 4)


def kernel(x, w_in, diff_lambda, diff_norm_g, hgrn_lb_logits, hgrn_norm_g, w_branch, w_o,
           ln1_g, ln1_b, w_up, conv_w, conv_b, w_down, ln2_g, ln2_b):
    b, s, d = x.shape
    depth = w_in.shape[0]
    m = b * s
    alpha = (2 * depth) ** 0.25
    slopes4 = _alibi_slopes()
    p = jax.nn.softmax(hgrn_lb_logits.astype(F32), axis=0)
    lower_bounds = jnp.cumsum(p, axis=0) - p[0]

    w_down_b = w_down.astype(BF16)
    h = x.reshape(m, d)
    hb = h.astype(BF16)
    for l in range(depth):
        lam_init = 0.8 - 0.6 * math.exp(-0.3 * l)
        proj = _matmul(hb, w_in, l, out_dtype=F32, tm=1024, name="in_proj")
        proj3 = proj.reshape(b, s, -1)
        ya = _diff_attention(proj3, slopes4[:, 0], diff_lambda[l], diff_norm_g[l], lam_init)
        yr = _retention(proj3)
        yc = _hgrn2(proj3, lower_bounds[l], hgrn_norm_g[l])
        yd = _dilated_attention(proj3, slopes4)
        ys = [y.reshape(m, BRANCH_W) for y in (ya, yr, yc, yd)]
        merged = _merge(ys, w_branch, l, proj, d)
        z = _matmul(merged, w_o, l, out_dtype=F32, res=h, alpha=alpha, name="out_proj")
        h, hb = _layernorm(z, ln1_g[l], ln1_b[l])
        act = _ffn_up(hb, w_up, l, conv_w[l], conv_b[l], s)
        z = _matmul(act, w_down_b, l, out_dtype=F32, tn=512, res=h, alpha=alpha, name="ffn_down")
        h, hb = _layernorm(z, ln2_g[l], ln2_b[l])
    return h.reshape(b, s, d)
```

```python
import functools
import math

import jax
import jax.numpy as jnp
from jax import lax
from jax.experimental import pallas as pl
from jax.experimental.pallas import tpu as pltpu

F32 = jnp.float32
BF16 = jnp.bfloat16

N_HEADS = 8
HEAD_W = 128
BRANCH_W = N_HEADS * HEAD_W
N_BRANCH = 4
DA_QK = 64
RET_QK = 64
RET_CHUNK = 128
RET_DECAY_BASE = 5.0
HG_CHUNK = 64
DIL_WINDOWS = (128, 512, 2048)
DIL_RATES = (1, 4, 16)
DIL_KEYS = 129
DIL_BLOCK = 128
DIL_UNROLL = 4
CONV_W = 3
EPS = 1e-5
NEG_INF = float("-inf")

_C = BRANCH_W // 128
OFF_DA_Q, OFF_DA_K, OFF_DA_V = 0, _C, 2 * _C
OFF_RT_Q, OFF_RT_K = 3 * _C, 3 * _C + _C // 2
OFF_RT_V, OFF_RT_G = 4 * _C, 5 * _C
OFF_HG_Q, OFF_HG_F, OFF_HG_I, OFF_HG_G = 6 * _C, 7 * _C, 8 * _C, 9 * _C
OFF_DL_Q, OFF_DL_K, OFF_DL_V = 10 * _C, 13 * _C, 16 * _C
OFF_GATES = 19 * _C

VMEM_LIMIT = 56 * 1024 * 1024


def _params(sem):
    return pltpu.CompilerParams(dimension_semantics=sem, vmem_limit_bytes=VMEM_LIMIT)


def _dot(a, b):
    return jnp.dot(a, b, preferred_element_type=F32)


def _dot_nt(a, b):
    return lax.dot_general(a, b, (((1,), (1,)), ((), ())), preferred_element_type=F32)


def _dot_tn(a, b):
    return lax.dot_general(a, b, (((0,), (0,)), ((), ())), preferred_element_type=F32)


def _pick(dim, pref):
    if dim <= pref:
        return dim
    t = pref
    while t >= 128:
        if dim % t == 0 and t % 128 == 0:
            return t
        t -= 128
    return dim


CAST_ROWS = 512


def _cast_weights(w_ref, wb_ref):
    rows = w_ref.shape[-2]
    step = min(rows, CAST_ROWS)
    for r0 in range(0, rows, step):
        r1 = min(rows, r0 + step)
        wb_ref[..., r0:r1, :] = w_ref[..., r0:r1, :].astype(BF16)


def _stage_weights(w_hbm, layer, first_col, n_blocks, stage_ref, wb_ref, sem):
    j = pl.program_id(0)
    tn = stage_ref.shape[-1]

    def copy(jj):
        col = pl.multiple_of(first_col + jj * tn, tn)
        return pltpu.make_async_copy(w_hbm.at[layer, :, pl.ds(col, tn)], stage_ref, sem)

    @pl.when(pl.program_id(1) == 0)
    def _():
        @pl.when(j == 0)
        def _():
            copy(0).start()

        copy(j).wait()
        _cast_weights(stage_ref, wb_ref)

        @pl.when(j + 1 < n_blocks)
        def _():
            copy(j + 1).start()


def _mm_kernel(*refs, alpha, has_res, layer, first_col, n_blocks):
    if has_res:
        x_ref, w_ref, r_ref, o_ref, *scratch = refs
    else:
        x_ref, w_ref, o_ref, *scratch = refs

    if scratch:
        stage_ref, wb_ref, sem = scratch
        _stage_weights(w_ref, layer, first_col, n_blocks, stage_ref, wb_ref, sem)
    else:
        wb_ref = w_ref

    acc = _dot(x_ref[...], wb_ref[...])
    if has_res:
        acc = alpha * r_ref[...] + acc
    o_ref[...] = acc.astype(o_ref.dtype)


def _matmul(x, w, layer, *, out_dtype, cols=None, tm=512, tn=1024, res=None, alpha=1.0, name="mm"):
    m, kdim = x.shape
    first_col, n = (0, w.shape[-1]) if cols is None else cols
    tm, tn = _pick(m, tm), _pick(n, tn)
    assert first_col % tn == 0
    if w.dtype == BF16:
        w_spec = pl.BlockSpec((None, kdim, tn), lambda j, i: (layer, 0, first_col // tn + j),
                              pipeline_mode=pl.Buffered(1))
        scratch = []
    else:
        w_spec = pl.BlockSpec(memory_space=pl.ANY)
        scratch = [pltpu.VMEM((kdim, tn), F32), pltpu.VMEM((kdim, tn), BF16), pltpu.SemaphoreType.DMA(())]
    in_specs = [pl.BlockSpec((tm, kdim), lambda j, i: (i, 0)), w_spec]
    args = [x, w]
    if res is not None:
        in_specs.append(pl.BlockSpec((tm, tn), lambda j, i: (i, j)))
        args.append(res)
    return pl.pallas_call(
        functools.partial(_mm_kernel, alpha=alpha, has_res=res is not None, layer=layer,
                          first_col=first_col, n_blocks=n // tn),
        grid=(n // tn, m // tm),
        in_specs=in_specs,
        out_specs=pl.BlockSpec((tm, tn), lambda j, i: (i, j)),
        out_shape=jax.ShapeDtypeStruct((m, n), out_dtype),
        scratch_shapes=scratch,
        compiler_params=_params(("arbitrary", "arbitrary")),
        name=name,
    )(*args)


def _ln_kernel(x_ref, g_ref, b_ref, o_ref, ob_ref):
    x = x_ref[...]
    mu = jnp.mean(x, axis=-1, keepdims=True)
    xc = x - mu
    var = jnp.mean(xc * xc, axis=-1, keepdims=True)
    y = xc * lax.rsqrt(var + EPS) * g_ref[...] + b_ref[...]
    o_ref[...] = y
    ob_ref[...] = y.astype(BF16)


def _layernorm(x, g, b, *, tm=256):
    m, d = x.shape
    tm = _pick(m, tm)
    return pl.pallas_call(
        _ln_kernel,
        grid=(m // tm,),
        in_specs=[
            pl.BlockSpec((tm, d), lambda i: (i, 0)),
            pl.BlockSpec((1, d), lambda i: (0, 0)),
            pl.BlockSpec((1, d), lambda i: (0, 0)),
        ],
        out_specs=[pl.BlockSpec((tm, d), lambda i: (i, 0)), pl.BlockSpec((tm, d), lambda i: (i, 0))],
        out_shape=[jax.ShapeDtypeStruct((m, d), F32), jax.ShapeDtypeStruct((m, d), BF16)],
        compiler_params=_params(("parallel",)),
        name="layernorm",
    )(x, g.reshape(1, d), b.reshape(1, d))


LOG2E = 1.4426950408889634


def _split3_f32(x):
    hi = x.astype(BF16).astype(F32)
    mid = (x - hi).astype(BF16).astype(F32)
    lo = (x - hi - mid).astype(BF16).astype(F32)
    return hi, mid, lo


def _lane_fill(lane, first, vals, other):
    out = other
    for n, val in enumerate(vals):
        out = jnp.where(lane == first + n, val, out)
    return out


def _diff_attn_kernel(slope_ref, lam_ref, g_ref, q_ref, k_ref, v_ref, o_ref,
                      m1_ref, a1_ref, m2_ref, a2_ref, kaug_sc, mask_sc, sa_sc, sb_sc, *, t, lam_init):
    h = pl.program_id(1)
    qi = pl.program_id(2)
    slope2 = slope_ref[h] * LOG2E
    lane = lax.broadcasted_iota(jnp.int32, (t, HEAD_W), 1)
    aug_first = (DA_QK, 0)
    owns = (lane < DA_QK, lane >= DA_QK)
    accs = ((m1_ref, a1_ref), (m2_ref, a2_ref))

    @pl.when(qi == 0)
    def _():
        j = lax.broadcasted_iota(jnp.int32, (t, HEAD_W), 0).astype(F32)
        parts = _split3_f32(slope2 * j)
        for idx in range(2):
            kaug_sc[idx] = _lane_fill(lane, aug_first[idx], list(parts) + [1.0, 1.0, 1.0], 0.0)
        d = (lax.broadcasted_iota(jnp.int32, (t, t), 0) - lax.broadcasted_iota(jnp.int32, (t, t), 1))
        mask_sc[...] = jnp.where(d >= 0, 0.0, NEG_INF)

    for m_ref, a_ref in accs:
        m_ref[...] = jnp.full(m_ref.shape, NEG_INF, F32)
        a_ref[...] = jnp.zeros(a_ref.shape, F32)

    q = q_ref[...] * (DA_QK ** -0.5 * LOG2E)
    ones_col = jnp.where(lane == 0, 1.0, 0.0).astype(BF16)

    def scores(ki, s_ref):
        r0 = pl.multiple_of(ki * t, t)
        block_c = jnp.full((1, HEAD_W), -slope2 * ((qi - ki) * t).astype(F32), F32)
        c_parts = _split3_f32(block_c)
        k = k_ref[pl.ds(r0, t), :]
        for idx in range(2):
            q_aug = _lane_fill(lane[0:1], aug_first[idx], [1.0, 1.0, 1.0] + list(c_parts), 0.0)
            qq = jnp.where(owns[idx], q, q_aug).astype(BF16)
            kk = jnp.where(owns[idx], k, kaug_sc[idx]).astype(BF16)
            s_ref[idx] = _dot_nt(qq, kk)

    def update(ki, s_ref, masked):
        r0 = pl.multiple_of(ki * t, t)
        v_aug = jnp.concatenate([v_ref[pl.ds(r0, t), :].astype(BF16), ones_col], axis=1)
        for idx, (m_ref, a_ref) in enumerate(accs):
            s = s_ref[idx]
            if masked:
                s = s + mask_sc[...]
            m_prev = m_ref[...]
            m_new = jnp.maximum(m_prev, jnp.max(s, axis=-1, keepdims=True))
            corr = jnp.exp2(m_prev - m_new)
            p = jnp.exp2(s - jnp.tile(m_new, (1, t // HEAD_W)))
            a_ref[...] = jnp.tile(corr, (1, 2)) * a_ref[...] + _dot(p.astype(BF16), v_aug)
            m_ref[...] = m_new

    scores(0, sa_sc)

    def body(pair, carry):
        k0 = 2 * pair
        scores(k0 + 1, sb_sc)
        update(k0, sa_sc, False)
        scores(k0 + 2, sa_sc)
        update(k0 + 1, sb_sc, False)
        return carry

    lax.fori_loop(0, qi // 2, body, 0)

    @pl.when(qi % 2 == 0)
    def _():
        update(qi, sa_sc, True)

    @pl.when(qi % 2 == 1)
    def _():
        scores(qi, sb_sc)
        update(qi - 1, sa_sc, False)
        update(qi, sb_sc, True)

    lv = lam_ref[...]
    lam = (jnp.exp(jnp.sum(lv[0:1] * lv[1:2], axis=-1, keepdims=True))
           - jnp.exp(jnp.sum(lv[2:3] * lv[3:4], axis=-1, keepdims=True)) + lam_init)
    o1 = a1_ref[:, :HEAD_W] / a1_ref[:, HEAD_W:HEAD_W + 1]
    o2 = a2_ref[:, :HEAD_W] / a2_ref[:, HEAD_W:HEAD_W + 1]
    o = o1 - lam * o2
    ms = jnp.mean(o * o, axis=-1, keepdims=True)
    o = o * lax.rsqrt(ms + EPS) * g_ref[...] * (1.0 - lam_init)
    o_ref[...] = o.astype(o_ref.dtype)


def _diff_attention(proj3, slopes, lam_vec, g, lam_init, *, t=512):
    b, s, _ = proj3.shape
    t = _pick(s, t)
    kern = functools.partial(_diff_attn_kernel, t=t, lam_init=lam_init)
    grid_spec = pltpu.PrefetchScalarGridSpec(
        num_scalar_prefetch=1,
        grid=(b, N_HEADS, s // t),
        in_specs=[
            pl.BlockSpec(lam_vec.shape, lambda bi, h, qi, sl: (0, 0)),
            pl.BlockSpec((1, HEAD_W), lambda bi, h, qi, sl: (0, 0)),
            pl.BlockSpec((None, t, HEAD_W), lambda bi, h, qi, sl: (bi, qi, OFF_DA_Q + h)),
            pl.BlockSpec((None, s, HEAD_W), lambda bi, h, qi, sl: (bi, 0, OFF_DA_K + h)),
            pl.BlockSpec((None, s, HEAD_W), lambda bi, h, qi, sl: (bi, 0, OFF_DA_V + h)),
        ],
        out_specs=pl.BlockSpec((None, t, HEAD_W), lambda bi, h, qi, sl: (bi, qi, h)),
        scratch_shapes=[pltpu.VMEM((t, HEAD_W), F32), pltpu.VMEM((t, 2 * HEAD_W), F32)] * 2
        + [pltpu.VMEM((2, t, HEAD_W), F32), pltpu.VMEM((t, t), F32)]
        + [pltpu.VMEM((2, t, t), F32)] * 2,
    )
    return pl.pallas_call(
        kern,
        grid_spec=grid_spec,
        out_shape=jax.ShapeDtypeStruct((b, s, BRANCH_W), BF16),
        compiler_params=_params(("parallel", "parallel", "arbitrary")),
        name="diff_attention",
    )(slopes, lam_vec, g.reshape(1, HEAD_W), proj3, proj3, proj3)


def _retention_kernel(inner_ref, qd_ref, kd_ref, cd_ref, q_ref, k_ref, v_ref, g_ref, o_ref,
                      state_ref, *, rows):
    ci = pl.program_id(2)

    @pl.when(ci == 0)
    def _():
        state_ref[...] = jnp.zeros(state_ref.shape, F32)

    c = RET_CHUNK
    for p in range(2):
        inner = inner_ref[p]
        qdec = qd_ref[p]
        kdec = kd_ref[p]
        cdec = cd_ref[p]
        state = state_ref[p]
        for n in range(rows // c):
            r0 = n * c
            qc = (q_ref[r0:r0 + c, p * RET_QK:(p + 1) * RET_QK] * (RET_QK ** -0.5)).astype(BF16)
            kf = k_ref[r0:r0 + c, p * RET_QK:(p + 1) * RET_QK]
            vc = v_ref[r0:r0 + c, p * HEAD_W:(p + 1) * HEAD_W].astype(BF16)
            att = _dot_nt(qc, kf.astype(BF16)) * inner
            o = _dot(att.astype(BF16), vc) + _dot(qc, state.astype(BF16)) * qdec
            state = cdec * state + _dot_tn((kf * kdec).astype(BF16), vc)
            mu = jnp.mean(o, axis=-1, keepdims=True)
            oc = o - mu
            var = jnp.mean(oc * oc, axis=-1, keepdims=True)
            gate = g_ref[r0:r0 + c, p * HEAD_W:(p + 1) * HEAD_W]
            y = oc * lax.rsqrt(var + EPS) * (gate * jax.nn.sigmoid(gate))
            o_ref[r0:r0 + c, p * HEAD_W:(p + 1) * HEAD_W] = y.astype(o_ref.dtype)
        state_ref[p] = state


def _retention_tables():
    h = N_HEADS
    c = RET_CHUNK
    log_g = jnp.log1p(-(2.0 ** (-RET_DECAY_BASE - jnp.arange(h, dtype=F32))))
    idx = jnp.arange(c, dtype=F32)
    rel = idx[:, None] - idx[None, :]
    inner = jnp.where(rel >= 0, jnp.exp(log_g[:, None, None] * jnp.maximum(rel, 0.0)), 0.0)
    qd = jnp.broadcast_to(jnp.exp(log_g[:, None] * (idx + 1.0))[:, :, None], (h, c, HEAD_W))
    kd = jnp.broadcast_to(jnp.exp(log_g[:, None] * (c - 1.0 - idx))[:, :, None], (h, c, RET_QK))
    cd = jnp.broadcast_to(jnp.exp(log_g * c)[:, None, None], (h, RET_QK, HEAD_W))
    return inner, qd, kd, cd


def _retention(proj3, *, rows=512):
    b, s, _ = proj3.shape
    rows = _pick(s, rows)
    inner, qd, kd, cd = _retention_tables()
    c = RET_CHUNK
    return pl.pallas_call(
        functools.partial(_retention_kernel, rows=rows),
        grid=(b, N_HEADS // 2, s // rows),
        in_specs=[
            pl.BlockSpec((2, c, c), lambda bi, hp, ci: (hp, 0, 0)),
            pl.BlockSpec((2, c, HEAD_W), lambda bi, hp, ci: (hp, 0, 0)),
            pl.BlockSpec((2, c, RET_QK), lambda bi, hp, ci: (hp, 0, 0)),
            pl.BlockSpec((2, RET_QK, HEAD_W), lambda bi, hp, ci: (hp, 0, 0)),
            pl.BlockSpec((None, rows, 128), lambda bi, hp, ci: (bi, ci, OFF_RT_Q + hp)),
            pl.BlockSpec((None, rows, 128), lambda bi, hp, ci: (bi, ci, OFF_RT_K + hp)),
            pl.BlockSpec((None, rows, 2 * HEAD_W), lambda bi, hp, ci: (bi, ci, OFF_RT_V // 2 + hp)),
            pl.BlockSpec((None, rows, 2 * HEAD_W), lambda bi, hp, ci: (bi, ci, OFF_RT_G // 2 + hp)),
        ],
        out_specs=pl.BlockSpec((None, rows, 2 * HEAD_W), lambda bi, hp, ci: (bi, ci, hp)),
        out_shape=jax.ShapeDtypeStruct((b, s, BRANCH_W), BF16),
        scratch_shapes=[pltpu.VMEM((2, RET_QK, HEAD_W), F32)],
        compiler_params=_params(("parallel", "parallel", "arbitrary")),
        name="retention",
    )(inner, qd, kd, cd, proj3, proj3, proj3, proj3)


def _split3(x):
    hi = x.astype(BF16)
    r1 = x - hi.astype(F32)
    mid = r1.astype(BF16)
    lo = (r1 - mid.astype(F32)).astype(BF16)
    return hi, mid, lo


HG_SUB = 16


def _hgrn2_kernel(lb_ref, gn_ref, q_ref, f_ref, i_ref, gt_ref, o_ref,
                  state_ref, b_sc, k_sc, q_sc, *, rows):
    ci = pl.program_id(2)
    c, sc = HG_CHUNK, HG_SUB

    @pl.when(ci == 0)
    def _():
        state_ref[...] = jnp.zeros(state_ref.shape, F32)

    lb = lb_ref[...]
    gn = gn_ref[...]
    fr = f_ref[...]
    log_f = jnp.log(lb + (1.0 - lb) * jax.nn.sigmoid(fr))
    k_sc[...] = (1.0 - lb) * jax.nn.sigmoid(-fr)
    qraw = q_ref[...]
    q_sc[...] = qraw * jax.nn.sigmoid(qraw)
    ri = lax.broadcasted_iota(jnp.int32, (rows, rows), 0)
    cj = lax.broadcasted_iota(jnp.int32, (rows, rows), 1)
    tri = jnp.where(jnp.logical_and(ri >= cj, (ri // c) == (cj // c)), 1.0, 0.0).astype(BF16)
    hi, mid, lo = _split3(log_f)
    b_sc[...] = _dot(tri, hi) + _dot(tri, mid) + _dot(tri, lo)
    hs = sc // 2
    row_half = lax.broadcasted_iota(jnp.int32, (hs, HEAD_W), 0)
    row_chunk = lax.broadcasted_iota(jnp.int32, (c, HEAD_W), 0)

    for n in range(rows // c):
        r0 = n * c
        parts = []
        for sub in range(c // sc):
            s0 = r0 + sub * sc
            b_i = b_sc[s0:s0 + sc, :]
            q_i = q_sc[s0:s0 + sc, :]
            halves = [(b_i[:hs], q_i[:hs]), (b_i[hs:], q_i[hs:])]
            acc_h = [jnp.zeros((hs, HEAD_W), F32), jnp.zeros((hs, HEAD_W), F32)]
            for jj in range(sc):
                j = s0 + jj
                b_j, k_j, v_j = b_sc[j:j + 1, :], k_sc[j:j + 1, :], i_ref[j:j + 1, :]
                for hh, (b_h, q_h) in enumerate(halves):
                    first = jj - hh * hs
                    if first >= hs:
                        continue
                    d = b_h - b_j
                    if first > 0:
                        d = jnp.where(row_half >= first, d, NEG_INF)
                    a = jnp.sum(q_h * k_j * jnp.exp(d), axis=-1, keepdims=True)
                    acc_h[hh] = acc_h[hh] + a * v_j
            parts.append(jnp.concatenate(acc_h, axis=0))
        o = jnp.concatenate(parts, axis=0)
        bc = b_sc[r0:r0 + c, :]
        qc = q_sc[r0:r0 + c, :]
        kc = k_sc[r0:r0 + c, :]
        vc = i_ref[r0:r0 + c, :].astype(BF16)
        q_segs, k_segs = [], []
        for sub in range(1, c // sc):
            anchor = b_sc[r0 + sub * sc:r0 + sub * sc + 1, :]
            in_sub = jnp.logical_and(row_chunk >= sub * sc, row_chunk < (sub + 1) * sc)
            q_segs.append(qc * jnp.exp(jnp.where(in_sub, bc - anchor, NEG_INF)))
            k_segs.append(kc * jnp.exp(jnp.where(row_chunk < sub * sc, anchor - bc, NEG_INF)))
        att = _dot_nt(jnp.concatenate(q_segs, axis=1).astype(BF16),
                      jnp.concatenate(k_segs, axis=1).astype(BF16))
        state_t = state_ref[...]
        o = (o + _dot(att.astype(BF16), vc)
             + _dot_nt((qc * jnp.exp(bc)).astype(BF16), state_t.astype(BF16)))
        blast = b_sc[r0 + c - 1:r0 + c, :]
        kd = (kc * jnp.exp(blast - bc)).astype(BF16)
        state_ref[...] = jnp.exp(blast) * state_t + _dot_tn(vc, kd)
        ms = jnp.mean(o * o, axis=-1, keepdims=True)
        gate = gt_ref[r0:r0 + c, :]
        y = o * lax.rsqrt(ms + EPS) * gn * (gate * jax.nn.sigmoid(gate))
        o_ref[r0:r0 + c, :] = y.astype(o_ref.dtype)


def _hgrn2(proj3, lb, gn, *, rows=256):
    b, s, _ = proj3.shape
    rows = _pick(s, rows)
    blk = lambda off: pl.BlockSpec((None, rows, HEAD_W), lambda bi, h, ci: (bi, ci, off + h))
    return pl.pallas_call(
        functools.partial(_hgrn2_kernel, rows=rows),
        grid=(b, N_HEADS, s // rows),
        in_specs=[
            pl.BlockSpec((None, 1, HEAD_W), lambda bi, h, ci: (h, 0, 0)),
            pl.BlockSpec((1, HEAD_W), lambda bi, h, ci: (0, 0)),
            blk(OFF_HG_Q), blk(OFF_HG_F), blk(OFF_HG_I), blk(OFF_HG_G),
        ],
        out_specs=pl.BlockSpec((None, rows, HEAD_W), lambda bi, h, ci: (bi, ci, h)),
        out_shape=jax.ShapeDtypeStruct((b, s, BRANCH_W), BF16),
        scratch_shapes=[pltpu.VMEM((HEAD_W, HEAD_W), F32)] + [pltpu.VMEM((rows, HEAD_W), F32)] * 3,
        compiler_params=_params(("parallel", "parallel", "arbitrary")),
        name="hgrn2",
    )(lb.reshape(N_HEADS, 1, HEAD_W), gn.reshape(1, HEAD_W), proj3, proj3, proj3, proj3)


def _dil_kernel(slope_ref, *refs, tile):
    n_g = len(DIL_RATES)
    groups = [refs[5 * g:5 * g + 5] for g in range(n_g)]
    o_ref, o_sc, l_sc = refs[5 * n_g:]
    h = pl.program_id(1)
    blk = pl.program_id(2)
    t = DIL_BLOCK
    d_cur = (lax.broadcasted_iota(jnp.int32, (t, t), 0)
             - lax.broadcasted_iota(jnp.int32, (t, t), 1)).astype(F32)
    d_prev = d_cur + float(t)

    for g, (q_ref, k_ref, v_ref, kh_ref, vh_ref) in enumerate(groups):
        rate = DIL_RATES[g]
        halo = t * rate
        slope = slope_ref[g * N_HEADS + h] * float(rate)
        bias_cur = jnp.where(d_cur >= 0, -slope * d_cur, NEG_INF)
        bias_prev = jnp.where(d_prev <= float(t), -slope * d_prev, NEG_INF)
        bias_both = jnp.concatenate([bias_prev, bias_cur], axis=1)
        bias_first = jnp.concatenate([jnp.full((t, t), NEG_INF, F32), bias_cur], axis=1)

        def rows(start, rate=rate):
            return pl.ds(start, t) if rate == 1 else pl.ds(start, t, stride=rate)

        def units(ug, carry, g=g, rate=rate, halo=halo, q_ref=q_ref, k_ref=k_ref, v_ref=v_ref,
                  kh_ref=kh_ref, vh_ref=vh_ref, bias_both=bias_both, bias_first=bias_first, rows=rows):
            starts, qs, ks, vs, biases = [], [], [], [], []
            for uu in range(DIL_UNROLL):
                u = ug * DIL_UNROLL + uu
                r = u % rate
                mb = u // rate
                start = mb * halo + r
                starts.append(start)
                qs.append((q_ref[rows(start), :] * (HEAD_W ** -0.5)).astype(BF16))
                kp = kh_ref[rows(r), :]
                vp = vh_ref[rows(r), :]
                prev_ok = blk > 0
                if tile > halo:
                    pstart = jnp.maximum(start - halo, r)
                    in_tile = mb > 0
                    kp = jnp.where(in_tile, k_ref[rows(pstart), :], kp)
                    vp = jnp.where(in_tile, v_ref[rows(pstart), :], vp)
                    prev_ok = jnp.logical_or(prev_ok, in_tile)
                ks.append(jnp.concatenate([kp, k_ref[rows(start), :]], axis=0).astype(BF16))
                vs.append(jnp.concatenate([vp, v_ref[rows(start), :]], axis=0).astype(BF16))
                biases.append(jnp.where(prev_ok, bias_both, bias_first))
            ss = [_dot_nt(q, k) + bias for q, k, bias in zip(qs, ks, biases)]
            ms = [jnp.max(s, axis=-1, keepdims=True) for s in ss]
            ps = [jnp.exp(s - m) for s, m in zip(ss, ms)]
            ls = [jnp.sum(p, axis=-1, keepdims=True) for p in ps]
            accs = [_dot(p.astype(BF16), v) for p, v in zip(ps, vs)]
            for start, m, l, acc in zip(starts, ms, ls, accs):
                o = acc / l
                lse = jnp.broadcast_to(m + jnp.log(l), (t, HEAD_W))
                if g > 0:
                    lse_p = l_sc[rows(start), :]
                    lse_n = jnp.maximum(lse, lse_p)
                    lse_n = lse_n + jnp.log(jnp.exp(lse - lse_n) + jnp.exp(lse_p - lse_n))
                    o = o_sc[rows(start), :] * jnp.exp(lse_p - lse_n) + o * jnp.exp(lse - lse_n)
                    lse = lse_n
                o_sc[rows(start), :] = o
                if g < n_g - 1:
                    l_sc[rows(start), :] = lse
            return carry

        lax.fori_loop(0, tile // t // DIL_UNROLL, units, 0)

    o_ref[...] = o_sc[...].astype(o_ref.dtype)


def _dilated_attention(proj3, slopes4):
    b, s, _ = proj3.shape
    t = DIL_BLOCK
    tile = t * max(DIL_RATES)
    assert s % tile == 0
    in_specs, args = [], []
    for g, rate in enumerate(DIL_RATES):
        halo = t * rate
        per = tile // halo

        def cur(off, g=g):
            return pl.BlockSpec((None, tile, HEAD_W),
                                lambda bi, h, blk, sl: (bi, blk, off + g * N_HEADS + h))

        def front(off, g=g, halo=halo, per=per):
            return pl.BlockSpec((None, halo, HEAD_W),
                                lambda bi, h, blk, sl: (bi, jnp.maximum(blk * per - 1, 0), off + g * N_HEADS + h))

        in_specs += [cur(OFF_DL_Q), cur(OFF_DL_K), cur(OFF_DL_V), front(OFF_DL_K), front(OFF_DL_V)]
        args += [proj3] * 5
    grid_spec = pltpu.PrefetchScalarGridSpec(
        num_scalar_prefetch=1,
        grid=(b, N_HEADS, s // tile),
        in_specs=in_specs,
        out_specs=pl.BlockSpec((None, tile, HEAD_W), lambda bi, h, blk, sl: (bi, blk, h)),
        scratch_shapes=[pltpu.VMEM((tile, HEAD_W), F32), pltpu.VMEM((tile, HEAD_W), F32)],
    )
    return pl.pallas_call(
        functools.partial(_dil_kernel, tile=tile),
        grid_spec=grid_spec,
        out_shape=jax.ShapeDtypeStruct((b, s, BRANCH_W), BF16),
        compiler_params=_params(("parallel", "parallel", "parallel")),
        name="dilated",
    )(slopes4[:, 1:].T.reshape(-1), *args)


def _merge_kernel(ya_ref, yr_ref, yc_ref, yd_ref, w_ref, g0_ref, g1_ref, g2_ref, g3_ref, o_ref, wb_ref):
    @pl.when(pl.program_id(1) == 0)
    def _():
        _cast_weights(w_ref, wb_ref)

    acc = None
    for n, (y_ref, g_ref) in enumerate(((ya_ref, g0_ref), (yr_ref, g1_ref), (yc_ref, g2_ref), (yd_ref, g3_ref))):
        t = jax.nn.sigmoid(g_ref[...].astype(F32)) * _dot(y_ref[...], wb_ref[n])
        acc = t if acc is None else acc + t
    o_ref[...] = acc.astype(o_ref.dtype)


def _merge(ys, w_branch, layer, gates, d_model, *, tm=512, tn=1024):
    m = gates.shape[0]
    tm, tn = _pick(m, tm), _pick(d_model, tn)
    gcols = d_model // tn
    y_spec = pl.BlockSpec((tm, BRANCH_W), lambda j, i: (i, 0))

    def gate_spec(n):
        return pl.BlockSpec((tm, tn), lambda j, i: (i, n * gcols + j))

    return pl.pallas_call(
        _merge_kernel,
        grid=(d_model // tn, m // tm),
        in_specs=[y_spec, y_spec, y_spec, y_spec,
                  pl.BlockSpec((None, N_BRANCH, BRANCH_W, tn), lambda j, i: (layer, 0, 0, j),
                               pipeline_mode=pl.Buffered(1)),
                  gate_spec(0), gate_spec(1), gate_spec(2), gate_spec(3)],
        out_specs=pl.BlockSpec((tm, tn), lambda j, i: (i, j)),
        out_shape=jax.ShapeDtypeStruct((m, d_model), BF16),
        scratch_shapes=[pltpu.VMEM((N_BRANCH, BRANCH_W, tn), BF16)],
        compiler_params=_params(("parallel", "arbitrary")),
        name="merge",
    )(*ys, w_branch, gates, gates, gates, gates)


FFN_SUB_ROWS = 512


def _ffn_up_kernel(x_ref, w_hbm, cw_ref, cb_ref, o_ref, sa_ref, sg_ref, wab_ref, wgb_ref, sems,
                   carry_ref, *, tiles_per_seq, layer, n_blocks):
    i = pl.program_id(1)
    tn = o_ref.shape[1]
    _stage_weights(w_hbm, layer, 0, n_blocks, sa_ref, wab_ref, sems.at[0])
    _stage_weights(w_hbm, layer, n_blocks * tn, n_blocks, sg_ref, wgb_ref, sems.at[1])

    @pl.when(i % tiles_per_seq == 0)
    def _():
        carry_ref[...] = jnp.zeros(carry_ref.shape, F32)

    tm, tn = o_ref.shape
    sub = min(tm, FFN_SUB_ROWS)
    cw = cw_ref[...]
    cb = cb_ref[...]
    row = lax.broadcasted_iota(jnp.int32, (sub, tn), 0)
    prev = carry_ref[...]
    for r0 in range(0, tm, sub):
        x = x_ref[r0:r0 + sub, :]
        a = _dot(x, wab_ref[...])
        gl = _dot(x, wgb_ref[...])
        a1 = jnp.where(row == 0, prev[7:8, :], pltpu.roll(a, 1, 0))
        a2 = jnp.where(row == 0, prev[6:7, :],
                       jnp.where(row == 1, prev[7:8, :], pltpu.roll(a, 2, 0)))
        conv = cb + cw[0:1, :] * a2 + cw[1:2, :] * a1 + cw[2:3, :] * a
        gelu = 0.5 * conv * (1.0 + lax.erf(conv * (2.0 ** -0.5)))
        o_ref[r0:r0 + sub, :] = (gelu * gl).astype(o_ref.dtype)
        prev = a[sub - 8:sub, :]
    carry_ref[...] = prev


def _ffn_up(x, w_up, layer, conv_w, conv_b, seq, *, tm=2048, tn=256):
    m, kdim = x.shape
    d_ff = w_up.shape[-1] // 2
    tm, tn = _pick(seq, tm), _pick(d_ff, tn)
    nj = d_ff // tn
    return pl.pallas_call(
        functools.partial(_ffn_up_kernel, tiles_per_seq=seq // tm, layer=layer, n_blocks=nj),
        grid=(nj, m // tm),
        in_specs=[
            pl.BlockSpec((tm, kdim), lambda j, i: (i, 0)),
            pl.BlockSpec(memory_space=pl.ANY),
            pl.BlockSpec((CONV_W, tn), lambda j, i: (0, j)),
            pl.BlockSpec((1, tn), lambda j, i: (0, j)),
        ],
        out_specs=pl.BlockSpec((tm, tn), lambda j, i: (i, j)),
        out_shape=jax.ShapeDtypeStruct((m, d_ff), BF16),
        scratch_shapes=[pltpu.VMEM((kdim, tn), F32), pltpu.VMEM((kdim, tn), F32),
                        pltpu.VMEM((kdim, tn), BF16), pltpu.VMEM((kdim, tn), BF16),
                        pltpu.SemaphoreType.DMA((2,)), pltpu.VMEM((8, tn), F32)],
        compiler_params=_params(("arbitrary", "arbitrary")),
        name="ffn_up",
    )(x, w_up, conv_w, conv_b.reshape(1, d_ff))


def _alibi_slopes():
    n = 4 * N_HEADS
    slopes = 2.0 ** (-8.0 * jnp.arange(1, n + 1, dtype=F32) / n)
    return slopes.reshape(N_HEADS, 4)


def kernel(x, w_in, diff_lambda, diff_norm_g, hgrn_lb_logits, hgrn_norm_g, w_branch, w_o,
           ln1_g, ln1_b, w_up, conv_w, conv_b, w_down, ln2_g, ln2_b):
    b, s, d = x.shape
    depth = w_in.shape[0]
    m = b * s
    alpha = (2 * depth) ** 0.25
    slopes4 = _alibi_slopes()
    p = jax.nn.softmax(hgrn_lb_logits.astype(F32), axis=0)
    lower_bounds = jnp.cumsum(p, axis=0) - p[0]

    w_down_b = w_down.astype(BF16)
    h = x.reshape(m, d)
    hb = h.astype(BF16)
    for l in range(depth):
        lam_init = 0.8 - 0.6 * math.exp(-0.3 * l)
        n_mix = OFF_GATES * 128
        proj = _matmul(hb, w_in, l, out_dtype=F32, cols=(0, n_mix), tm=1024, name="in_proj")
        gates = _matmul(hb, w_in, l, out_dtype=BF16, cols=(n_mix, N_BRANCH * d), tm=1024, name="in_proj_gates")
        proj3 = proj.reshape(b, s, -1)
        ya = _diff_attention(proj3, slopes4[:, 0], diff_lambda[l], diff_norm_g[l], lam_init)
        yr = _retention(proj3)
        yc = _hgrn2(proj3, lower_bounds[l], hgrn_norm_g[l])
        yd = _dilated_attention(proj3, slopes4)
        ys = [y.reshape(m, BRANCH_W) for y in (ya, yr, yc, yd)]
        merged = _merge(ys, w_branch, l, gates, d)
        z = _matmul(merged, w_o, l, out_dtype=F32, res=h, alpha=alpha, name="out_proj")
        h, hb = _layernorm(z, ln1_g[l], ln1_b[l])
        act = _ffn_up(hb, w_up, l, conv_w[l], conv_b[l], s)
        z = _matmul(act, w_down_b, l, out_dtype=F32, tn=512, res=h, alpha=alpha, name="ffn_down")
        h, hb = _layernorm(z, ln2_g[l], ln2_b[l])
    return h.reshape(b, s, d)
```

```python
import functools
import math

import jax
import jax.numpy as jnp
from jax import lax
from jax.experimental import pallas as pl
from jax.experimental.pallas import tpu as pltpu

F32 = jnp.float32
BF16 = jnp.bfloat16

N_HEADS = 8
HEAD_W = 128
BRANCH_W = N_HEADS * HEAD_W
N_BRANCH = 4
DA_QK = 64
RET_QK = 64
RET_CHUNK = 256
RET_DECAY_BASE = 5.0
HG_CHUNK = 64
DIL_WINDOWS = (128, 512, 2048)
DIL_RATES = (1, 4, 16)
DIL_KEYS = 129
DIL_BLOCK = 128
DIL_UNROLL = 8
CONV_W = 3
EPS = 1e-5
NEG_INF = float("-inf")

_C = BRANCH_W // 128
OFF_DA_Q, OFF_DA_K, OFF_DA_V = 0, _C, 2 * _C
OFF_RT_Q, OFF_RT_K = 3 * _C, 3 * _C + _C // 2
OFF_RT_V, OFF_RT_G = 4 * _C, 5 * _C
OFF_HG_Q, OFF_HG_F, OFF_HG_I, OFF_HG_G = 6 * _C, 7 * _C, 8 * _C, 9 * _C
OFF_DL_Q, OFF_DL_K, OFF_DL_V = 10 * _C, 13 * _C, 16 * _C
OFF_GATES = 19 * _C

VMEM_LIMIT = 56 * 1024 * 1024


def _params(sem):
    return pltpu.CompilerParams(dimension_semantics=sem, vmem_limit_bytes=VMEM_LIMIT)


def _dot(a, b):
    return jnp.dot(a, b, preferred_element_type=F32)


def _dot_nt(a, b):
    return lax.dot_general(a, b, (((1,), (1,)), ((), ())), preferred_element_type=F32)


def _dot_tn(a, b):
    return lax.dot_general(a, b, (((0,), (0,)), ((), ())), preferred_element_type=F32)


def _pick(dim, pref):
    if dim <= pref:
        return dim
    t = pref
    while t >= 128:
        if dim % t == 0 and t % 128 == 0:
            return t
        t -= 128
    return dim


CAST_ROWS = 512


def _cast_weights(w_ref, wb_ref):
    rows = w_ref.shape[-2]
    step = min(rows, CAST_ROWS)
    for r0 in range(0, rows, step):
        r1 = min(rows, r0 + step)
        wb_ref[..., r0:r1, :] = w_ref[..., r0:r1, :].astype(BF16)


def _stage_weights(w_hbm, layer, first_col, n_blocks, stage_ref, wb_ref, sem):
    j = pl.program_id(0)
    tn = stage_ref.shape[-1]

    def copy(jj):
        col = pl.multiple_of(first_col + jj * tn, tn)
        return pltpu.make_async_copy(w_hbm.at[layer, :, pl.ds(col, tn)], stage_ref, sem)

    @pl.when(pl.program_id(1) == 0)
    def _():
        @pl.when(j == 0)
        def _():
            copy(0).start()

        copy(j).wait()
        _cast_weights(stage_ref, wb_ref)

        @pl.when(j + 1 < n_blocks)
        def _():
            copy(j + 1).start()


def _mm_kernel(*refs, alpha, has_res, layer, n_blocks):
    if has_res:
        x_ref, w_ref, r_ref, o_ref, *scratch = refs
    else:
        x_ref, w_ref, o_ref, *scratch = refs

    if scratch:
        stage_ref, wb_ref, sem = scratch
        _stage_weights(w_ref, layer, 0, n_blocks, stage_ref, wb_ref, sem)
    else:
        wb_ref = w_ref

    acc = _dot(x_ref[...], wb_ref[...])
    if has_res:
        acc = alpha * r_ref[...] + acc
    o_ref[...] = acc.astype(o_ref.dtype)


def _matmul(x, w, layer, *, out_dtype, tm=512, tn=1024, res=None, alpha=1.0, name="mm"):
    m, kdim = x.shape
    n = w.shape[-1]
    tm, tn = _pick(m, tm), _pick(n, tn)
    if w.dtype == BF16:
        w_spec = pl.BlockSpec((None, kdim, tn), lambda j, i: (layer, 0, j), pipeline_mode=pl.Buffered(1))
        scratch = []
    else:
        w_spec = pl.BlockSpec(memory_space=pl.ANY)
        scratch = [pltpu.VMEM((kdim, tn), F32), pltpu.VMEM((kdim, tn), BF16), pltpu.SemaphoreType.DMA(())]
    in_specs = [pl.BlockSpec((tm, kdim), lambda j, i: (i, 0)), w_spec]
    args = [x, w]
    if res is not None:
        in_specs.append(pl.BlockSpec((tm, tn), lambda j, i: (i, j)))
        args.append(res)
    return pl.pallas_call(
        functools.partial(_mm_kernel, alpha=alpha, has_res=res is not None, layer=layer, n_blocks=n // tn),
        grid=(n // tn, m // tm),
        in_specs=in_specs,
        out_specs=pl.BlockSpec((tm, tn), lambda j, i: (i, j)),
        out_shape=jax.ShapeDtypeStruct((m, n), out_dtype),
        scratch_shapes=scratch,
        compiler_params=_params(("arbitrary", "arbitrary")),
        name=name,
    )(*args)


def _ln_kernel(x_ref, g_ref, b_ref, o_ref, ob_ref):
    x = x_ref[...]
    mu = jnp.mean(x, axis=-1, keepdims=True)
    xc = x - mu
    var = jnp.mean(xc * xc, axis=-1, keepdims=True)
    y = xc * lax.rsqrt(var + EPS) * g_ref[...] + b_ref[...]
    o_ref[...] = y
    ob_ref[...] = y.astype(BF16)


def _layernorm(x, g, b, *, tm=256):
    m, d = x.shape
    tm = _pick(m, tm)
    return pl.pallas_call(
        _ln_kernel,
        grid=(m // tm,),
        in_specs=[
            pl.BlockSpec((tm, d), lambda i: (i, 0)),
            pl.BlockSpec((1, d), lambda i: (0, 0)),
            pl.BlockSpec((1, d), lambda i: (0, 0)),
        ],
        out_specs=[pl.BlockSpec((tm, d), lambda i: (i, 0)), pl.BlockSpec((tm, d), lambda i: (i, 0))],
        out_shape=[jax.ShapeDtypeStruct((m, d), F32), jax.ShapeDtypeStruct((m, d), BF16)],
        compiler_params=_params(("parallel",)),
        name="layernorm",
    )(x, g.reshape(1, d), b.reshape(1, d))


LOG2E = 1.4426950408889634


def _split3_f32(x):
    hi = x.astype(BF16).astype(F32)
    mid = (x - hi).astype(BF16).astype(F32)
    lo = (x - hi - mid).astype(BF16).astype(F32)
    return hi, mid, lo


def _lane_fill(lane, first, vals, other):
    out = other
    for n, val in enumerate(vals):
        out = jnp.where(lane == first + n, val, out)
    return out


def _diff_attn_kernel(slope_ref, lam_ref, g_ref, q_ref, k_ref, v_ref, o_ref,
                      m1_ref, a1_ref, m2_ref, a2_ref, kaug_sc, mask_sc, sa_sc, sb_sc, *, t, lam_init):
    h = pl.program_id(1)
    qi = pl.program_id(2)
    slope2 = slope_ref[h] * LOG2E
    lane = lax.broadcasted_iota(jnp.int32, (t, HEAD_W), 1)
    aug_first = (DA_QK, 0)
    owns = (lane < DA_QK, lane >= DA_QK)
    accs = ((m1_ref, a1_ref), (m2_ref, a2_ref))

    @pl.when(qi == 0)
    def _():
        j = lax.broadcasted_iota(jnp.int32, (t, HEAD_W), 0).astype(F32)
        parts = _split3_f32(slope2 * j)
        for idx in range(2):
            kaug_sc[idx] = _lane_fill(lane, aug_first[idx], list(parts) + [1.0, 1.0, 1.0], 0.0)
        d = (lax.broadcasted_iota(jnp.int32, (t, t), 0) - lax.broadcasted_iota(jnp.int32, (t, t), 1))
        mask_sc[...] = jnp.where(d >= 0, 0.0, NEG_INF)

    for m_ref, a_ref in accs:
        m_ref[...] = jnp.full(m_ref.shape, NEG_INF, F32)
        a_ref[...] = jnp.zeros(a_ref.shape, F32)

    q = q_ref[...] * (DA_QK ** -0.5 * LOG2E)
    ones_col = jnp.where(lane == 0, 1.0, 0.0).astype(BF16)

    def scores(ki, s_ref):
        r0 = pl.multiple_of(ki * t, t)
        block_c = jnp.full((1, HEAD_W), -slope2 * ((qi - ki) * t).astype(F32), F32)
        c_parts = _split3_f32(block_c)
        k = k_ref[pl.ds(r0, t), :]
        for idx in range(2):
            q_aug = _lane_fill(lane[0:1], aug_first[idx], [1.0, 1.0, 1.0] + list(c_parts), 0.0)
            qq = jnp.where(owns[idx], q, q_aug).astype(BF16)
            kk = jnp.where(owns[idx], k, kaug_sc[idx]).astype(BF16)
            s_ref[idx] = _dot_nt(qq, kk)

    def update(ki, s_ref, masked):
        r0 = pl.multiple_of(ki * t, t)
        v_aug = jnp.concatenate([v_ref[pl.ds(r0, t), :].astype(BF16), ones_col], axis=1)
        for idx, (m_ref, a_ref) in enumerate(accs):
            s = s_ref[idx]
            if masked:
                s = s + mask_sc[...]
            m_prev = m_ref[...]
            m_new = jnp.maximum(m_prev, jnp.max(s, axis=-1, keepdims=True))
            corr = jnp.exp2(m_prev - m_new)
            p = jnp.exp2(s - jnp.tile(m_new, (1, t // HEAD_W)))
            a_ref[...] = jnp.tile(corr, (1, 2)) * a_ref[...] + _dot(p.astype(BF16), v_aug)
            m_ref[...] = m_new

    scores(0, sa_sc)

    def body(pair, carry):
        k0 = 2 * pair
        scores(k0 + 1, sb_sc)
        update(k0, sa_sc, False)
        scores(k0 + 2, sa_sc)
        update(k0 + 1, sb_sc, False)
        return carry

    lax.fori_loop(0, qi // 2, body, 0)

    @pl.when(qi % 2 == 0)
    def _():
        update(qi, sa_sc, True)

    @pl.when(qi % 2 == 1)
    def _():
        scores(qi, sb_sc)
        update(qi - 1, sa_sc, False)
        update(qi, sb_sc, True)

    lv = lam_ref[...]
    lam = (jnp.exp(jnp.sum(lv[0:1] * lv[1:2], axis=-1, keepdims=True))
           - jnp.exp(jnp.sum(lv[2:3] * lv[3:4], axis=-1, keepdims=True)) + lam_init)
    o1 = a1_ref[:, :HEAD_W] / a1_ref[:, HEAD_W:HEAD_W + 1]
    o2 = a2_ref[:, :HEAD_W] / a2_ref[:, HEAD_W:HEAD_W + 1]
    o = o1 - lam * o2
    ms = jnp.mean(o * o, axis=-1, keepdims=True)
    o = o * lax.rsqrt(ms + EPS) * g_ref[...] * (1.0 - lam_init)
    o_ref[...] = o.astype(o_ref.dtype)


def _diff_attention(proj3, slopes, lam_vec, g, lam_init, *, t=512):
    b, s, _ = proj3.shape
    t = _pick(s, t)
    kern = functools.partial(_diff_attn_kernel, t=t, lam_init=lam_init)
    grid_spec = pltpu.PrefetchScalarGridSpec(
        num_scalar_prefetch=1,
        grid=(b, N_HEADS, s // t),
        in_specs=[
            pl.BlockSpec(lam_vec.shape, lambda bi, h, qi, sl: (0, 0)),
            pl.BlockSpec((1, HEAD_W), lambda bi, h, qi, sl: (0, 0)),
            pl.BlockSpec((None, t, HEAD_W), lambda bi, h, qi, sl: (bi, qi, OFF_DA_Q + h)),
            pl.BlockSpec((None, s, HEAD_W), lambda bi, h, qi, sl: (bi, 0, OFF_DA_K + h)),
            pl.BlockSpec((None, s, HEAD_W), lambda bi, h, qi, sl: (bi, 0, OFF_DA_V + h)),
        ],
        out_specs=pl.BlockSpec((None, t, HEAD_W), lambda bi, h, qi, sl: (bi, qi, h)),
        scratch_shapes=[pltpu.VMEM((t, HEAD_W), F32), pltpu.VMEM((t, 2 * HEAD_W), F32)] * 2
        + [pltpu.VMEM((2, t, HEAD_W), F32), pltpu.VMEM((t, t), F32)]
        + [pltpu.VMEM((2, t, t), F32)] * 2,
    )
    return pl.pallas_call(
        kern,
        grid_spec=grid_spec,
        out_shape=jax.ShapeDtypeStruct((b, s, BRANCH_W), BF16),
        compiler_params=_params(("parallel", "parallel", "arbitrary")),
        name="diff_attention",
    )(slopes, lam_vec, g.reshape(1, HEAD_W), proj3, proj3, proj3)


def _retention_kernel(inner_ref, qd_ref, kd_ref, cd_ref, q_ref, k_ref, v_ref, g_ref, o_ref,
                      state_ref, *, rows):
    ci = pl.program_id(2)

    @pl.when(ci == 0)
    def _():
        state_ref[...] = jnp.zeros(state_ref.shape, F32)

    c = RET_CHUNK
    for p in range(2):
        inner = inner_ref[p]
        qdec = qd_ref[p]
        kdec = kd_ref[p]
        cdec = cd_ref[p]
        state = state_ref[p]
        for n in range(rows // c):
            r0 = n * c
            qc = (q_ref[r0:r0 + c, p * RET_QK:(p + 1) * RET_QK] * (RET_QK ** -0.5)).astype(BF16)
            kf = k_ref[r0:r0 + c, p * RET_QK:(p + 1) * RET_QK]
            vc = v_ref[r0:r0 + c, p * HEAD_W:(p + 1) * HEAD_W].astype(BF16)
            att = _dot_nt(qc, kf.astype(BF16)) * inner
            o = _dot(att.astype(BF16), vc) + _dot(qc, state.astype(BF16)) * qdec
            state = cdec * state + _dot_tn((kf * kdec).astype(BF16), vc)
            mu = jnp.mean(o, axis=-1, keepdims=True)
            oc = o - mu
            var = jnp.mean(oc * oc, axis=-1, keepdims=True)
            gate = g_ref[r0:r0 + c, p * HEAD_W:(p + 1) * HEAD_W]
            y = oc * lax.rsqrt(var + EPS) * (gate * jax.nn.sigmoid(gate))
            o_ref[r0:r0 + c, p * HEAD_W:(p + 1) * HEAD_W] = y.astype(o_ref.dtype)
        state_ref[p] = state


def _retention_tables():
    h = N_HEADS
    c = RET_CHUNK
    log_g = jnp.log1p(-(2.0 ** (-RET_DECAY_BASE - jnp.arange(h, dtype=F32))))
    idx = jnp.arange(c, dtype=F32)
    rel = idx[:, None] - idx[None, :]
    inner = jnp.where(rel >= 0, jnp.exp(log_g[:, None, None] * jnp.maximum(rel, 0.0)), 0.0)
    qd = jnp.broadcast_to(jnp.exp(log_g[:, None] * (idx + 1.0))[:, :, None], (h, c, HEAD_W))
    kd = jnp.broadcast_to(jnp.exp(log_g[:, None] * (c - 1.0 - idx))[:, :, None], (h, c, RET_QK))
    cd = jnp.broadcast_to(jnp.exp(log_g * c)[:, None, None], (h, RET_QK, HEAD_W))
    return inner, qd, kd, cd


def _retention(proj3, *, rows=512):
    b, s, _ = proj3.shape
    rows = _pick(s, rows)
    inner, qd, kd, cd = _retention_tables()
    c = RET_CHUNK
    return pl.pallas_call(
        functools.partial(_retention_kernel, rows=rows),
        grid=(b, N_HEADS // 2, s // rows),
        in_specs=[
            pl.BlockSpec((2, c, c), lambda bi, hp, ci: (hp, 0, 0)),
            pl.BlockSpec((2, c, HEAD_W), lambda bi, hp, ci: (hp, 0, 0)),
            pl.BlockSpec((2, c, RET_QK), lambda bi, hp, ci: (hp, 0, 0)),
            pl.BlockSpec((2, RET_QK, HEAD_W), lambda bi, hp, ci: (hp, 0, 0)),
            pl.BlockSpec((None, rows, 128), lambda bi, hp, ci: (bi, ci, OFF_RT_Q + hp)),
            pl.BlockSpec((None, rows, 128), lambda bi, hp, ci: (bi, ci, OFF_RT_K + hp)),
            pl.BlockSpec((None, rows, 2 * HEAD_W), lambda bi, hp, ci: (bi, ci, OFF_RT_V // 2 + hp)),
            pl.BlockSpec((None, rows, 2 * HEAD_W), lambda bi, hp, ci: (bi, ci, OFF_RT_G // 2 + hp)),
        ],
        out_specs=pl.BlockSpec((None, rows, 2 * HEAD_W), lambda bi, hp, ci: (bi, ci, hp)),
        out_shape=jax.ShapeDtypeStruct((b, s, BRANCH_W), BF16),
        scratch_shapes=[pltpu.VMEM((2, RET_QK, HEAD_W), F32)],
        compiler_params=_params(("parallel", "parallel", "arbitrary")),
        name="retention",
    )(inner, qd, kd, cd, proj3, proj3, proj3, proj3)


def _split3(x):
    hi = x.astype(BF16)
    r1 = x - hi.astype(F32)
    mid = r1.astype(BF16)
    lo = (r1 - mid.astype(F32)).astype(BF16)
    return hi, mid, lo


HG_SUB = 16


def _hgrn2_kernel(lb_ref, gn_ref, q_ref, f_ref, i_ref, gt_ref, o_ref,
                  state_ref, b_sc, k_sc, q_sc, *, rows):
    ci = pl.program_id(2)
    c, sc = HG_CHUNK, HG_SUB

    @pl.when(ci == 0)
    def _():
        state_ref[...] = jnp.zeros(state_ref.shape, F32)

    lb = lb_ref[...]
    gn = gn_ref[...]
    fr = f_ref[...]
    log_f = jnp.log(lb + (1.0 - lb) * jax.nn.sigmoid(fr))
    k_sc[...] = (1.0 - lb) * jax.nn.sigmoid(-fr)
    qraw = q_ref[...]
    q_sc[...] = qraw * jax.nn.sigmoid(qraw)
    ri = lax.broadcasted_iota(jnp.int32, (rows, rows), 0)
    cj = lax.broadcasted_iota(jnp.int32, (rows, rows), 1)
    tri = jnp.where(jnp.logical_and(ri >= cj, (ri // c) == (cj // c)), 1.0, 0.0).astype(BF16)
    hi, mid, lo = _split3(log_f)
    b_sc[...] = _dot(tri, hi) + _dot(tri, mid) + _dot(tri, lo)
    hs = sc // 2
    row_half = lax.broadcasted_iota(jnp.int32, (hs, HEAD_W), 0)
    row_chunk = lax.broadcasted_iota(jnp.int32, (c, HEAD_W), 0)

    for n in range(rows // c):
        r0 = n * c
        parts = []
        for sub in range(c // sc):
            s0 = r0 + sub * sc
            b_i = b_sc[s0:s0 + sc, :]
            q_i = q_sc[s0:s0 + sc, :]
            halves = [(b_i[:hs], q_i[:hs]), (b_i[hs:], q_i[hs:])]
            acc_h = [jnp.zeros((hs, HEAD_W), F32), jnp.zeros((hs, HEAD_W), F32)]
            for jj in range(sc):
                j = s0 + jj
                b_j, k_j, v_j = b_sc[j:j + 1, :], k_sc[j:j + 1, :], i_ref[j:j + 1, :]
                for hh, (b_h, q_h) in enumerate(halves):
                    first = jj - hh * hs
                    if first >= hs:
                        continue
                    d = b_h - b_j
                    if first > 0:
                        d = jnp.where(row_half >= first, d, NEG_INF)
                    a = jnp.sum(q_h * k_j * jnp.exp(d), axis=-1, keepdims=True)
                    acc_h[hh] = acc_h[hh] + a * v_j
            parts.append(jnp.concatenate(acc_h, axis=0))
        o = jnp.concatenate(parts, axis=0)
        bc = b_sc[r0:r0 + c, :]
        qc = q_sc[r0:r0 + c, :]
        kc = k_sc[r0:r0 + c, :]
        vc = i_ref[r0:r0 + c, :].astype(BF16)
        q_segs, k_segs = [], []
        for sub in range(1, c // sc):
            anchor = b_sc[r0 + sub * sc:r0 + sub * sc + 1, :]
            in_sub = jnp.logical_and(row_chunk >= sub * sc, row_chunk < (sub + 1) * sc)
            q_segs.append(qc * jnp.exp(jnp.where(in_sub, bc - anchor, NEG_INF)))
            k_segs.append(kc * jnp.exp(jnp.where(row_chunk < sub * sc, anchor - bc, NEG_INF)))
        att = _dot_nt(jnp.concatenate(q_segs, axis=1).astype(BF16),
                      jnp.concatenate(k_segs, axis=1).astype(BF16))
        state_t = state_ref[...]
        o = (o + _dot(att.astype(BF16), vc)
             + _dot_nt((qc * jnp.exp(bc)).astype(BF16), state_t.astype(BF16)))
        blast = b_sc[r0 + c - 1:r0 + c, :]
        kd = (kc * jnp.exp(blast - bc)).astype(BF16)
        state_ref[...] = jnp.exp(blast) * state_t + _dot_tn(vc, kd)
        ms = jnp.mean(o * o, axis=-1, keepdims=True)
        gate = gt_ref[r0:r0 + c, :]
        y = o * lax.rsqrt(ms + EPS) * gn * (gate * jax.nn.sigmoid(gate))
        o_ref[r0:r0 + c, :] = y.astype(o_ref.dtype)


def _hgrn2(proj3, lb, gn, *, rows=256):
    b, s, _ = proj3.shape
    rows = _pick(s, rows)
    blk = lambda off: pl.BlockSpec((None, rows, HEAD_W), lambda bi, h, ci: (bi, ci, off + h))
    return pl.pallas_call(
        functools.partial(_hgrn2_kernel, rows=rows),
        grid=(b, N_HEADS, s // rows),
        in_specs=[
            pl.BlockSpec((None, 1, HEAD_W), lambda bi, h, ci: (h, 0, 0)),
            pl.BlockSpec((1, HEAD_W), lambda bi, h, ci: (0, 0)),
            blk(OFF_HG_Q), blk(OFF_HG_F), blk(OFF_HG_I), blk(OFF_HG_G),
        ],
        out_specs=pl.BlockSpec((None, rows, HEAD_W), lambda bi, h, ci: (bi, ci, h)),
        out_shape=jax.ShapeDtypeStruct((b, s, BRANCH_W), BF16),
        scratch_shapes=[pltpu.VMEM((HEAD_W, HEAD_W), F32)] + [pltpu.VMEM((rows, HEAD_W), F32)] * 3,
        compiler_params=_params(("parallel", "parallel", "arbitrary")),
        name="hgrn2",
    )(lb.reshape(N_HEADS, 1, HEAD_W), gn.reshape(1, HEAD_W), proj3, proj3, proj3, proj3)


def _dil_kernel(slope_ref, *refs, tile):
    n_g = len(DIL_RATES)
    groups = [refs[5 * g:5 * g + 5] for g in range(n_g)]
    o_ref, o_sc, l_sc = refs[5 * n_g:]
    h = pl.program_id(1)
    blk = pl.program_id(2)
    t = DIL_BLOCK
    d_cur = (lax.broadcasted_iota(jnp.int32, (t, t), 0)
             - lax.broadcasted_iota(jnp.int32, (t, t), 1)).astype(F32)
    d_prev = d_cur + float(t)

    for g, (q_ref, k_ref, v_ref, kh_ref, vh_ref) in enumerate(groups):
        rate = DIL_RATES[g]
        halo = t * rate
        slope = slope_ref[g * N_HEADS + h] * float(rate)
        bias_cur = jnp.where(d_cur >= 0, -slope * d_cur, NEG_INF)
        bias_prev = jnp.where(d_prev <= float(t), -slope * d_prev, NEG_INF)
        bias_both = jnp.concatenate([bias_prev, bias_cur], axis=1)
        bias_first = jnp.concatenate([jnp.full((t, t), NEG_INF, F32), bias_cur], axis=1)

        def rows(start, rate=rate):
            return pl.ds(start, t) if rate == 1 else pl.ds(start, t, stride=rate)

        def units(ug, carry, g=g, rate=rate, halo=halo, q_ref=q_ref, k_ref=k_ref, v_ref=v_ref,
                  kh_ref=kh_ref, vh_ref=vh_ref, bias_both=bias_both, bias_first=bias_first, rows=rows):
            starts, qs, ks, vs, biases = [], [], [], [], []
            for uu in range(DIL_UNROLL):
                u = ug * DIL_UNROLL + uu
                r = u % rate
                mb = u // rate
                start = mb * halo + r
                starts.append(start)
                qs.append((q_ref[rows(start), :] * (HEAD_W ** -0.5)).astype(BF16))
                kp = kh_ref[rows(r), :]
                vp = vh_ref[rows(r), :]
                prev_ok = blk > 0
                if tile > halo:
                    pstart = jnp.maximum(start - halo, r)
                    in_tile = mb > 0
                    kp = jnp.where(in_tile, k_ref[rows(pstart), :], kp)
                    vp = jnp.where(in_tile, v_ref[rows(pstart), :], vp)
                    prev_ok = jnp.logical_or(prev_ok, in_tile)
                ks.append(jnp.concatenate([kp, k_ref[rows(start), :]], axis=0).astype(BF16))
                vs.append(jnp.concatenate([vp, v_ref[rows(start), :]], axis=0).astype(BF16))
                biases.append(jnp.where(prev_ok, bias_both, bias_first))
            ss = [_dot_nt(q, k) + bias for q, k, bias in zip(qs, ks, biases)]
            ms = [jnp.max(s, axis=-1, keepdims=True) for s in ss]
            ps = [jnp.exp(s - m) for s, m in zip(ss, ms)]
            ls = [jnp.sum(p, axis=-1, keepdims=True) for p in ps]
            accs = [_dot(p.astype(BF16), v) for p, v in zip(ps, vs)]
            for start, m, l, acc in zip(starts, ms, ls, accs):
                o = acc / l
                lse = jnp.broadcast_to(m + jnp.log(l), (t, HEAD_W))
                if g > 0:
                    lse_p = l_sc[rows(start), :]
                    lse_n = jnp.maximum(lse, lse_p)
                    lse_n = lse_n + jnp.log(jnp.exp(lse - lse_n) + jnp.exp(lse_p - lse_n))
                    o = o_sc[rows(start), :] * jnp.exp(lse_p - lse_n) + o * jnp.exp(lse - lse_n)
                    lse = lse_n
                o_sc[rows(start), :] = o
                if g < n_g - 1:
                    l_sc[rows(start), :] = lse
            return carry

        lax.fori_loop(0, tile // t // DIL_UNROLL, units, 0)

    o_ref[...] = o_sc[...].astype(o_ref.dtype)


def _dilated_attention(proj3, slopes4):
    b, s, _ = proj3.shape
    t = DIL_BLOCK
    tile = t * max(DIL_RATES)
    assert s % tile == 0
    in_specs, args = [], []
    for g, rate in enumerate(DIL_RATES):
        halo = t * rate
        per = tile // halo

        def cur(off, g=g):
            return pl.BlockSpec((None, tile, HEAD_W),
                                lambda bi, h, blk, sl: (bi, blk, off + g * N_HEADS + h))

        def front(off, g=g, halo=halo, per=per):
            return pl.BlockSpec((None, halo, HEAD_W),
                                lambda bi, h, blk, sl: (bi, jnp.maximum(blk * per - 1, 0), off + g * N_HEADS + h))

        in_specs += [cur(OFF_DL_Q), cur(OFF_DL_K), cur(OFF_DL_V), front(OFF_DL_K), front(OFF_DL_V)]
        args += [proj3] * 5
    grid_spec = pltpu.PrefetchScalarGridSpec(
        num_scalar_prefetch=1,
        grid=(b, N_HEADS, s // tile),
        in_specs=in_specs,
        out_specs=pl.BlockSpec((None, tile, HEAD_W), lambda bi, h, blk, sl: (bi, blk, h)),
        scratch_shapes=[pltpu.VMEM((tile, HEAD_W), F32), pltpu.VMEM((tile, HEAD_W), F32)],
    )
    return pl.pallas_call(
        functools.partial(_dil_kernel, tile=tile),
        grid_spec=grid_spec,
        out_shape=jax.ShapeDtypeStruct((b, s, BRANCH_W), BF16),
        compiler_params=_params(("parallel", "parallel", "parallel")),
        name="dilated",
    )(slopes4[:, 1:].T.reshape(-1), *args)


def _merge_kernel(ya_ref, yr_ref, yc_ref, yd_ref, w_ref, g0_ref, g1_ref, g2_ref, g3_ref, o_ref, wb_ref):
    @pl.when(pl.program_id(1) == 0)
    def _():
        _cast_weights(w_ref, wb_ref)

    acc = None
    for n, (y_ref, g_ref) in enumerate(((ya_ref, g0_ref), (yr_ref, g1_ref), (yc_ref, g2_ref), (yd_ref, g3_ref))):
        t = jax.nn.sigmoid(g_ref[...]) * _dot(y_ref[...], wb_ref[n])
        acc = t if acc is None else acc + t
    o_ref[...] = acc.astype(o_ref.dtype)


def _merge(ys, w_branch, layer, proj, d_model, *, tm=512, tn=1024):
    m = proj.shape[0]
    tm, tn = _pick(m, tm), _pick(d_model, tn)
    gcols = d_model // tn
    goff = OFF_GATES * 128 // tn
    y_spec = pl.BlockSpec((tm, BRANCH_W), lambda j, i: (i, 0))

    def gate_spec(n):
        return pl.BlockSpec((tm, tn), lambda j, i: (i, goff + n * gcols + j))

    return pl.pallas_call(
        _merge_kernel,
        grid=(d_model // tn, m // tm),
        in_specs=[y_spec, y_spec, y_spec, y_spec,
                  pl.BlockSpec((None, N_BRANCH, BRANCH_W, tn), lambda j, i: (layer, 0, 0, j),
                               pipeline_mode=pl.Buffered(1)),
                  gate_spec(0), gate_spec(1), gate_spec(2), gate_spec(3)],
        out_specs=pl.BlockSpec((tm, tn), lambda j, i: (i, j)),
        out_shape=jax.ShapeDtypeStruct((m, d_model), BF16),
        scratch_shapes=[pltpu.VMEM((N_BRANCH, BRANCH_W, tn), BF16)],
        compiler_params=_params(("parallel", "arbitrary")),
        name="merge",
    )(*ys, w_branch, proj, proj, proj, proj)


FFN_SUB_ROWS = 512


def _ffn_up_kernel(x_ref, w_hbm, cw_ref, cb_ref, o_ref, sa_ref, sg_ref, wab_ref, wgb_ref, sems,
                   carry_ref, *, tiles_per_seq, layer, n_blocks):
    i = pl.program_id(1)
    tn = o_ref.shape[1]
    _stage_weights(w_hbm, layer, 0, n_blocks, sa_ref, wab_ref, sems.at[0])
    _stage_weights(w_hbm, layer, n_blocks * tn, n_blocks, sg_ref, wgb_ref, sems.at[1])

    @pl.when(i % tiles_per_seq == 0)
    def _():
        carry_ref[...] = jnp.zeros(carry_ref.shape, F32)

    tm, tn = o_ref.shape
    sub = min(tm, FFN_SUB_ROWS)
    cw = cw_ref[...]
    cb = cb_ref[...]
    row = lax.broadcasted_iota(jnp.int32, (sub, tn), 0)
    prev = carry_ref[...]
    for r0 in range(0, tm, sub):
        x = x_ref[r0:r0 + sub, :]
        a = _dot(x, wab_ref[...])
        gl = _dot(x, wgb_ref[...])
        a1 = jnp.where(row == 0, prev[7:8, :], pltpu.roll(a, 1, 0))
        a2 = jnp.where(row == 0, prev[6:7, :],
                       jnp.where(row == 1, prev[7:8, :], pltpu.roll(a, 2, 0)))
        conv = cb + cw[0:1, :] * a2 + cw[1:2, :] * a1 + cw[2:3, :] * a
        gelu = 0.5 * conv * (1.0 + lax.erf(conv * (2.0 ** -0.5)))
        o_ref[r0:r0 + sub, :] = (gelu * gl).astype(o_ref.dtype)
        prev = a[sub - 8:sub, :]
    carry_ref[...] = prev


def _ffn_up(x, w_up, layer, conv_w, conv_b, seq, *, tm=2048, tn=256):
    m, kdim = x.shape
    d_ff = w_up.shape[-1] // 2
    tm, tn = _pick(seq, tm), _pick(d_ff, tn)
    nj = d_ff // tn
    return pl.pallas_call(
        functools.partial(_ffn_up_kernel, tiles_per_seq=seq // tm, layer=layer, n_blocks=nj),
        grid=(nj, m // tm),
        in_specs=[
            pl.BlockSpec((tm, kdim), lambda j, i: (i, 0)),
            pl.BlockSpec(memory_space=pl.ANY),
            pl.BlockSpec((CONV_W, tn), lambda j, i: (0, j)),
            pl.BlockSpec((1, tn), lambda j, i: (0, j)),
        ],
        out_specs=pl.BlockSpec((tm, tn), lambda j, i: (i, j)),
        out_shape=jax.ShapeDtypeStruct((m, d_ff), BF16),
        scratch_shapes=[pltpu.VMEM((kdim, tn), F32), pltpu.VMEM((kdim, tn), F32),
                        pltpu.VMEM((kdim, tn), BF16), pltpu.VMEM((kdim, tn), BF16),
                        pltpu.SemaphoreType.DMA((2,)), pltpu.VMEM((8, tn), F32)],
        compiler_params=_params(("arbitrary", "arbitrary")),
        name="ffn_up",
    )(x, w_up, conv_w, conv_b.reshape(1, d_ff))


def _alibi_slopes():
    n = 4 * N_HEADS
    slopes = 2.0 ** (-8.0 * jnp.arange(1, n + 1, dtype=F32) / n)
    return slopes.reshape(N_HEADS, 4)


def kernel(x, w_in, diff_lambda, diff_norm_g, hgrn_lb_logits, hgrn_norm_g, w_branch, w_o,
           ln1_g, ln1_b, w_up, conv_w, conv_b, w_down, ln2_g, ln2_b):
    b, s, d = x.shape
    depth = w_in.shape[0]
    m = b * s
    alpha = (2 * depth) ** 0.25
    slopes4 = _alibi_slopes()
    p = jax.nn.softmax(hgrn_lb_logits.astype(F32), axis=0)
    lower_bounds = jnp.cumsum(p, axis=0) - p[0]

    w_down_b = w_down.astype(BF16)
    h = x.reshape(m, d)
    hb = h.astype(BF16)
    for l in range(depth):
        lam_init = 0.8 - 0.6 * math.exp(-0.3 * l)
        proj = _matmul(hb, w_in, l, out_dtype=F32, tm=1024, name="in_proj")
        proj3 = proj.reshape(b, s, -1)
        ya = _diff_attention(proj3, slopes4[:, 0], diff_lambda[l], diff_norm_g[l], lam_init)
        yr = _retention(proj3)
        yc = _hgrn2(proj3, lower_bounds[l], hgrn_norm_g[l])
        yd = _dilated_attention(proj3, slopes4)
        ys = [y.reshape(m, BRANCH_W) for y in (ya, yr, yc, yd)]
        merged = _merge(ys, w_branch, l, proj, d)
        z = _matmul(merged, w_o, l, out_dtype=F32, res=h, alpha=alpha, name="out_proj")
        h, hb = _layernorm(z, ln1_g[l], ln1_b[l])
        act = _ffn_up(hb, w_up, l, conv_w[l], conv_b[l], s)
        z = _matmul(act, w_down_b, l, out_dtype=F32, tn=512, res=h, alpha=alpha, name="ffn_down")
        h, hb = _layernorm(z, ln2_g[l], ln2_b[l])
    return h.reshape(b, s, d)
```

```python
import functools
import math

import jax
import jax.numpy as jnp
from jax import lax
from jax.experimental import pallas as pl
from jax.experimental.pallas import tpu as pltpu

F32 = jnp.float32
BF16 = jnp.bfloat16

N_HEADS = 8
HEAD_W = 128
BRANCH_W = N_HEADS * HEAD_W
N_BRANCH = 4
DA_QK = 64
RET_QK = 64
RET_CHUNK = 256
RET_DECAY_BASE = 5.0
HG_CHUNK = 64
DIL_RATES = (1, 4, 16)
DIL_BLOCK = 128
DIL_UNROLL = 8
CONV_W = 3
EPS = 1e-5
NEG_INF = float("-inf")

_C = BRANCH_W // 128
OFF_DA_Q, OFF_DA_K, OFF_DA_V = 0, _C, 2 * _C
OFF_RT_Q, OFF_RT_K = 3 * _C, 3 * _C + _C // 2
OFF_RT_V, OFF_RT_G = 4 * _C, 5 * _C
OFF_HG_Q, OFF_HG_F, OFF_HG_I, OFF_HG_G = 6 * _C, 7 * _C, 8 * _C, 9 * _C
OFF_DL_Q, OFF_DL_K, OFF_DL_V = 10 * _C, 13 * _C, 16 * _C
OFF_GATES = 19 * _C

VMEM_LIMIT = 56 * 1024 * 1024


def _params(sem):
    return pltpu.CompilerParams(dimension_semantics=sem, vmem_limit_bytes=VMEM_LIMIT)


def _dot(a, b):
    return jnp.dot(a, b, preferred_element_type=F32)


def _dot_nt(a, b):
    return lax.dot_general(a, b, (((1,), (1,)), ((), ())), preferred_element_type=F32)


def _dot_tn(a, b):
    return lax.dot_general(a, b, (((0,), (0,)), ((), ())), preferred_element_type=F32)


def _pick(dim, pref):
    if dim <= pref:
        return dim
    t = pref
    while t >= 128:
        if dim % t == 0 and t % 128 == 0:
            return t
        t -= 128
    return dim


CAST_ROWS = 512


def _cast_weights(w_ref, wb_ref):
    rows = w_ref.shape[-2]
    step = min(rows, CAST_ROWS)
    for r0 in range(0, rows, step):
        r1 = min(rows, r0 + step)
        wb_ref[..., r0:r1, :] = w_ref[..., r0:r1, :].astype(BF16)


def _stage_weights(w_hbm, layer, first_col, n_blocks, stage_ref, wb_ref, sem):
    j = pl.program_id(0)
    tn = stage_ref.shape[-1]

    def copy(jj):
        col = pl.multiple_of(first_col + jj * tn, tn)
        return pltpu.make_async_copy(w_hbm.at[layer, :, pl.ds(col, tn)], stage_ref, sem)

    @pl.when(pl.program_id(1) == 0)
    def _():
        @pl.when(j == 0)
        def _():
            copy(0).start()

        copy(j).wait()
        _cast_weights(stage_ref, wb_ref)

        @pl.when(j + 1 < n_blocks)
        def _():
            copy(j + 1).start()


def _mm_kernel(*refs, alpha, n_res, layer, n_blocks):
    x_ref, w_ref = refs[:2]
    res_refs = refs[2:2 + n_res]
    o_ref = refs[2 + n_res]
    scratch = refs[3 + n_res:]

    if scratch:
        stage_ref, wb_ref, sem = scratch
        _stage_weights(w_ref, layer, 0, n_blocks, stage_ref, wb_ref, sem)
    else:
        wb_ref = w_ref

    acc = _dot(x_ref[...], wb_ref[...])
    if n_res == 1:
        acc = alpha * res_refs[0][...] + acc
    elif n_res:
        z_ref, mu_ref, rs_ref, g_ref, b_ref = res_refs
        reps = (1, o_ref.shape[1] // HEAD_W)
        h = (z_ref[...] - jnp.tile(mu_ref[...], reps)) * jnp.tile(rs_ref[...], reps) * g_ref[...] + b_ref[...]
        acc = alpha * h + acc
    o_ref[...] = acc.astype(o_ref.dtype)


def _matmul(x, w, layer, *, out_dtype, tm=512, tn=1024, res=None, alpha=1.0, name="mm"):
    m, kdim = x.shape
    n = w.shape[-1]
    tm, tn = _pick(m, tm), _pick(n, tn)
    if w.dtype == BF16:
        w_spec = pl.BlockSpec((None, kdim, tn), lambda j, i: (layer, 0, j), pipeline_mode=pl.Buffered(1))
        scratch = []
    else:
        w_spec = pl.BlockSpec(memory_space=pl.ANY)
        scratch = [pltpu.VMEM((kdim, tn), F32), pltpu.VMEM((kdim, tn), BF16), pltpu.SemaphoreType.DMA(())]
    in_specs = [pl.BlockSpec((tm, kdim), lambda j, i: (i, 0)), w_spec]
    args = [x, w]
    n_res = 0
    if res is not None:
        tile = pl.BlockSpec((tm, tn), lambda j, i: (i, j))
        if isinstance(res, tuple):
            z, mean, rstd, gain, bias = res
            stat = pl.BlockSpec((tm, HEAD_W), lambda j, i: (i, 0))
            vec = pl.BlockSpec((1, tn), lambda j, i: (0, j))
            in_specs += [tile, stat, stat, vec, vec]
            args += [z, mean, rstd, gain.reshape(1, n), bias.reshape(1, n)]
        else:
            in_specs.append(tile)
            args.append(res)
        n_res = len(args) - 2
    return pl.pallas_call(
        functools.partial(_mm_kernel, alpha=alpha, n_res=n_res, layer=layer, n_blocks=n // tn),
        grid=(n // tn, m // tm),
        in_specs=in_specs,
        out_specs=pl.BlockSpec((tm, tn), lambda j, i: (i, j)),
        out_shape=jax.ShapeDtypeStruct((m, n), out_dtype),
        scratch_shapes=scratch,
        compiler_params=_params(("arbitrary", "arbitrary")),
        name=name,
    )(*args)


def _ln_kernel(x_ref, g_ref, b_ref, *out_refs, final):
    x = x_ref[...]
    mu = jnp.mean(x, axis=-1, keepdims=True)
    xc = x - mu
    rstd = lax.rsqrt(jnp.mean(xc * xc, axis=-1, keepdims=True) + EPS)
    y = xc * rstd * g_ref[...] + b_ref[...]
    if final:
        out_refs[0][...] = y
    else:
        ob_ref, mu_ref, rs_ref = out_refs
        ob_ref[...] = y.astype(BF16)
        mu_ref[...] = jnp.broadcast_to(mu, mu_ref.shape)
        rs_ref[...] = jnp.broadcast_to(rstd, rs_ref.shape)


def _layernorm(x, g, b, *, final, tm=256):
    m, d = x.shape
    tm = _pick(m, tm)
    row = pl.BlockSpec((tm, d), lambda i: (i, 0))
    stat = pl.BlockSpec((tm, HEAD_W), lambda i: (i, 0))
    if final:
        out_specs, out_shape = [row], [jax.ShapeDtypeStruct((m, d), F32)]
    else:
        out_specs = [row, stat, stat]
        out_shape = [jax.ShapeDtypeStruct((m, d), BF16)] + [jax.ShapeDtypeStruct((m, HEAD_W), F32)] * 2
    return pl.pallas_call(
        functools.partial(_ln_kernel, final=final),
        grid=(m // tm,),
        in_specs=[row, pl.BlockSpec((1, d), lambda i: (0, 0)), pl.BlockSpec((1, d), lambda i: (0, 0))],
        out_specs=out_specs,
        out_shape=out_shape,
        compiler_params=_params(("parallel",)),
        name="layernorm",
    )(x, g.reshape(1, d), b.reshape(1, d))


LOG2E = 1.4426950408889634


def _split3_f32(x):
    hi = x.astype(BF16).astype(F32)
    mid = (x - hi).astype(BF16).astype(F32)
    lo = (x - hi - mid).astype(BF16).astype(F32)
    return hi, mid, lo


def _lane_fill(lane, first, vals, other):
    out = other
    for n, val in enumerate(vals):
        out = jnp.where(lane == first + n, val, out)
    return out


def _diff_attn_kernel(slope_ref, lam_ref, g_ref, q_ref, k_ref, v_ref, o_ref,
                      m1_ref, a1_ref, m2_ref, a2_ref, kaug_sc, mask_sc, sa_sc, sb_sc, *, t, lam_init):
    h = pl.program_id(1)
    qi = pl.program_id(2)
    slope2 = slope_ref[h] * LOG2E
    lane = lax.broadcasted_iota(jnp.int32, (t, HEAD_W), 1)
    aug_first = (DA_QK, 0)
    owns = (lane < DA_QK, lane >= DA_QK)
    accs = ((m1_ref, a1_ref), (m2_ref, a2_ref))

    @pl.when(qi == 0)
    def _():
        j = lax.broadcasted_iota(jnp.int32, (t, HEAD_W), 0).astype(F32)
        parts = _split3_f32(slope2 * j)
        for idx in range(2):
            kaug_sc[idx] = _lane_fill(lane, aug_first[idx], list(parts) + [1.0, 1.0, 1.0], 0.0)
        d = (lax.broadcasted_iota(jnp.int32, (t, t), 0) - lax.broadcasted_iota(jnp.int32, (t, t), 1))
        mask_sc[...] = jnp.where(d >= 0, 0.0, NEG_INF)

    for m_ref, a_ref in accs:
        m_ref[...] = jnp.full(m_ref.shape, NEG_INF, F32)
        a_ref[...] = jnp.zeros(a_ref.shape, F32)

    q = q_ref[...] * (DA_QK ** -0.5 * LOG2E)
    ones_col = jnp.where(lane == 0, 1.0, 0.0).astype(BF16)

    def scores(ki, s_ref):
        r0 = pl.multiple_of(ki * t, t)
        block_c = jnp.full((1, HEAD_W), -slope2 * ((qi - ki) * t).astype(F32), F32)
        c_parts = _split3_f32(block_c)
        k = k_ref[pl.ds(r0, t), :]
        for idx in range(2):
            q_aug = _lane_fill(lane[0:1], aug_first[idx], [1.0, 1.0, 1.0] + list(c_parts), 0.0)
            qq = jnp.where(owns[idx], q, q_aug).astype(BF16)
            kk = jnp.where(owns[idx], k, kaug_sc[idx]).astype(BF16)
            s_ref[idx] = _dot_nt(qq, kk)

    def update(ki, s_ref, masked):
        r0 = pl.multiple_of(ki * t, t)
        v_aug = jnp.concatenate([v_ref[pl.ds(r0, t), :].astype(BF16), ones_col], axis=1)
        for idx, (m_ref, a_ref) in enumerate(accs):
            s = s_ref[idx]
            if masked:
                s = s + mask_sc[...]
            m_prev = m_ref[...]
            m_new = jnp.maximum(m_prev, jnp.max(s, axis=-1, keepdims=True))
            corr = jnp.exp2(m_prev - m_new)
            p = jnp.exp2(s - jnp.tile(m_new, (1, t // HEAD_W)))
            a_ref[...] = jnp.tile(corr, (1, 2)) * a_ref[...] + _dot(p.astype(BF16), v_aug)
            m_ref[...] = m_new

    scores(0, sa_sc)

    def body(pair, carry):
        k0 = 2 * pair
        scores(k0 + 1, sb_sc)
        update(k0, sa_sc, False)
        scores(k0 + 2, sa_sc)
        update(k0 + 1, sb_sc, False)
        return carry

    lax.fori_loop(0, qi // 2, body, 0)

    @pl.when(qi % 2 == 0)
    def _():
        update(qi, sa_sc, True)

    @pl.when(qi % 2 == 1)
    def _():
        scores(qi, sb_sc)
        update(qi - 1, sa_sc, False)
        update(qi, sb_sc, True)

    lv = lam_ref[...]
    lam = (jnp.exp(jnp.sum(lv[0:1] * lv[1:2], axis=-1, keepdims=True))
           - jnp.exp(jnp.sum(lv[2:3] * lv[3:4], axis=-1, keepdims=True)) + lam_init)
    o1 = a1_ref[:, :HEAD_W] / a1_ref[:, HEAD_W:HEAD_W + 1]
    o2 = a2_ref[:, :HEAD_W] / a2_ref[:, HEAD_W:HEAD_W + 1]
    o = o1 - lam * o2
    ms = jnp.mean(o * o, axis=-1, keepdims=True)
    o = o * lax.rsqrt(ms + EPS) * g_ref[...] * (1.0 - lam_init)
    o_ref[...] = o.astype(o_ref.dtype)


def _diff_attention(proj3, slopes, lam_vec, g, lam_init, *, t=512):
    b, s, _ = proj3.shape
    t = _pick(s, t)
    kern = functools.partial(_diff_attn_kernel, t=t, lam_init=lam_init)
    grid_spec = pltpu.PrefetchScalarGridSpec(
        num_scalar_prefetch=1,
        grid=(b, N_HEADS, s // t),
        in_specs=[
            pl.BlockSpec(lam_vec.shape, lambda bi, h, qi, sl: (0, 0)),
            pl.BlockSpec((1, HEAD_W), lambda bi, h, qi, sl: (0, 0)),
            pl.BlockSpec((None, t, HEAD_W), lambda bi, h, qi, sl: (bi, qi, OFF_DA_Q + h)),
            pl.BlockSpec((None, s, HEAD_W), lambda bi, h, qi, sl: (bi, 0, OFF_DA_K + h)),
            pl.BlockSpec((None, s, HEAD_W), lambda bi, h, qi, sl: (bi, 0, OFF_DA_V + h)),
        ],
        out_specs=pl.BlockSpec((None, t, HEAD_W), lambda bi, h, qi, sl: (bi, qi, h)),
        scratch_shapes=[pltpu.VMEM((t, HEAD_W), F32), pltpu.VMEM((t, 2 * HEAD_W), F32)] * 2
        + [pltpu.VMEM((2, t, HEAD_W), F32), pltpu.VMEM((t, t), F32)]
        + [pltpu.VMEM((2, t, t), F32)] * 2,
    )
    return pl.pallas_call(
        kern,
        grid_spec=grid_spec,
        out_shape=jax.ShapeDtypeStruct((b, s, BRANCH_W), BF16),
        compiler_params=_params(("parallel", "parallel", "arbitrary")),
        name="diff_attention",
    )(slopes, lam_vec, g.reshape(1, HEAD_W), proj3, proj3, proj3)


def _retention_kernel(inner_ref, qd_ref, kd_ref, cd_ref, q_ref, k_ref, v_ref, g_ref, o_ref,
                      state_ref, *, rows):
    ci = pl.program_id(2)

    @pl.when(ci == 0)
    def _():
        state_ref[...] = jnp.zeros(state_ref.shape, F32)

    c = RET_CHUNK
    for p in range(2):
        inner = inner_ref[p]
        qdec = qd_ref[p]
        kdec = kd_ref[p]
        cdec = cd_ref[p]
        state = state_ref[p]
        for n in range(rows // c):
            r0 = n * c
            qc = (q_ref[r0:r0 + c, p * RET_QK:(p + 1) * RET_QK] * (RET_QK ** -0.5)).astype(BF16)
            kf = k_ref[r0:r0 + c, p * RET_QK:(p + 1) * RET_QK]
            vc = v_ref[r0:r0 + c, p * HEAD_W:(p + 1) * HEAD_W].astype(BF16)
            att = _dot_nt(qc, kf.astype(BF16)) * inner
            o = _dot(att.astype(BF16), vc) + _dot(qc, state.astype(BF16)) * qdec
            state = cdec * state + _dot_tn((kf * kdec).astype(BF16), vc)
            mu = jnp.mean(o, axis=-1, keepdims=True)
            oc = o - mu
            var = jnp.mean(oc * oc, axis=-1, keepdims=True)
            gate = g_ref[r0:r0 + c, p * HEAD_W:(p + 1) * HEAD_W]
            y = oc * lax.rsqrt(var + EPS) * (gate * jax.nn.sigmoid(gate))
            o_ref[r0:r0 + c, p * HEAD_W:(p + 1) * HEAD_W] = y.astype(o_ref.dtype)
        state_ref[p] = state


def _retention_tables():
    h = N_HEADS
    c = RET_CHUNK
    log_g = jnp.log1p(-(2.0 ** (-RET_DECAY_BASE - jnp.arange(h, dtype=F32))))
    idx = jnp.arange(c, dtype=F32)
    rel = idx[:, None] - idx[None, :]
    inner = jnp.where(rel >= 0, jnp.exp(log_g[:, None, None] * jnp.maximum(rel, 0.0)), 0.0)
    qd = jnp.broadcast_to(jnp.exp(log_g[:, None] * (idx + 1.0))[:, :, None], (h, c, HEAD_W))
    kd = jnp.broadcast_to(jnp.exp(log_g[:, None] * (c - 1.0 - idx))[:, :, None], (h, c, RET_QK))
    cd = jnp.broadcast_to(jnp.exp(log_g * c)[:, None, None], (h, RET_QK, HEAD_W))
    return inner, qd, kd, cd


def _retention(proj3, *, rows=512):
    b, s, _ = proj3.shape
    rows = _pick(s, rows)
    inner, qd, kd, cd = _retention_tables()
    c = RET_CHUNK
    return pl.pallas_call(
        functools.partial(_retention_kernel, rows=rows),
        grid=(b, N_HEADS // 2, s // rows),
        in_specs=[
            pl.BlockSpec((2, c, c), lambda bi, hp, ci: (hp, 0, 0)),
            pl.BlockSpec((2, c, HEAD_W), lambda bi, hp, ci: (hp, 0, 0)),
            pl.BlockSpec((2, c, RET_QK), lambda bi, hp, ci: (hp, 0, 0)),
            pl.BlockSpec((2, RET_QK, HEAD_W), lambda bi, hp, ci: (hp, 0, 0)),
            pl.BlockSpec((None, rows, 128), lambda bi, hp, ci: (bi, ci, OFF_RT_Q + hp)),
            pl.BlockSpec((None, rows, 128), lambda bi, hp, ci: (bi, ci, OFF_RT_K + hp)),
            pl.BlockSpec((None, rows, 2 * HEAD_W), lambda bi, hp, ci: (bi, ci, OFF_RT_V // 2 + hp)),
            pl.BlockSpec((None, rows, 2 * HEAD_W), lambda bi, hp, ci: (bi, ci, OFF_RT_G // 2 + hp)),
        ],
        out_specs=pl.BlockSpec((None, rows, 2 * HEAD_W), lambda bi, hp, ci: (bi, ci, hp)),
        out_shape=jax.ShapeDtypeStruct((b, s, BRANCH_W), BF16),
        scratch_shapes=[pltpu.VMEM((2, RET_QK, HEAD_W), F32)],
        compiler_params=_params(("parallel", "parallel", "arbitrary")),
        name="retention",
    )(inner, qd, kd, cd, proj3, proj3, proj3, proj3)


def _split3(x):
    hi = x.astype(BF16)
    r1 = x - hi.astype(F32)
    mid = r1.astype(BF16)
    lo = (r1 - mid.astype(F32)).astype(BF16)
    return hi, mid, lo


HG_SUB = 16


def _hgrn2_kernel(lb_ref, gn_ref, q_ref, f_ref, i_ref, gt_ref, o_ref,
                  state_ref, b_sc, k_sc, q_sc, *, rows):
    ci = pl.program_id(2)
    c, sc = HG_CHUNK, HG_SUB

    @pl.when(ci == 0)
    def _():
        state_ref[...] = jnp.zeros(state_ref.shape, F32)

    lb = lb_ref[...]
    gn = gn_ref[...]
    fr = f_ref[...]
    log_f = jnp.log(lb + (1.0 - lb) * jax.nn.sigmoid(fr))
    k_sc[...] = (1.0 - lb) * jax.nn.sigmoid(-fr)
    qraw = q_ref[...]
    q_sc[...] = qraw * jax.nn.sigmoid(qraw)
    ri = lax.broadcasted_iota(jnp.int32, (rows, rows), 0)
    cj = lax.broadcasted_iota(jnp.int32, (rows, rows), 1)
    tri = jnp.where(jnp.logical_and(ri >= cj, (ri // c) == (cj // c)), 1.0, 0.0).astype(BF16)
    hi, mid, lo = _split3(log_f)
    b_sc[...] = _dot(tri, hi) + _dot(tri, mid) + _dot(tri, lo)
    hs = sc // 2
    row_half = lax.broadcasted_iota(jnp.int32, (hs, HEAD_W), 0)
    row_chunk = lax.broadcasted_iota(jnp.int32, (c, HEAD_W), 0)

    for n in range(rows // c):
        r0 = n * c
        parts = []
        for sub in range(c // sc):
            s0 = r0 + sub * sc
            b_i = b_sc[s0:s0 + sc, :]
            q_i = q_sc[s0:s0 + sc, :]
            halves = [(b_i[:hs], q_i[:hs]), (b_i[hs:], q_i[hs:])]
            acc_h = [jnp.zeros((hs, HEAD_W), F32), jnp.zeros((hs, HEAD_W), F32)]
            for jj in range(sc):
                j = s0 + jj
                b_j, k_j, v_j = b_sc[j:j + 1, :], k_sc[j:j + 1, :], i_ref[j:j + 1, :]
                for hh, (b_h, q_h) in enumerate(halves):
                    first = jj - hh * hs
                    if first >= hs:
                        continue
                    d = b_h - b_j
                    if first > 0:
                        d = jnp.where(row_half >= first, d, NEG_INF)
                    a = jnp.sum(q_h * k_j * jnp.exp(d), axis=-1, keepdims=True)
                    acc_h[hh] = acc_h[hh] + a * v_j
            parts.append(jnp.concatenate(acc_h, axis=0))
        o = jnp.concatenate(parts, axis=0)
        bc = b_sc[r0:r0 + c, :]
        qc = q_sc[r0:r0 + c, :]
        kc = k_sc[r0:r0 + c, :]
        vc = i_ref[r0:r0 + c, :].astype(BF16)
        q_segs, k_segs = [], []
        for sub in range(1, c // sc):
            anchor = b_sc[r0 + sub * sc:r0 + sub * sc + 1, :]
            in_sub = jnp.logical_and(row_chunk >= sub * sc, row_chunk < (sub + 1) * sc)
            q_segs.append(qc * jnp.exp(jnp.where(in_sub, bc - anchor, NEG_INF)))
            k_segs.append(kc * jnp.exp(jnp.where(row_chunk < sub * sc, anchor - bc, NEG_INF)))
        att = _dot_nt(jnp.concatenate(q_segs, axis=1).astype(BF16),
                      jnp.concatenate(k_segs, axis=1).astype(BF16))
        state_t = state_ref[...]
        o = (o + _dot(att.astype(BF16), vc)
             + _dot_nt((qc * jnp.exp(bc)).astype(BF16), state_t.astype(BF16)))
        blast = b_sc[r0 + c - 1:r0 + c, :]
        kd = (kc * jnp.exp(blast - bc)).astype(BF16)
        state_ref[...] = jnp.exp(blast) * state_t + _dot_tn(vc, kd)
        ms = jnp.mean(o * o, axis=-1, keepdims=True)
        gate = gt_ref[r0:r0 + c, :]
        y = o * lax.rsqrt(ms + EPS) * gn * (gate * jax.nn.sigmoid(gate))
        o_ref[r0:r0 + c, :] = y.astype(o_ref.dtype)


def _hgrn2(proj3, lb, gn, *, rows=256):
    b, s, _ = proj3.shape
    rows = _pick(s, rows)
    blk = lambda off: pl.BlockSpec((None, rows, HEAD_W), lambda bi, h, ci: (bi, ci, off + h))
    return pl.pallas_call(
        functools.partial(_hgrn2_kernel, rows=rows),
        grid=(b, N_HEADS, s // rows),
        in_specs=[
            pl.BlockSpec((None, 1, HEAD_W), lambda bi, h, ci: (h, 0, 0)),
            pl.BlockSpec((1, HEAD_W), lambda bi, h, ci: (0, 0)),
            blk(OFF_HG_Q), blk(OFF_HG_F), blk(OFF_HG_I), blk(OFF_HG_G),
        ],
        out_specs=pl.BlockSpec((None, rows, HEAD_W), lambda bi, h, ci: (bi, ci, h)),
        out_shape=jax.ShapeDtypeStruct((b, s, BRANCH_W), BF16),
        scratch_shapes=[pltpu.VMEM((HEAD_W, HEAD_W), F32)] + [pltpu.VMEM((rows, HEAD_W), F32)] * 3,
        compiler_params=_params(("parallel", "parallel", "arbitrary")),
        name="hgrn2",
    )(lb.reshape(N_HEADS, 1, HEAD_W), gn.reshape(1, HEAD_W), proj3, proj3, proj3, proj3)


def _dil_kernel(slope_ref, *refs, tile):
    n_g = len(DIL_RATES)
    groups = [refs[5 * g:5 * g + 5] for g in range(n_g)]
    o_ref, o_sc, l_sc = refs[5 * n_g:]
    h = pl.program_id(1)
    blk = pl.program_id(2)
    t = DIL_BLOCK
    d_cur = (lax.broadcasted_iota(jnp.int32, (t, t), 0)
             - lax.broadcasted_iota(jnp.int32, (t, t), 1)).astype(F32)
    d_prev = d_cur + float(t)

    for g, (q_ref, k_ref, v_ref, kh_ref, vh_ref) in enumerate(groups):
        rate = DIL_RATES[g]
        halo = t * rate
        slope = slope_ref[g * N_HEADS + h] * float(rate)
        bias_cur = jnp.where(d_cur >= 0, -slope * d_cur, NEG_INF)
        bias_prev = jnp.where(d_prev <= float(t), -slope * d_prev, NEG_INF)
        bias_both = jnp.concatenate([bias_prev, bias_cur], axis=1)
        bias_first = jnp.concatenate([jnp.full((t, t), NEG_INF, F32), bias_cur], axis=1)

        def rows(start, rate=rate):
            return pl.ds(start, t) if rate == 1 else pl.ds(start, t, stride=rate)

        def units(ug, carry, g=g, rate=rate, halo=halo, q_ref=q_ref, k_ref=k_ref, v_ref=v_ref,
                  kh_ref=kh_ref, vh_ref=vh_ref, bias_both=bias_both, bias_first=bias_first, rows=rows):
            starts, qs, ks, vs, biases = [], [], [], [], []
            for uu in range(DIL_UNROLL):
                u = ug * DIL_UNROLL + uu
                r = u % rate
                mb = u // rate
                start = mb * halo + r
                starts.append(start)
                qs.append((q_ref[rows(start), :] * (HEAD_W ** -0.5)).astype(BF16))
                kp = kh_ref[rows(r), :]
                vp = vh_ref[rows(r), :]
                prev_ok = blk > 0
                if tile > halo:
                    pstart = jnp.maximum(start - halo, r)
                    in_tile = mb > 0
                    kp = jnp.where(in_tile, k_ref[rows(pstart), :], kp)
                    vp = jnp.where(in_tile, v_ref[rows(pstart), :], vp)
                    prev_ok = jnp.logical_or(prev_ok, in_tile)
                ks.append(jnp.concatenate([kp, k_ref[rows(start), :]], axis=0).astype(BF16))
                vs.append(jnp.concatenate([vp, v_ref[rows(start), :]], axis=0).astype(BF16))
                biases.append(jnp.where(prev_ok, bias_both, bias_first))
            ss = [_dot_nt(q, k) + bias for q, k, bias in zip(qs, ks, biases)]
            ms = [jnp.max(s, axis=-1, keepdims=True) for s in ss]
            ps = [jnp.exp(s - m) for s, m in zip(ss, ms)]
            ls = [jnp.sum(p, axis=-1, keepdims=True) for p in ps]
            accs = [_dot(p.astype(BF16), v) for p, v in zip(ps, vs)]
            for start, m, l, acc in zip(starts, ms, ls, accs):
                o = acc / l
                lse = jnp.broadcast_to(m + jnp.log(l), (t, HEAD_W))
                if g > 0:
                    lse_p = l_sc[rows(start), :]
                    lse_n = jnp.maximum(lse, lse_p)
                    lse_n = lse_n + jnp.log(jnp.exp(lse - lse_n) + jnp.exp(lse_p - lse_n))
                    o = o_sc[rows(start), :] * jnp.exp(lse_p - lse_n) + o * jnp.exp(lse - lse_n)
                    lse = lse_n
                o_sc[rows(start), :] = o
                if g < n_g - 1:
                    l_sc[rows(start), :] = lse
            return carry

        lax.fori_loop(0, tile // t // DIL_UNROLL, units, 0)

    o_ref[...] = o_sc[...].astype(o_ref.dtype)


def _dilated_attention(proj3, slopes4):
    b, s, _ = proj3.shape
    t = DIL_BLOCK
    tile = t * max(DIL_RATES)
    assert s % tile == 0
    in_specs, args = [], []
    for g, rate in enumerate(DIL_RATES):
        halo = t * rate
        per = tile // halo

        def cur(off, g=g):
            return pl.BlockSpec((None, tile, HEAD_W),
                                lambda bi, h, blk, sl: (bi, blk, off + g * N_HEADS + h))

        def front(off, g=g, halo=halo, per=per):
            return pl.BlockSpec((None, halo, HEAD_W),
                                lambda bi, h, blk, sl: (bi, jnp.maximum(blk * per - 1, 0), off + g * N_HEADS + h))

        in_specs += [cur(OFF_DL_Q), cur(OFF_DL_K), cur(OFF_DL_V), front(OFF_DL_K), front(OFF_DL_V)]
        args += [proj3] * 5
    grid_spec = pltpu.PrefetchScalarGridSpec(
        num_scalar_prefetch=1,
        grid=(b, N_HEADS, s // tile),
        in_specs=in_specs,
        out_specs=pl.BlockSpec((None, tile, HEAD_W), lambda bi, h, blk, sl: (bi, blk, h)),
        scratch_shapes=[pltpu.VMEM((tile, HEAD_W), F32), pltpu.VMEM((tile, HEAD_W), F32)],
    )
    return pl.pallas_call(
        functools.partial(_dil_kernel, tile=tile),
        grid_spec=grid_spec,
        out_shape=jax.ShapeDtypeStruct((b, s, BRANCH_W), BF16),
        compiler_params=_params(("parallel", "parallel", "parallel")),
        name="dilated",
    )(slopes4[:, 1:].T.reshape(-1), *args)


def _merge_kernel(ya_ref, yr_ref, yc_ref, yd_ref, w_ref, g0_ref, g1_ref, g2_ref, g3_ref, o_ref, wb_ref):
    @pl.when(pl.program_id(1) == 0)
    def _():
        _cast_weights(w_ref, wb_ref)

    acc = None
    for n, (y_ref, g_ref) in enumerate(((ya_ref, g0_ref), (yr_ref, g1_ref), (yc_ref, g2_ref), (yd_ref, g3_ref))):
        t = jax.nn.sigmoid(g_ref[...]) * _dot(y_ref[...], wb_ref[n])
        acc = t if acc is None else acc + t
    o_ref[...] = acc.astype(o_ref.dtype)


def _merge(ys, w_branch, layer, proj, d_model, *, tm=512, tn=1024):
    m = proj.shape[0]
    tm, tn = _pick(m, tm), _pick(d_model, tn)
    gcols = d_model // tn
    goff = OFF_GATES * 128 // tn
    y_spec = pl.BlockSpec((tm, BRANCH_W), lambda j, i: (i, 0))

    def gate_spec(n):
        return pl.BlockSpec((tm, tn), lambda j, i: (i, goff + n * gcols + j))

    return pl.pallas_call(
        _merge_kernel,
        grid=(d_model // tn, m // tm),
        in_specs=[y_spec, y_spec, y_spec, y_spec,
                  pl.BlockSpec((None, N_BRANCH, BRANCH_W, tn), lambda j, i: (layer, 0, 0, j),
                               pipeline_mode=pl.Buffered(1)),
                  gate_spec(0), gate_spec(1), gate_spec(2), gate_spec(3)],
        out_specs=pl.BlockSpec((tm, tn), lambda j, i: (i, j)),
        out_shape=jax.ShapeDtypeStruct((m, d_model), BF16),
        scratch_shapes=[pltpu.VMEM((N_BRANCH, BRANCH_W, tn), BF16)],
        compiler_params=_params(("parallel", "arbitrary")),
        name="merge",
    )(*ys, w_branch, proj, proj, proj, proj)


FFN_SUB_ROWS = 512


def _ffn_up_kernel(x_ref, w_hbm, cw_ref, cb_ref, o_ref, sa_ref, sg_ref, wab_ref, wgb_ref, sems,
                   carry_ref, *, tiles_per_seq, layer, n_blocks):
    i = pl.program_id(1)
    tn = o_ref.shape[1]
    _stage_weights(w_hbm, layer, 0, n_blocks, sa_ref, wab_ref, sems.at[0])
    _stage_weights(w_hbm, layer, n_blocks * tn, n_blocks, sg_ref, wgb_ref, sems.at[1])

    @pl.when(i % tiles_per_seq == 0)
    def _():
        carry_ref[...] = jnp.zeros(carry_ref.shape, F32)

    tm, tn = o_ref.shape
    sub = min(tm, FFN_SUB_ROWS)
    cw = cw_ref[...]
    cb = cb_ref[...]
    row = lax.broadcasted_iota(jnp.int32, (sub, tn), 0)
    prev = carry_ref[...]
    for r0 in range(0, tm, sub):
        x = x_ref[r0:r0 + sub, :]
        a = _dot(x, wab_ref[...])
        gl = _dot(x, wgb_ref[...])
        a1 = jnp.where(row == 0, prev[7:8, :], pltpu.roll(a, 1, 0))
        a2 = jnp.where(row == 0, prev[6:7, :],
                       jnp.where(row == 1, prev[7:8, :], pltpu.roll(a, 2, 0)))
        conv = cb + cw[0:1, :] * a2 + cw[1:2, :] * a1 + cw[2:3, :] * a
        gelu = 0.5 * conv * (1.0 + lax.erf(conv * (2.0 ** -0.5)))
        o_ref[r0:r0 + sub, :] = (gelu * gl).astype(o_ref.dtype)
        prev = a[sub - 8:sub, :]
    carry_ref[...] = prev


def _ffn_up(x, w_up, layer, conv_w, conv_b, seq, *, tm=2048, tn=256):
    m, kdim = x.shape
    d_ff = w_up.shape[-1] // 2
    tm, tn = _pick(seq, tm), _pick(d_ff, tn)
    nj = d_ff // tn
    return pl.pallas_call(
        functools.partial(_ffn_up_kernel, tiles_per_seq=seq // tm, layer=layer, n_blocks=nj),
        grid=(nj, m // tm),
        in_specs=[
            pl.BlockSpec((tm, kdim), lambda j, i: (i, 0)),
            pl.BlockSpec(memory_space=pl.ANY),
            pl.BlockSpec((CONV_W, tn), lambda j, i: (0, j)),
            pl.BlockSpec((1, tn), lambda j, i: (0, j)),
        ],
        out_specs=pl.BlockSpec((tm, tn), lambda j, i: (i, j)),
        out_shape=jax.ShapeDtypeStruct((m, d_ff), BF16),
        scratch_shapes=[pltpu.VMEM((kdim, tn), F32), pltpu.VMEM((kdim, tn), F32),
                        pltpu.VMEM((kdim, tn), BF16), pltpu.VMEM((kdim, tn), BF16),
                        pltpu.SemaphoreType.DMA((2,)), pltpu.VMEM((8, tn), F32)],
        compiler_params=_params(("arbitrary", "arbitrary")),
        name="ffn_up",
    )(x, w_up, conv_w, conv_b.reshape(1, d_ff))


def _alibi_slopes():
    n = 4 * N_HEADS
    slopes = 2.0 ** (-8.0 * jnp.arange(1, n + 1, dtype=F32) / n)
    return slopes.reshape(N_HEADS, 4)


def kernel(x, w_in, diff_lambda, diff_norm_g, hgrn_lb_logits, hgrn_norm_g, w_branch, w_o,
           ln1_g, ln1_b, w_up, conv_w, conv_b, w_down, ln2_g, ln2_b):
    b, s, d = x.shape
    depth = w_in.shape[0]
    m = b * s
    alpha = (2 * depth) ** 0.25
    slopes4 = _alibi_slopes()
    p = jax.nn.softmax(hgrn_lb_logits.astype(F32), axis=0)
    lower_bounds = jnp.cumsum(p, axis=0) - p[0]

    w_down_b = w_down.astype(BF16)
    res = x.reshape(m, d)
    hb = res.astype(BF16)
    for l in range(depth):
        lam_init = 0.8 - 0.6 * math.exp(-0.3 * l)
        proj = _matmul(hb, w_in, l, out_dtype=F32, tm=1024, name="in_proj")
        proj3 = proj.reshape(b, s, -1)
        ya = _diff_attention(proj3, slopes4[:, 0], diff_lambda[l], diff_norm_g[l], lam_init)
        yr = _retention(proj3)
        yc = _hgrn2(proj3, lower_bounds[l], hgrn_norm_g[l])
        yd = _dilated_attention(proj3, slopes4)
        ys = [y.reshape(m, BRANCH_W) for y in (ya, yr, yc, yd)]
        merged = _merge(ys, w_branch, l, proj, d)
        z = _matmul(merged, w_o, l, out_dtype=F32, res=res, alpha=alpha, name="out_proj")
        hb, mean, rstd = _layernorm(z, ln1_g[l], ln1_b[l], final=False)
        res = (z, mean, rstd, ln1_g[l], ln1_b[l])
        act = _ffn_up(hb, w_up, l, conv_w[l], conv_b[l], s)
        z = _matmul(act, w_down_b, l, out_dtype=F32, tn=512, res=res, alpha=alpha, name="ffn_down")
        if l == depth - 1:
            (out,) = _layernorm(z, ln2_g[l], ln2_b[l], final=True)
            return out.reshape(b, s, d)
        hb, mean, rstd = _layernorm(z, ln2_g[l], ln2_b[l], final=False)
        res = (z, mean, rstd, ln2_g[l], ln2_b[l])
```

```python
import functools
import math

import jax
import jax.numpy as jnp
from jax import lax
from jax.experimental import pallas as pl
from jax.experimental.pallas import tpu as pltpu

F32 = jnp.float32
BF16 = jnp.bfloat16

N_HEADS = 8
HEAD_W = 128
BRANCH_W = N_HEADS * HEAD_W
N_BRANCH = 4
DA_QK = 64
RET_QK = 64
RET_CHUNK = 256
RET_DECAY_BASE = 5.0
HG_CHUNK = 64
DIL_RATES = (1, 4, 16)
DIL_BLOCK = 128
DIL_UNROLL = 8
CONV_W = 3
EPS = 1e-5
NEG_INF = float("-inf")

_C = BRANCH_W // 128
OFF_DA_Q, OFF_DA_K, OFF_DA_V = 0, _C, 2 * _C
OFF_RT_Q, OFF_RT_K = 3 * _C, 3 * _C + _C // 2
OFF_RT_V, OFF_RT_G = 4 * _C, 5 * _C
OFF_HG_Q, OFF_HG_F, OFF_HG_I, OFF_HG_G = 6 * _C, 7 * _C, 8 * _C, 9 * _C
OFF_DL_Q, OFF_DL_K, OFF_DL_V = 10 * _C, 13 * _C, 16 * _C
OFF_GATES = 19 * _C

VMEM_LIMIT = 56 * 1024 * 1024


def _params(sem):
    return pltpu.CompilerParams(dimension_semantics=sem, vmem_limit_bytes=VMEM_LIMIT)


def _dot(a, b):
    return jnp.dot(a, b, preferred_element_type=F32)


def _dot_nt(a, b):
    return lax.dot_general(a, b, (((1,), (1,)), ((), ())), preferred_element_type=F32)


def _dot_tn(a, b):
    return lax.dot_general(a, b, (((0,), (0,)), ((), ())), preferred_element_type=F32)


def _pick(dim, pref):
    if dim <= pref:
        return dim
    t = pref
    while t >= 128:
        if dim % t == 0 and t % 128 == 0:
            return t
        t -= 128
    return dim


CAST_ROWS = 512


def _cast_weights(w_ref, wb_ref):
    rows = w_ref.shape[-2]
    step = min(rows, CAST_ROWS)
    for r0 in range(0, rows, step):
        r1 = min(rows, r0 + step)
        wb_ref[..., r0:r1, :] = w_ref[..., r0:r1, :].astype(BF16)


def _stage_weights(w_hbm, layer, first_col, n_blocks, stage_ref, wb_ref, sem):
    j = pl.program_id(0)
    tn = stage_ref.shape[-1]

    def copy(jj):
        col = pl.multiple_of(first_col + jj * tn, tn)
        return pltpu.make_async_copy(w_hbm.at[layer, :, pl.ds(col, tn)], stage_ref, sem)

    @pl.when(pl.program_id(1) == 0)
    def _():
        @pl.when(j == 0)
        def _():
            copy(0).start()

        copy(j).wait()
        _cast_weights(stage_ref, wb_ref)

        @pl.when(j + 1 < n_blocks)
        def _():
            copy(j + 1).start()


def _mm_kernel(*refs, alpha, n_res, layer, n_blocks):
    x_ref, w_ref = refs[:2]
    res_refs = refs[2:2 + n_res]
    o_ref = refs[2 + n_res]
    scratch = refs[3 + n_res:]

    if scratch:
        stage_ref, wb_ref, sem = scratch
        _stage_weights(w_ref, layer, 0, n_blocks, stage_ref, wb_ref, sem)
    else:
        wb_ref = w_ref

    acc = _dot(x_ref[...], wb_ref[...])
    if n_res == 1:
        acc = alpha * res_refs[0][...] + acc
    elif n_res:
        z_ref, mu_ref, rs_ref, g_ref, b_ref = res_refs
        reps = (1, o_ref.shape[1] // HEAD_W)
        h = (z_ref[...] - jnp.tile(mu_ref[...], reps)) * jnp.tile(rs_ref[...], reps) * g_ref[...] + b_ref[...]
        acc = alpha * h + acc
    o_ref[...] = acc.astype(o_ref.dtype)


def _matmul(x, w, layer, *, out_dtype, tm=512, tn=1024, res=None, alpha=1.0, name="mm"):
    m, kdim = x.shape
    n = w.shape[-1]
    tm, tn = _pick(m, tm), _pick(n, tn)
    if w.dtype == BF16:
        w_spec = pl.BlockSpec((None, kdim, tn), lambda j, i: (layer, 0, j), pipeline_mode=pl.Buffered(1))
        scratch = []
    else:
        w_spec = pl.BlockSpec(memory_space=pl.ANY)
        scratch = [pltpu.VMEM((kdim, tn), F32), pltpu.VMEM((kdim, tn), BF16), pltpu.SemaphoreType.DMA(())]
    in_specs = [pl.BlockSpec((tm, kdim), lambda j, i: (i, 0)), w_spec]
    args = [x, w]
    n_res = 0
    if res is not None:
        tile = pl.BlockSpec((tm, tn), lambda j, i: (i, j))
        if isinstance(res, tuple):
            z, mean, rstd, gain, bias = res
            stat = pl.BlockSpec((tm, HEAD_W), lambda j, i: (i, 0))
            vec = pl.BlockSpec((1, tn), lambda j, i: (0, j))
            in_specs += [tile, stat, stat, vec, vec]
            args += [z, mean, rstd, gain.reshape(1, n), bias.reshape(1, n)]
        else:
            in_specs.append(tile)
            args.append(res)
        n_res = len(args) - 2
    return pl.pallas_call(
        functools.partial(_mm_kernel, alpha=alpha, n_res=n_res, layer=layer, n_blocks=n // tn),
        grid=(n // tn, m // tm),
        in_specs=in_specs,
        out_specs=pl.BlockSpec((tm, tn), lambda j, i: (i, j)),
        out_shape=jax.ShapeDtypeStruct((m, n), out_dtype),
        scratch_shapes=scratch,
        compiler_params=_params(("arbitrary", "arbitrary")),
        name=name,
    )(*args)


def _ln_kernel(x_ref, g_ref, b_ref, *out_refs, final):
    x = x_ref[...]
    mu = jnp.mean(x, axis=-1, keepdims=True)
    xc = x - mu
    rstd = lax.rsqrt(jnp.mean(xc * xc, axis=-1, keepdims=True) + EPS)
    y = xc * rstd * g_ref[...] + b_ref[...]
    if final:
        out_refs[0][...] = y
    else:
        ob_ref, mu_ref, rs_ref = out_refs
        ob_ref[...] = y.astype(BF16)
        mu_ref[...] = jnp.broadcast_to(mu, mu_ref.shape)
        rs_ref[...] = jnp.broadcast_to(rstd, rs_ref.shape)


def _layernorm(x, g, b, *, final, tm=256):
    m, d = x.shape
    tm = _pick(m, tm)
    row = pl.BlockSpec((tm, d), lambda i: (i, 0))
    stat = pl.BlockSpec((tm, HEAD_W), lambda i: (i, 0))
    if final:
        out_specs, out_shape = [row], [jax.ShapeDtypeStruct((m, d), F32)]
    else:
        out_specs = [row, stat, stat]
        out_shape = [jax.ShapeDtypeStruct((m, d), BF16)] + [jax.ShapeDtypeStruct((m, HEAD_W), F32)] * 2
    return pl.pallas_call(
        functools.partial(_ln_kernel, final=final),
        grid=(m // tm,),
        in_specs=[row, pl.BlockSpec((1, d), lambda i: (0, 0)), pl.BlockSpec((1, d), lambda i: (0, 0))],
        out_specs=out_specs,
        out_shape=out_shape,
        compiler_params=_params(("parallel",)),
        name="layernorm",
    )(x, g.reshape(1, d), b.reshape(1, d))


LOG2E = 1.4426950408889634


def _split3_f32(x):
    hi = x.astype(BF16).astype(F32)
    mid = (x - hi).astype(BF16).astype(F32)
    lo = (x - hi - mid).astype(BF16).astype(F32)
    return hi, mid, lo


def _lane_fill(lane, first, vals, other):
    out = other
    for n, val in enumerate(vals):
        out = jnp.where(lane == first + n, val, out)
    return out


def _diff_attn_kernel(slope_ref, lam_ref, g_ref, q_ref, k_ref, v_ref, o_ref,
                      m1_ref, a1_ref, m2_ref, a2_ref, kaug_sc, mask_sc, sa_sc, sb_sc, *, t, lam_init):
    h = pl.program_id(1)
    qi = pl.program_id(2)
    slope2 = slope_ref[h] * LOG2E
    lane = lax.broadcasted_iota(jnp.int32, (t, HEAD_W), 1)
    aug_first = (DA_QK, 0)
    owns = (lane < DA_QK, lane >= DA_QK)
    accs = ((m1_ref, a1_ref), (m2_ref, a2_ref))

    @pl.when(qi == 0)
    def _():
        j = lax.broadcasted_iota(jnp.int32, (t, HEAD_W), 0).astype(F32)
        parts = _split3_f32(slope2 * j)
        for idx in range(2):
            kaug_sc[idx] = _lane_fill(lane, aug_first[idx], list(parts) + [1.0, 1.0, 1.0], 0.0)
        d = (lax.broadcasted_iota(jnp.int32, (t, t), 0) - lax.broadcasted_iota(jnp.int32, (t, t), 1))
        mask_sc[...] = jnp.where(d >= 0, 0.0, NEG_INF)

    for m_ref, a_ref in accs:
        m_ref[...] = jnp.full(m_ref.shape, NEG_INF, F32)
        a_ref[...] = jnp.zeros(a_ref.shape, F32)

    q = q_ref[...] * (DA_QK ** -0.5 * LOG2E)
    ones_col = jnp.where(lane == 0, 1.0, 0.0).astype(BF16)

    def scores(ki, s_ref):
        r0 = pl.multiple_of(ki * t, t)
        block_c = jnp.full((1, HEAD_W), -slope2 * ((qi - ki) * t).astype(F32), F32)
        c_parts = _split3_f32(block_c)
        k = k_ref[pl.ds(r0, t), :]
        for idx in range(2):
            q_aug = _lane_fill(lane[0:1], aug_first[idx], [1.0, 1.0, 1.0] + list(c_parts), 0.0)
            qq = jnp.where(owns[idx], q, q_aug).astype(BF16)
            kk = jnp.where(owns[idx], k, kaug_sc[idx]).astype(BF16)
            s_ref[idx] = _dot_nt(qq, kk)

    def update(ki, s_ref, masked):
        r0 = pl.multiple_of(ki * t, t)
        v_aug = jnp.concatenate([v_ref[pl.ds(r0, t), :].astype(BF16), ones_col], axis=1)
        for idx, (m_ref, a_ref) in enumerate(accs):
            s = s_ref[idx]
            if masked:
                s = s + mask_sc[...]
            m_prev = m_ref[...]
            m_new = jnp.maximum(m_prev, jnp.max(s, axis=-1, keepdims=True))
            corr = jnp.exp2(m_prev - m_new)
            p = jnp.exp2(s - jnp.tile(m_new, (1, t // HEAD_W)))
            a_ref[...] = jnp.tile(corr, (1, 2)) * a_ref[...] + _dot(p.astype(BF16), v_aug)
            m_ref[...] = m_new

    scores(0, sa_sc)

    def body(pair, carry):
        k0 = 2 * pair
        scores(k0 + 1, sb_sc)
        update(k0, sa_sc, False)
        scores(k0 + 2, sa_sc)
        update(k0 + 1, sb_sc, False)
        return carry

    lax.fori_loop(0, qi // 2, body, 0)

    @pl.when(qi % 2 == 0)
    def _():
        update(qi, sa_sc, True)

    @pl.when(qi % 2 == 1)
    def _():
        scores(qi, sb_sc)
        update(qi - 1, sa_sc, False)
        update(qi, sb_sc, True)

    lv = lam_ref[...]
    lam = (jnp.exp(jnp.sum(lv[0:1] * lv[1:2], axis=-1, keepdims=True))
           - jnp.exp(jnp.sum(lv[2:3] * lv[3:4], axis=-1, keepdims=True)) + lam_init)
    o1 = a1_ref[:, :HEAD_W] / a1_ref[:, HEAD_W:HEAD_W + 1]
    o2 = a2_ref[:, :HEAD_W] / a2_ref[:, HEAD_W:HEAD_W + 1]
    o = o1 - lam * o2
    ms = jnp.mean(o * o, axis=-1, keepdims=True)
    o = o * lax.rsqrt(ms + EPS) * g_ref[...] * (1.0 - lam_init)
    o_ref[...] = o.astype(o_ref.dtype)


def _diff_attention(proj3, slopes, lam_vec, g, lam_init, *, t=512):
    b, s, _ = proj3.shape
    t = _pick(s, t)
    kern = functools.partial(_diff_attn_kernel, t=t, lam_init=lam_init)
    grid_spec = pltpu.PrefetchScalarGridSpec(
        num_scalar_prefetch=1,
        grid=(b, N_HEADS, s // t),
        in_specs=[
            pl.BlockSpec(lam_vec.shape, lambda bi, h, qi, sl: (0, 0)),
            pl.BlockSpec((1, HEAD_W), lambda bi, h, qi, sl: (0, 0)),
            pl.BlockSpec((None, t, HEAD_W), lambda bi, h, qi, sl: (bi, qi, OFF_DA_Q + h)),
            pl.BlockSpec((None, s, HEAD_W), lambda bi, h, qi, sl: (bi, 0, OFF_DA_K + h)),
            pl.BlockSpec((None, s, HEAD_W), lambda bi, h, qi, sl: (bi, 0, OFF_DA_V + h)),
        ],
        out_specs=pl.BlockSpec((None, t, HEAD_W), lambda bi, h, qi, sl: (bi, qi, h)),
        scratch_shapes=[pltpu.VMEM((t, HEAD_W), F32), pltpu.VMEM((t, 2 * HEAD_W), F32)] * 2
        + [pltpu.VMEM((2, t, HEAD_W), F32), pltpu.VMEM((t, t), F32)]
        + [pltpu.VMEM((2, t, t), F32)] * 2,
    )
    return pl.pallas_call(
        kern,
        grid_spec=grid_spec,
        out_shape=jax.ShapeDtypeStruct((b, s, BRANCH_W), BF16),
        compiler_params=_params(("parallel", "parallel", "arbitrary")),
        name="diff_attention",
    )(slopes, lam_vec, g.reshape(1, HEAD_W), proj3, proj3, proj3)


def _retention_kernel(inner_ref, qd_ref, kd_ref, cd_ref, q_ref, k_ref, v_ref, g_ref, o_ref,
                      state_ref, *, rows):
    ci = pl.program_id(2)

    @pl.when(ci == 0)
    def _():
        state_ref[...] = jnp.zeros(state_ref.shape, F32)

    c = RET_CHUNK
    for p in range(2):
        inner = inner_ref[p]
        qdec = qd_ref[p]
        kdec = kd_ref[p]
        cdec = cd_ref[p]
        state = state_ref[p]
        for n in range(rows // c):
            r0 = n * c
            qc = (q_ref[r0:r0 + c, p * RET_QK:(p + 1) * RET_QK] * (RET_QK ** -0.5)).astype(BF16)
            kf = k_ref[r0:r0 + c, p * RET_QK:(p + 1) * RET_QK]
            vc = v_ref[r0:r0 + c, p * HEAD_W:(p + 1) * HEAD_W].astype(BF16)
            att = _dot_nt(qc, kf.astype(BF16)) * inner
            o = _dot(att.astype(BF16), vc) + _dot(qc, state.astype(BF16)) * qdec
            state = cdec * state + _dot_tn((kf * kdec).astype(BF16), vc)
            mu = jnp.mean(o, axis=-1, keepdims=True)
            oc = o - mu
            var = jnp.mean(oc * oc, axis=-1, keepdims=True)
            gate = g_ref[r0:r0 + c, p * HEAD_W:(p + 1) * HEAD_W]
            y = oc * lax.rsqrt(var + EPS) * (gate * jax.nn.sigmoid(gate))
            o_ref[r0:r0 + c, p * HEAD_W:(p + 1) * HEAD_W] = y.astype(o_ref.dtype)
        state_ref[p] = state


def _retention_tables():
    h = N_HEADS
    c = RET_CHUNK
    log_g = jnp.log1p(-(2.0 ** (-RET_DECAY_BASE - jnp.arange(h, dtype=F32))))
    idx = jnp.arange(c, dtype=F32)
    rel = idx[:, None] - idx[None, :]
    inner = jnp.where(rel >= 0, jnp.exp(log_g[:, None, None] * jnp.maximum(rel, 0.0)), 0.0)
    qd = jnp.broadcast_to(jnp.exp(log_g[:, None] * (idx + 1.0))[:, :, None], (h, c, HEAD_W))
    kd = jnp.broadcast_to(jnp.exp(log_g[:, None] * (c - 1.0 - idx))[:, :, None], (h, c, RET_QK))
    cd = jnp.broadcast_to(jnp.exp(log_g * c)[:, None, None], (h, RET_QK, HEAD_W))
    return inner, qd, kd, cd


def _retention(proj3, *, rows=512):
    b, s, _ = proj3.shape
    rows = _pick(s, rows)
    inner, qd, kd, cd = _retention_tables()
    c = RET_CHUNK
    return pl.pallas_call(
        functools.partial(_retention_kernel, rows=rows),
        grid=(b, N_HEADS // 2, s // rows),
        in_specs=[
            pl.BlockSpec((2, c, c), lambda bi, hp, ci: (hp, 0, 0)),
            pl.BlockSpec((2, c, HEAD_W), lambda bi, hp, ci: (hp, 0, 0)),
            pl.BlockSpec((2, c, RET_QK), lambda bi, hp, ci: (hp, 0, 0)),
            pl.BlockSpec((2, RET_QK, HEAD_W), lambda bi, hp, ci: (hp, 0, 0)),
            pl.BlockSpec((None, rows, 128), lambda bi, hp, ci: (bi, ci, OFF_RT_Q + hp)),
            pl.BlockSpec((None, rows, 128), lambda bi, hp, ci: (bi, ci, OFF_RT_K + hp)),
            pl.BlockSpec((None, rows, 2 * HEAD_W), lambda bi, hp, ci: (bi, ci, OFF_RT_V // 2 + hp)),
            pl.BlockSpec((None, rows, 2 * HEAD_W), lambda bi, hp, ci: (bi, ci, OFF_RT_G // 2 + hp)),
        ],
        out_specs=pl.BlockSpec((None, rows, 2 * HEAD_W), lambda bi, hp, ci: (bi, ci, hp)),
        out_shape=jax.ShapeDtypeStruct((b, s, BRANCH_W), BF16),
        scratch_shapes=[pltpu.VMEM((2, RET_QK, HEAD_W), F32)],
        compiler_params=_params(("parallel", "parallel", "arbitrary")),
        name="retention",
    )(inner, qd, kd, cd, proj3, proj3, proj3, proj3)


def _split3(x):
    hi = x.astype(BF16)
    r1 = x - hi.astype(F32)
    mid = r1.astype(BF16)
    lo = (r1 - mid.astype(F32)).astype(BF16)
    return hi, mid, lo


HG_SUB = 16


def _hgrn2_kernel(lb_ref, gn_ref, q_ref, f_ref, i_ref, gt_ref, o_ref,
                  state_ref, b_sc, k_sc, q_sc, *, rows):
    ci = pl.program_id(2)
    c, sc = HG_CHUNK, HG_SUB

    @pl.when(ci == 0)
    def _():
        state_ref[...] = jnp.zeros(state_ref.shape, F32)

    lb = lb_ref[...]
    gn = gn_ref[...]
    fr = f_ref[...]
    log_f = jnp.log(lb + (1.0 - lb) * jax.nn.sigmoid(fr))
    k_sc[...] = (1.0 - lb) * jax.nn.sigmoid(-fr)
    qraw = q_ref[...]
    q_sc[...] = qraw * jax.nn.sigmoid(qraw)
    ri = lax.broadcasted_iota(jnp.int32, (rows, rows), 0)
    cj = lax.broadcasted_iota(jnp.int32, (rows, rows), 1)
    tri = jnp.where(jnp.logical_and(ri >= cj, (ri // c) == (cj // c)), 1.0, 0.0).astype(BF16)
    hi, mid, lo = _split3(log_f)
    b_sc[...] = (_dot(tri, hi) + _dot(tri, mid) + _dot(tri, lo)) * LOG2E
    hs = sc // 2
    row_half = lax.broadcasted_iota(jnp.int32, (hs, HEAD_W), 0)
    row_chunk = lax.broadcasted_iota(jnp.int32, (c, HEAD_W), 0)

    for n in range(rows // c):
        r0 = n * c
        parts = []
        for sub in range(c // sc):
            s0 = r0 + sub * sc
            b_i = b_sc[s0:s0 + sc, :]
            q_i = q_sc[s0:s0 + sc, :]
            halves = [(b_i[:hs], q_i[:hs]), (b_i[hs:], q_i[hs:])]
            acc_h = [jnp.zeros((hs, HEAD_W), F32), jnp.zeros((hs, HEAD_W), F32)]
            for jj in range(sc):
                j = s0 + jj
                b_j, k_j, v_j = b_sc[j:j + 1, :], k_sc[j:j + 1, :], i_ref[j:j + 1, :]
                for hh, (b_h, q_h) in enumerate(halves):
                    first = jj - hh * hs
                    if first >= hs:
                        continue
                    d = b_h - b_j
                    if first > 0:
                        d = jnp.where(row_half >= first, d, NEG_INF)
                    a = jnp.sum(q_h * k_j * jnp.exp2(d), axis=-1, keepdims=True)
                    acc_h[hh] = acc_h[hh] + a * v_j
            parts.append(jnp.concatenate(acc_h, axis=0))
        o = jnp.concatenate(parts, axis=0)
        bc = b_sc[r0:r0 + c, :]
        qc = q_sc[r0:r0 + c, :]
        kc = k_sc[r0:r0 + c, :]
        vc = i_ref[r0:r0 + c, :].astype(BF16)
        q_segs, k_segs = [], []
        for sub in range(1, c // sc):
            anchor = b_sc[r0 + sub * sc:r0 + sub * sc + 1, :]
            in_sub = jnp.logical_and(row_chunk >= sub * sc, row_chunk < (sub + 1) * sc)
            q_segs.append(qc * jnp.exp2(jnp.where(in_sub, bc - anchor, NEG_INF)))
            k_segs.append(kc * jnp.exp2(jnp.where(row_chunk < sub * sc, anchor - bc, NEG_INF)))
        att = _dot_nt(jnp.concatenate(q_segs, axis=1).astype(BF16),
                      jnp.concatenate(k_segs, axis=1).astype(BF16))
        state_t = state_ref[...]
        o = (o + _dot(att.astype(BF16), vc)
             + _dot_nt((qc * jnp.exp2(bc)).astype(BF16), state_t.astype(BF16)))
        blast = b_sc[r0 + c - 1:r0 + c, :]
        kd = (kc * jnp.exp2(blast - bc)).astype(BF16)
        state_ref[...] = jnp.exp2(blast) * state_t + _dot_tn(vc, kd)
        ms = jnp.mean(o * o, axis=-1, keepdims=True)
        gate = gt_ref[r0:r0 + c, :]
        y = o * lax.rsqrt(ms + EPS) * gn * (gate * jax.nn.sigmoid(gate))
        o_ref[r0:r0 + c, :] = y.astype(o_ref.dtype)


def _hgrn2(proj3, lb, gn, *, rows=256):
    b, s, _ = proj3.shape
    rows = _pick(s, rows)
    blk = lambda off: pl.BlockSpec((None, rows, HEAD_W), lambda bi, h, ci: (bi, ci, off + h))
    return pl.pallas_call(
        functools.partial(_hgrn2_kernel, rows=rows),
        grid=(b, N_HEADS, s // rows),
        in_specs=[
            pl.BlockSpec((None, 1, HEAD_W), lambda bi, h, ci: (h, 0, 0)),
            pl.BlockSpec((1, HEAD_W), lambda bi, h, ci: (0, 0)),
            blk(OFF_HG_Q), blk(OFF_HG_F), blk(OFF_HG_I), blk(OFF_HG_G),
        ],
        out_specs=pl.BlockSpec((None, rows, HEAD_W), lambda bi, h, ci: (bi, ci, h)),
        out_shape=jax.ShapeDtypeStruct((b, s, BRANCH_W), BF16),
        scratch_shapes=[pltpu.VMEM((HEAD_W, HEAD_W), F32)] + [pltpu.VMEM((rows, HEAD_W), F32)] * 3,
        compiler_params=_params(("parallel", "parallel", "arbitrary")),
        name="hgrn2",
    )(lb.reshape(N_HEADS, 1, HEAD_W), gn.reshape(1, HEAD_W), proj3, proj3, proj3, proj3)


def _dil_kernel(slope_ref, *refs, tile):
    n_g = len(DIL_RATES)
    groups = [refs[5 * g:5 * g + 5] for g in range(n_g)]
    o_ref, o_sc, l_sc = refs[5 * n_g:]
    h = pl.program_id(1)
    blk = pl.program_id(2)
    t = DIL_BLOCK
    d_cur = (lax.broadcasted_iota(jnp.int32, (t, t), 0)
             - lax.broadcasted_iota(jnp.int32, (t, t), 1)).astype(F32)
    d_prev = d_cur + float(t)

    for g, (q_ref, k_ref, v_ref, kh_ref, vh_ref) in enumerate(groups):
        rate = DIL_RATES[g]
        halo = t * rate
        slope = slope_ref[g * N_HEADS + h] * (float(rate) * LOG2E)
        bias_cur = jnp.where(d_cur >= 0, -slope * d_cur, NEG_INF)
        bias_prev = jnp.where(d_prev <= float(t), -slope * d_prev, NEG_INF)
        bias_both = jnp.concatenate([bias_prev, bias_cur], axis=1)
        bias_first = jnp.concatenate([jnp.full((t, t), NEG_INF, F32), bias_cur], axis=1)

        def rows(start, rate=rate):
            return pl.ds(start, t) if rate == 1 else pl.ds(start, t, stride=rate)

        def units(ug, carry, g=g, rate=rate, halo=halo, q_ref=q_ref, k_ref=k_ref, v_ref=v_ref,
                  kh_ref=kh_ref, vh_ref=vh_ref, bias_both=bias_both, bias_first=bias_first, rows=rows):
            starts, qs, ks, vs, biases = [], [], [], [], []
            for uu in range(DIL_UNROLL):
                u = ug * DIL_UNROLL + uu
                r = u % rate
                mb = u // rate
                start = mb * halo + r
                starts.append(start)
                qs.append((q_ref[rows(start), :] * (HEAD_W ** -0.5 * LOG2E)).astype(BF16))
                kp = kh_ref[rows(r), :]
                vp = vh_ref[rows(r), :]
                prev_ok = blk > 0
                if tile > halo:
                    pstart = jnp.maximum(start - halo, r)
                    in_tile = mb > 0
                    kp = jnp.where(in_tile, k_ref[rows(pstart), :], kp)
                    vp = jnp.where(in_tile, v_ref[rows(pstart), :], vp)
                    prev_ok = jnp.logical_or(prev_ok, in_tile)
                ks.append(jnp.concatenate([kp, k_ref[rows(start), :]], axis=0).astype(BF16))
                vs.append(jnp.concatenate([vp, v_ref[rows(start), :]], axis=0).astype(BF16))
                biases.append(jnp.where(prev_ok, bias_both, bias_first))
            ss = [_dot_nt(q, k) + bias for q, k, bias in zip(qs, ks, biases)]
            ms = [jnp.max(s, axis=-1, keepdims=True) for s in ss]
            ps = [jnp.exp2(s - m) for s, m in zip(ss, ms)]
            ls = [jnp.sum(p, axis=-1, keepdims=True) for p in ps]
            accs = [_dot(p.astype(BF16), v) for p, v in zip(ps, vs)]
            for start, m, l, acc in zip(starts, ms, ls, accs):
                o = acc / l
                lse = jnp.broadcast_to(m + jnp.log2(l), (t, HEAD_W))
                if g > 0:
                    lse_p = l_sc[rows(start), :]
                    gap = lse_p - lse
                    w_p = 1.0 / (1.0 + jnp.exp2(-gap))
                    o = o + w_p * (o_sc[rows(start), :] - o)
                    if g < n_g - 1:
                        lse = jnp.maximum(lse, lse_p) + jnp.log2(1.0 + jnp.exp2(-jnp.abs(gap)))
                o_sc[rows(start), :] = o
                if g < n_g - 1:
                    l_sc[rows(start), :] = lse
            return carry

        lax.fori_loop(0, tile // t // DIL_UNROLL, units, 0)

    o_ref[...] = o_sc[...].astype(o_ref.dtype)


def _dilated_attention(proj3, slopes4):
    b, s, _ = proj3.shape
    t = DIL_BLOCK
    tile = t * max(DIL_RATES)
    assert s % tile == 0
    in_specs, args = [], []
    for g, rate in enumerate(DIL_RATES):
        halo = t * rate
        per = tile // halo

        def cur(off, g=g):
            return pl.BlockSpec((None, tile, HEAD_W),
                                lambda bi, h, blk, sl: (bi, blk, off + g * N_HEADS + h))

        def front(off, g=g, halo=halo, per=per):
            return pl.BlockSpec((None, halo, HEAD_W),
                                lambda bi, h, blk, sl: (bi, jnp.maximum(blk * per - 1, 0), off + g * N_HEADS + h))

        in_specs += [cur(OFF_DL_Q), cur(OFF_DL_K), cur(OFF_DL_V), front(OFF_DL_K), front(OFF_DL_V)]
        args += [proj3] * 5
    grid_spec = pltpu.PrefetchScalarGridSpec(
        num_scalar_prefetch=1,
        grid=(b, N_HEADS, s // tile),
        in_specs=in_specs,
        out_specs=pl.BlockSpec((None, tile, HEAD_W), lambda bi, h, blk, sl: (bi, blk, h)),
        scratch_shapes=[pltpu.VMEM((tile, HEAD_W), F32), pltpu.VMEM((tile, HEAD_W), F32)],
    )
    return pl.pallas_call(
        functools.partial(_dil_kernel, tile=tile),
        grid_spec=grid_spec,
        out_shape=jax.ShapeDtypeStruct((b, s, BRANCH_W), BF16),
        compiler_params=_params(("parallel", "parallel", "parallel")),
        name="dilated",
    )(slopes4[:, 1:].T.reshape(-1), *args)


def _merge_kernel(ya_ref, yr_ref, yc_ref, yd_ref, w_ref, g0_ref, g1_ref, g2_ref, g3_ref, o_ref, wb_ref):
    @pl.when(pl.program_id(1) == 0)
    def _():
        _cast_weights(w_ref, wb_ref)

    acc = None
    for n, (y_ref, g_ref) in enumerate(((ya_ref, g0_ref), (yr_ref, g1_ref), (yc_ref, g2_ref), (yd_ref, g3_ref))):
        t = jax.nn.sigmoid(g_ref[...]) * _dot(y_ref[...], wb_ref[n])
        acc = t if acc is None else acc + t
    o_ref[...] = acc.astype(o_ref.dtype)


def _merge(ys, w_branch, layer, proj, d_model, *, tm=512, tn=1024):
    m = proj.shape[0]
    tm, tn = _pick(m, tm), _pick(d_model, tn)
    gcols = d_model // tn
    goff = OFF_GATES * 128 // tn
    y_spec = pl.BlockSpec((tm, BRANCH_W), lambda j, i: (i, 0))

    def gate_spec(n):
        return pl.BlockSpec((tm, tn), lambda j, i: (i, goff + n * gcols + j))

    return pl.pallas_call(
        _merge_kernel,
        grid=(d_model // tn, m // tm),
        in_specs=[y_spec, y_spec, y_spec, y_spec,
                  pl.BlockSpec((None, N_BRANCH, BRANCH_W, tn), lambda j, i: (layer, 0, 0, j),
                               pipeline_mode=pl.Buffered(1)),
                  gate_spec(0), gate_spec(1), gate_spec(2), gate_spec(3)],
        out_specs=pl.BlockSpec((tm, tn), lambda j, i: (i, j)),
        out_shape=jax.ShapeDtypeStruct((m, d_model), BF16),
        scratch_shapes=[pltpu.VMEM((N_BRANCH, BRANCH_W, tn), BF16)],
        compiler_params=_params(("parallel", "arbitrary")),
        name="merge",
    )(*ys, w_branch, proj, proj, proj, proj)


FFN_SUB_ROWS = 512


def _ffn_up_kernel(x_ref, w_hbm, cw_ref, cb_ref, o_ref, sa_ref, sg_ref, wab_ref, wgb_ref, sems,
                   carry_ref, *, tiles_per_seq, layer, n_blocks):
    i = pl.program_id(1)
    tn = o_ref.shape[1]
    _stage_weights(w_hbm, layer, 0, n_blocks, sa_ref, wab_ref, sems.at[0])
    _stage_weights(w_hbm, layer, n_blocks * tn, n_blocks, sg_ref, wgb_ref, sems.at[1])

    @pl.when(i % tiles_per_seq == 0)
    def _():
        carry_ref[...] = jnp.zeros(carry_ref.shape, F32)

    tm, tn = o_ref.shape
    sub = min(tm, FFN_SUB_ROWS)
    cw = cw_ref[...]
    cb = cb_ref[...]
    row = lax.broadcasted_iota(jnp.int32, (sub, tn), 0)
    prev = carry_ref[...]
    for r0 in range(0, tm, sub):
        x = x_ref[r0:r0 + sub, :]
        a = _dot(x, wab_ref[...])
        gl = _dot(x, wgb_ref[...])
        a1 = jnp.where(row == 0, prev[7:8, :], pltpu.roll(a, 1, 0))
        a2 = jnp.where(row == 0, prev[6:7, :],
                       jnp.where(row == 1, prev[7:8, :], pltpu.roll(a, 2, 0)))
        conv = cb + cw[0:1, :] * a2 + cw[1:2, :] * a1 + cw[2:3, :] * a
        gelu = 0.5 * conv * (1.0 + lax.erf(conv * (2.0 ** -0.5)))
        o_ref[r0:r0 + sub, :] = (gelu * gl).astype(o_ref.dtype)
        prev = a[sub - 8:sub, :]
    carry_ref[...] = prev


def _ffn_up(x, w_up, layer, conv_w, conv_b, seq, *, tm=2048, tn=256):
    m, kdim = x.shape
    d_ff = w_up.shape[-1] // 2
    tm, tn = _pick(seq, tm), _pick(d_ff, tn)
    nj = d_ff // tn
    return pl.pallas_call(
        functools.partial(_ffn_up_kernel, tiles_per_seq=seq // tm, layer=layer, n_blocks=nj),
        grid=(nj, m // tm),
        in_specs=[
            pl.BlockSpec((tm, kdim), lambda j, i: (i, 0)),
            pl.BlockSpec(memory_space=pl.ANY),
            pl.BlockSpec((CONV_W, tn), lambda j, i: (0, j)),
            pl.BlockSpec((1, tn), lambda j, i: (0, j)),
        ],
        out_specs=pl.BlockSpec((tm, tn), lambda j, i: (i, j)),
        out_shape=jax.ShapeDtypeStruct((m, d_ff), BF16),
        scratch_shapes=[pltpu.VMEM((kdim, tn), F32), pltpu.VMEM((kdim, tn), F32),
                        pltpu.VMEM((kdim, tn), BF16), pltpu.VMEM((kdim, tn), BF16),
                        pltpu.SemaphoreType.DMA((2,)), pltpu.VMEM((8, tn), F32)],
        compiler_params=_params(("arbitrary", "arbitrary")),
        name="ffn_up",
    )(x, w_up, conv_w, conv_b.reshape(1, d_ff))


def _alibi_slopes():
    n = 4 * N_HEADS
    slopes = 2.0 ** (-8.0 * jnp.arange(1, n + 1, dtype=F32) / n)
    return slopes.reshape(N_HEADS, 4)


def kernel(x, w_in, diff_lambda, diff_norm_g, hgrn_lb_logits, hgrn_norm_g, w_branch, w_o,
           ln1_g, ln1_b, w_up, conv_w, conv_b, w_down, ln2_g, ln2_b):
    b, s, d = x.shape
    depth = w_in.shape[0]
    m = b * s
    alpha = (2 * depth) ** 0.25
    slopes4 = _alibi_slopes()
    p = jax.nn.softmax(hgrn_lb_logits.astype(F32), axis=0)
    lower_bounds = jnp.cumsum(p, axis=0) - p[0]

    w_down_b = w_down.astype(BF16)
    res = x.reshape(m, d)
    hb = res.astype(BF16)
    for l in range(depth):
        lam_init = 0.8 - 0.6 * math.exp(-0.3 * l)
        proj = _matmul(hb, w_in, l, out_dtype=F32, tm=1024, name="in_proj")
        proj3 = proj.reshape(b, s, -1)
        ya = _diff_attention(proj3, slopes4[:, 0], diff_lambda[l], diff_norm_g[l], lam_init)
        yr = _retention(proj3)
        yc = _hgrn2(proj3, lower_bounds[l], hgrn_norm_g[l])
        yd = _dilated_attention(proj3, slopes4)
        ys = [y.reshape(m, BRANCH_W) for y in (ya, yr, yc, yd)]
        merged = _merge(ys, w_branch, l, proj, d)
        z = _matmul(merged, w_o, l, out_dtype=F32, res=res, alpha=alpha, name="out_proj")
        hb, mean, rstd = _layernorm(z, ln1_g[l], ln1_b[l], final=False)
        res = (z, mean, rstd, ln1_g[l], ln1_b[l])
        act = _ffn_up(hb, w_up, l, conv_w[l], conv_b[l], s)
        z = _matmul(act, w_down_b, l, out_dtype=F32, tn=512, res=res, alpha=alpha, name="ffn_down")
        if l == depth - 1:
            (out,) = _layernorm(z, ln2_g[l], ln2_b[l], final=True)
            return out.reshape(b, s, d)
        hb, mean, rstd = _layernorm(z, ln2_g[l], ln2_b[l], final=False)
        res = (z, mean, rstd, ln2_g[l], ln2_b[l])
```

```python
import functools
import math

import jax
import jax.numpy as jnp
from jax import lax
from jax.experimental import pallas as pl
from jax.experimental.pallas import tpu as pltpu

F32 = jnp.float32
BF16 = jnp.bfloat16

N_HEADS = 8
HEAD_W = 128
BRANCH_W = N_HEADS * HEAD_W
N_BRANCH = 4
DA_QK = 64
RET_QK = 64
RET_CHUNK = 256
RET_DECAY_BASE = 5.0
HG_CHUNK = 64
DIL_RATES = (1, 4, 16)
DIL_BLOCK = 128
DIL_UNROLL = 8
CONV_W = 3
EPS = 1e-5
NEG_INF = float("-inf")

_C = BRANCH_W // 128
OFF_DA_Q, OFF_DA_K, OFF_DA_V = 0, _C, 2 * _C
OFF_RT_Q, OFF_RT_K = 3 * _C, 3 * _C + _C // 2
OFF_RT_V, OFF_RT_G = 4 * _C, 5 * _C
OFF_HG_Q, OFF_HG_F, OFF_HG_I, OFF_HG_G = 6 * _C, 7 * _C, 8 * _C, 9 * _C
OFF_DL_Q, OFF_DL_K, OFF_DL_V = 10 * _C, 13 * _C, 16 * _C
OFF_GATES = 19 * _C

VMEM_LIMIT = 56 * 1024 * 1024


def _params(sem):
    return pltpu.CompilerParams(dimension_semantics=sem, vmem_limit_bytes=VMEM_LIMIT)


def _dot(a, b):
    return jnp.dot(a, b, preferred_element_type=F32)


def _dot_nt(a, b):
    return lax.dot_general(a, b, (((1,), (1,)), ((), ())), preferred_element_type=F32)


def _dot_tn(a, b):
    return lax.dot_general(a, b, (((0,), (0,)), ((), ())), preferred_element_type=F32)


def _pick(dim, pref):
    if dim <= pref:
        return dim
    t = pref
    while t >= 128:
        if dim % t == 0 and t % 128 == 0:
            return t
        t -= 128
    return dim


CAST_ROWS = 512


def _cast_weights(w_ref, wb_ref):
    rows = w_ref.shape[-2]
    step = min(rows, CAST_ROWS)
    for r0 in range(0, rows, step):
        r1 = min(rows, r0 + step)
        wb_ref[..., r0:r1, :] = w_ref[..., r0:r1, :].astype(BF16)


def _stage_weights(w_hbm, layer, first_col, n_blocks, stage_ref, wb_ref, sem):
    j = pl.program_id(0)
    tn = stage_ref.shape[-1]

    def copy(jj):
        col = pl.multiple_of(first_col + jj * tn, tn)
        return pltpu.make_async_copy(w_hbm.at[layer, :, pl.ds(col, tn)], stage_ref, sem)

    @pl.when(pl.program_id(1) == 0)
    def _():
        @pl.when(j == 0)
        def _():
            copy(0).start()

        copy(j).wait()
        _cast_weights(stage_ref, wb_ref)

        @pl.when(j + 1 < n_blocks)
        def _():
            copy(j + 1).start()


def _mm_kernel(*refs, alpha, n_res, layer, n_blocks):
    x_ref, w_ref = refs[:2]
    res_refs = refs[2:2 + n_res]
    o_ref = refs[2 + n_res]
    scratch = refs[3 + n_res:]

    if scratch:
        stage_ref, wb_ref, sem = scratch
        _stage_weights(w_ref, layer, 0, n_blocks, stage_ref, wb_ref, sem)
    else:
        wb_ref = w_ref

    acc = _dot(x_ref[...], wb_ref[...])
    if n_res == 1:
        acc = alpha * res_refs[0][...] + acc
    elif n_res:
        z_ref, mu_ref, rs_ref, g_ref, b_ref = res_refs
        reps = (1, o_ref.shape[1] // HEAD_W)
        h = (z_ref[...] - jnp.tile(mu_ref[...], reps)) * jnp.tile(rs_ref[...], reps) * g_ref[...] + b_ref[...]
        acc = alpha * h + acc
    o_ref[...] = acc.astype(o_ref.dtype)


def _matmul(x, w, layer, *, out_dtype, tm=512, tn=1024, res=None, alpha=1.0, name="mm"):
    m, kdim = x.shape
    n = w.shape[-1]
    tm, tn = _pick(m, tm), _pick(n, tn)
    if w.dtype == BF16:
        w_spec = pl.BlockSpec((None, kdim, tn), lambda j, i: (layer, 0, j), pipeline_mode=pl.Buffered(1))
        scratch = []
    else:
        w_spec = pl.BlockSpec(memory_space=pl.ANY)
        scratch = [pltpu.VMEM((kdim, tn), F32), pltpu.VMEM((kdim, tn), BF16), pltpu.SemaphoreType.DMA(())]
    in_specs = [pl.BlockSpec((tm, kdim), lambda j, i: (i, 0)), w_spec]
    args = [x, w]
    n_res = 0
    if res is not None:
        tile = pl.BlockSpec((tm, tn), lambda j, i: (i, j))
        if isinstance(res, tuple):
            z, mean, rstd, gain, bias = res
            stat = pl.BlockSpec((tm, HEAD_W), lambda j, i: (i, 0))
            vec = pl.BlockSpec((1, tn), lambda j, i: (0, j))
            in_specs += [tile, stat, stat, vec, vec]
            args += [z, mean, rstd, gain.reshape(1, n), bias.reshape(1, n)]
        else:
            in_specs.append(tile)
            args.append(res)
        n_res = len(args) - 2
    return pl.pallas_call(
        functools.partial(_mm_kernel, alpha=alpha, n_res=n_res, layer=layer, n_blocks=n // tn),
        grid=(n // tn, m // tm),
        in_specs=in_specs,
        out_specs=pl.BlockSpec((tm, tn), lambda j, i: (i, j)),
        out_shape=jax.ShapeDtypeStruct((m, n), out_dtype),
        scratch_shapes=scratch,
        compiler_params=_params(("arbitrary", "arbitrary")),
        name=name,
    )(*args)


LN_STRIP = 8


def _ln_kernel(x_ref, g_ref, b_ref, *out_refs, final):
    g = g_ref[...]
    b = b_ref[...]
    strip = min(LN_STRIP, x_ref.shape[0])
    for r0 in range(0, x_ref.shape[0], strip):
        rows = slice(r0, r0 + strip)
        x = x_ref[rows, :]
        mu = jnp.mean(x, axis=-1, keepdims=True)
        xc = x - mu
        rstd = lax.rsqrt(jnp.mean(xc * xc, axis=-1, keepdims=True) + EPS)
        y = xc * rstd * g + b
        if final:
            out_refs[0][rows, :] = y
        else:
            ob_ref, mu_ref, rs_ref = out_refs
            ob_ref[rows, :] = y.astype(BF16)
            mu_ref[rows, :] = jnp.broadcast_to(mu, (strip, HEAD_W))
            rs_ref[rows, :] = jnp.broadcast_to(rstd, (strip, HEAD_W))


def _layernorm(x, g, b, *, final, tm=256):
    m, d = x.shape
    tm = _pick(m, tm)
    row = pl.BlockSpec((tm, d), lambda i: (i, 0))
    stat = pl.BlockSpec((tm, HEAD_W), lambda i: (i, 0))
    if final:
        out_specs, out_shape = [row], [jax.ShapeDtypeStruct((m, d), F32)]
    else:
        out_specs = [row, stat, stat]
        out_shape = [jax.ShapeDtypeStruct((m, d), BF16)] + [jax.ShapeDtypeStruct((m, HEAD_W), F32)] * 2
    return pl.pallas_call(
        functools.partial(_ln_kernel, final=final),
        grid=(m // tm,),
        in_specs=[row, pl.BlockSpec((1, d), lambda i: (0, 0)), pl.BlockSpec((1, d), lambda i: (0, 0))],
        out_specs=out_specs,
        out_shape=out_shape,
        compiler_params=_params(("parallel",)),
        name="layernorm",
    )(x, g.reshape(1, d), b.reshape(1, d))


LOG2E = 1.4426950408889634


def _split3_f32(x):
    hi = x.astype(BF16).astype(F32)
    mid = (x - hi).astype(BF16).astype(F32)
    lo = (x - hi - mid).astype(BF16).astype(F32)
    return hi, mid, lo


def _lane_fill(lane, first, vals, other):
    out = other
    for n, val in enumerate(vals):
        out = jnp.where(lane == first + n, val, out)
    return out


def _diff_attn_kernel(slope_ref, lam_ref, g_ref, q_ref, k_ref, v_ref, o_ref,
                      m1_ref, a1_ref, m2_ref, a2_ref, kaug_sc, mask_sc, sa_sc, sb_sc, kk_sc, vv_sc, *, t, lam_init):
    h = pl.program_id(1)
    qi = pl.program_id(2)
    slope2 = slope_ref[h] * LOG2E
    lane = lax.broadcasted_iota(jnp.int32, (t, HEAD_W), 1)
    aug_first = (DA_QK, 0)
    owns = (lane < DA_QK, lane >= DA_QK)
    accs = ((m1_ref, a1_ref), (m2_ref, a2_ref))

    @pl.when(qi == 0)
    def _():
        j = lax.broadcasted_iota(jnp.int32, (t, HEAD_W), 0).astype(F32)
        parts = _split3_f32(slope2 * j)
        for idx in range(2):
            kaug_sc[idx] = _lane_fill(lane, aug_first[idx], list(parts) + [1.0, 1.0, 1.0], 0.0)
        d = (lax.broadcasted_iota(jnp.int32, (t, t), 0) - lax.broadcasted_iota(jnp.int32, (t, t), 1))
        mask_sc[...] = jnp.where(d >= 0, 0.0, NEG_INF)
        ones0 = jnp.where(lane == 0, 1.0, 0.0).astype(BF16)

        def prep(kb, carry):
            r0 = pl.multiple_of(kb * t, t)
            k = k_ref[pl.ds(r0, t), :]
            for idx in range(2):
                kk_sc[idx, pl.ds(r0, t), :] = jnp.where(owns[idx], k, kaug_sc[idx]).astype(BF16)
            vv_sc[pl.ds(r0, t), :] = jnp.concatenate([v_ref[pl.ds(r0, t), :].astype(BF16), ones0], axis=1)
            return carry

        lax.fori_loop(0, k_ref.shape[0] // t, prep, 0)

    for m_ref, a_ref in accs:
        m_ref[...] = jnp.full(m_ref.shape, NEG_INF, F32)
        a_ref[...] = jnp.zeros(a_ref.shape, F32)

    q = q_ref[...] * (DA_QK ** -0.5 * LOG2E)

    def scores(ki, s_ref):
        r0 = pl.multiple_of(ki * t, t)
        block_c = jnp.full((1, HEAD_W), -slope2 * ((qi - ki) * t).astype(F32), F32)
        c_parts = _split3_f32(block_c)
        for idx in range(2):
            q_aug = _lane_fill(lane[0:1], aug_first[idx], [1.0, 1.0, 1.0] + list(c_parts), 0.0)
            qq = jnp.where(owns[idx], q, q_aug).astype(BF16)
            s_ref[idx] = _dot_nt(qq, kk_sc[idx, pl.ds(r0, t), :])

    def update(ki, s_ref, masked):
        r0 = pl.multiple_of(ki * t, t)
        v_aug = vv_sc[pl.ds(r0, t), :]
        for idx, (m_ref, a_ref) in enumerate(accs):
            s = s_ref[idx]
            if masked:
                s = s + mask_sc[...]
            m_prev = m_ref[...]
            m_new = jnp.maximum(m_prev, jnp.max(s, axis=-1, keepdims=True))
            corr = jnp.exp2(m_prev - m_new)
            p = jnp.exp2(s - jnp.tile(m_new, (1, t // HEAD_W)))
            a_ref[...] = jnp.tile(corr, (1, 2)) * a_ref[...] + _dot(p.astype(BF16), v_aug)
            m_ref[...] = m_new

    scores(0, sa_sc)

    def body(pair, carry):
        k0 = 2 * pair
        scores(k0 + 1, sb_sc)
        update(k0, sa_sc, False)
        scores(k0 + 2, sa_sc)
        update(k0 + 1, sb_sc, False)
        return carry

    lax.fori_loop(0, qi // 2, body, 0)

    @pl.when(qi % 2 == 0)
    def _():
        update(qi, sa_sc, True)

    @pl.when(qi % 2 == 1)
    def _():
        scores(qi, sb_sc)
        update(qi - 1, sa_sc, False)
        update(qi, sb_sc, True)

    lv = lam_ref[...]
    lam = (jnp.exp(jnp.sum(lv[0:1] * lv[1:2], axis=-1, keepdims=True))
           - jnp.exp(jnp.sum(lv[2:3] * lv[3:4], axis=-1, keepdims=True)) + lam_init)
    o1 = a1_ref[:, :HEAD_W] / a1_ref[:, HEAD_W:HEAD_W + 1]
    o2 = a2_ref[:, :HEAD_W] / a2_ref[:, HEAD_W:HEAD_W + 1]
    o = o1 - lam * o2
    ms = jnp.mean(o * o, axis=-1, keepdims=True)
    o = o * lax.rsqrt(ms + EPS) * g_ref[...] * (1.0 - lam_init)
    o_ref[...] = o.astype(o_ref.dtype)


def _diff_attention(proj3, slopes, lam_vec, g, lam_init, *, t=512):
    b, s, _ = proj3.shape
    t = _pick(s, t)
    kern = functools.partial(_diff_attn_kernel, t=t, lam_init=lam_init)
    grid_spec = pltpu.PrefetchScalarGridSpec(
        num_scalar_prefetch=1,
        grid=(b, N_HEADS, s // t),
        in_specs=[
            pl.BlockSpec(lam_vec.shape, lambda bi, h, qi, sl: (0, 0)),
            pl.BlockSpec((1, HEAD_W), lambda bi, h, qi, sl: (0, 0)),
            pl.BlockSpec((None, t, HEAD_W), lambda bi, h, qi, sl: (bi, qi, OFF_DA_Q + h)),
            pl.BlockSpec((None, s, HEAD_W), lambda bi, h, qi, sl: (bi, 0, OFF_DA_K + h)),
            pl.BlockSpec((None, s, HEAD_W), lambda bi, h, qi, sl: (bi, 0, OFF_DA_V + h)),
        ],
        out_specs=pl.BlockSpec((None, t, HEAD_W), lambda bi, h, qi, sl: (bi, qi, h)),
        scratch_shapes=[pltpu.VMEM((t, HEAD_W), F32), pltpu.VMEM((t, 2 * HEAD_W), F32)] * 2
        + [pltpu.VMEM((2, t, HEAD_W), F32), pltpu.VMEM((t, t), F32)]
        + [pltpu.VMEM((2, t, t), F32)] * 2
        + [pltpu.VMEM((2, s, HEAD_W), BF16), pltpu.VMEM((s, 2 * HEAD_W), BF16)],
    )
    return pl.pallas_call(
        kern,
        grid_spec=grid_spec,
        out_shape=jax.ShapeDtypeStruct((b, s, BRANCH_W), BF16),
        compiler_params=_params(("parallel", "parallel", "arbitrary")),
        name="diff_attention",
    )(slopes, lam_vec, g.reshape(1, HEAD_W), proj3, proj3, proj3)


def _retention_kernel(inner_ref, qd_ref, kd_ref, cd_ref, q_ref, k_ref, v_ref, g_ref, o_ref,
                      state_ref, *, rows):
    ci = pl.program_id(2)

    @pl.when(ci == 0)
    def _():
        state_ref[...] = jnp.zeros(state_ref.shape, F32)

    c = RET_CHUNK
    for p in range(2):
        inner = inner_ref[p]
        qdec = qd_ref[p]
        kdec = kd_ref[p]
        cdec = cd_ref[p]
        state = state_ref[p]
        for n in range(rows // c):
            r0 = n * c
            qc = (q_ref[r0:r0 + c, p * RET_QK:(p + 1) * RET_QK] * (RET_QK ** -0.5)).astype(BF16)
            kf = k_ref[r0:r0 + c, p * RET_QK:(p + 1) * RET_QK]
            vc = v_ref[r0:r0 + c, p * HEAD_W:(p + 1) * HEAD_W].astype(BF16)
            att = _dot_nt(qc, kf.astype(BF16)) * inner
            o = _dot(att.astype(BF16), vc) + _dot(qc, state.astype(BF16)) * qdec
            state = cdec * state + _dot_tn((kf * kdec).astype(BF16), vc)
            mu = jnp.mean(o, axis=-1, keepdims=True)
            oc = o - mu
            var = jnp.mean(oc * oc, axis=-1, keepdims=True)
            gate = g_ref[r0:r0 + c, p * HEAD_W:(p + 1) * HEAD_W]
            y = oc * lax.rsqrt(var + EPS) * (gate * jax.nn.sigmoid(gate))
            o_ref[r0:r0 + c, p * HEAD_W:(p + 1) * HEAD_W] = y.astype(o_ref.dtype)
        state_ref[p] = state


def _retention_tables():
    h = N_HEADS
    c = RET_CHUNK
    log_g = jnp.log1p(-(2.0 ** (-RET_DECAY_BASE - jnp.arange(h, dtype=F32))))
    idx = jnp.arange(c, dtype=F32)
    rel = idx[:, None] - idx[None, :]
    inner = jnp.where(rel >= 0, jnp.exp(log_g[:, None, None] * jnp.maximum(rel, 0.0)), 0.0)
    qd = jnp.broadcast_to(jnp.exp(log_g[:, None] * (idx + 1.0))[:, :, None], (h, c, HEAD_W))
    kd = jnp.broadcast_to(jnp.exp(log_g[:, None] * (c - 1.0 - idx))[:, :, None], (h, c, RET_QK))
    cd = jnp.broadcast_to(jnp.exp(log_g * c)[:, None, None], (h, RET_QK, HEAD_W))
    return inner, qd, kd, cd


def _retention(proj3, *, rows=512):
    b, s, _ = proj3.shape
    rows = _pick(s, rows)
    inner, qd, kd, cd = _retention_tables()
    c = RET_CHUNK
    return pl.pallas_call(
        functools.partial(_retention_kernel, rows=rows),
        grid=(b, N_HEADS // 2, s // rows),
        in_specs=[
            pl.BlockSpec((2, c, c), lambda bi, hp, ci: (hp, 0, 0)),
            pl.BlockSpec((2, c, HEAD_W), lambda bi, hp, ci: (hp, 0, 0)),
            pl.BlockSpec((2, c, RET_QK), lambda bi, hp, ci: (hp, 0, 0)),
            pl.BlockSpec((2, RET_QK, HEAD_W), lambda bi, hp, ci: (hp, 0, 0)),
            pl.BlockSpec((None, rows, 128), lambda bi, hp, ci: (bi, ci, OFF_RT_Q + hp)),
            pl.BlockSpec((None, rows, 128), lambda bi, hp, ci: (bi, ci, OFF_RT_K + hp)),
            pl.BlockSpec((None, rows, 2 * HEAD_W), lambda bi, hp, ci: (bi, ci, OFF_RT_V // 2 + hp)),
            pl.BlockSpec((None, rows, 2 * HEAD_W), lambda bi, hp, ci: (bi, ci, OFF_RT_G // 2 + hp)),
        ],
        out_specs=pl.BlockSpec((None, rows, 2 * HEAD_W), lambda bi, hp, ci: (bi, ci, hp)),
        out_shape=jax.ShapeDtypeStruct((b, s, BRANCH_W), BF16),
        scratch_shapes=[pltpu.VMEM((2, RET_QK, HEAD_W), F32)],
        compiler_params=_params(("parallel", "parallel", "arbitrary")),
        name="retention",
    )(inner, qd, kd, cd, proj3, proj3, proj3, proj3)


def _split3(x):
    hi = x.astype(BF16)
    r1 = x - hi.astype(F32)
    mid = r1.astype(BF16)
    lo = (r1 - mid.astype(F32)).astype(BF16)
    return hi, mid, lo


HG_SUB = 16


def _hgrn2_kernel(lb_ref, gn_ref, q_ref, f_ref, i_ref, gt_ref, o_ref,
                  state_ref, b_sc, k_sc, q_sc, *, rows):
    ci = pl.program_id(2)
    c, sc = HG_CHUNK, HG_SUB

    @pl.when(ci == 0)
    def _():
        state_ref[...] = jnp.zeros(state_ref.shape, F32)

    lb = lb_ref[...]
    gn = gn_ref[...]
    fr = f_ref[...]
    log_f = jnp.log(lb + (1.0 - lb) * jax.nn.sigmoid(fr))
    k_sc[...] = (1.0 - lb) * jax.nn.sigmoid(-fr)
    qraw = q_ref[...]
    q_sc[...] = qraw * jax.nn.sigmoid(qraw)
    ri = lax.broadcasted_iota(jnp.int32, (rows, rows), 0)
    cj = lax.broadcasted_iota(jnp.int32, (rows, rows), 1)
    tri = jnp.where(jnp.logical_and(ri >= cj, (ri // c) == (cj // c)), 1.0, 0.0).astype(BF16)
    hi, mid, lo = _split3(log_f)
    b_sc[...] = (_dot(tri, hi) + _dot(tri, mid) + _dot(tri, lo)) * LOG2E
    hs = sc // 2
    row_half = lax.broadcasted_iota(jnp.int32, (hs, HEAD_W), 0)
    row_chunk = lax.broadcasted_iota(jnp.int32, (c, HEAD_W), 0)

    for n in range(rows // c):
        r0 = n * c
        parts = []
        for sub in range(c // sc):
            s0 = r0 + sub * sc
            b_i = b_sc[s0:s0 + sc, :]
            q_i = q_sc[s0:s0 + sc, :]
            halves = [(b_i[:hs], q_i[:hs]), (b_i[hs:], q_i[hs:])]
            acc_h = [jnp.zeros((hs, HEAD_W), F32), jnp.zeros((hs, HEAD_W), F32)]
            for jj in range(sc):
                j = s0 + jj
                b_j, k_j, v_j = b_sc[j:j + 1, :], k_sc[j:j + 1, :], i_ref[j:j + 1, :]
                for hh, (b_h, q_h) in enumerate(halves):
                    first = jj - hh * hs
                    if first >= hs:
                        continue
                    d = b_h - b_j
                    if first > 0:
                        d = jnp.where(row_half >= first, d, NEG_INF)
                    a = jnp.sum(q_h * k_j * jnp.exp2(d), axis=-1, keepdims=True)
                    acc_h[hh] = acc_h[hh] + a * v_j
            parts.append(jnp.concatenate(acc_h, axis=0))
        o = jnp.concatenate(parts, axis=0)
        bc = b_sc[r0:r0 + c, :]
        qc = q_sc[r0:r0 + c, :]
        kc = k_sc[r0:r0 + c, :]
        vc = i_ref[r0:r0 + c, :].astype(BF16)
        q_segs, k_segs = [], []
        for sub in range(1, c // sc):
            anchor = b_sc[r0 + sub * sc:r0 + sub * sc + 1, :]
            in_sub = jnp.logical_and(row_chunk >= sub * sc, row_chunk < (sub + 1) * sc)
            q_segs.append(qc * jnp.exp2(jnp.where(in_sub, bc - anchor, NEG_INF)))
            k_segs.append(kc * jnp.exp2(jnp.where(row_chunk < sub * sc, anchor - bc, NEG_INF)))
        att = _dot_nt(jnp.concatenate(q_segs, axis=1).astype(BF16),
                      jnp.concatenate(k_segs, axis=1).astype(BF16))
        state_t = state_ref[...]
        o = (o + _dot(att.astype(BF16), vc)
             + _dot_nt((qc * jnp.exp2(bc)).astype(BF16), state_t.astype(BF16)))
        blast = b_sc[r0 + c - 1:r0 + c, :]
        kd = (kc * jnp.exp2(blast - bc)).astype(BF16)
        state_ref[...] = jnp.exp2(blast) * state_t + _dot_tn(vc, kd)
        ms = jnp.mean(o * o, axis=-1, keepdims=True)
        gate = gt_ref[r0:r0 + c, :]
        y = o * lax.rsqrt(ms + EPS) * gn * (gate * jax.nn.sigmoid(gate))
        o_ref[r0:r0 + c, :] = y.astype(o_ref.dtype)


def _hgrn2(proj3, lb, gn, *, rows=256):
    b, s, _ = proj3.shape
    rows = _pick(s, rows)
    blk = lambda off: pl.BlockSpec((None, rows, HEAD_W), lambda bi, h, ci: (bi, ci, off + h))
    return pl.pallas_call(
        functools.partial(_hgrn2_kernel, rows=rows),
        grid=(b, N_HEADS, s // rows),
        in_specs=[
            pl.BlockSpec((None, 1, HEAD_W), lambda bi, h, ci: (h, 0, 0)),
            pl.BlockSpec((1, HEAD_W), lambda bi, h, ci: (0, 0)),
            blk(OFF_HG_Q), blk(OFF_HG_F), blk(OFF_HG_I), blk(OFF_HG_G),
        ],
        out_specs=pl.BlockSpec((None, rows, HEAD_W), lambda bi, h, ci: (bi, ci, h)),
        out_shape=jax.ShapeDtypeStruct((b, s, BRANCH_W), BF16),
        scratch_shapes=[pltpu.VMEM((HEAD_W, HEAD_W), F32)] + [pltpu.VMEM((rows, HEAD_W), F32)] * 3,
        compiler_params=_params(("parallel", "parallel", "arbitrary")),
        name="hgrn2",
    )(lb.reshape(N_HEADS, 1, HEAD_W), gn.reshape(1, HEAD_W), proj3, proj3, proj3, proj3)


def _dil_kernel(slope_ref, *refs, tile):
    n_g = len(DIL_RATES)
    groups = [refs[5 * g:5 * g + 5] for g in range(n_g)]
    o_ref, o_sc, l_sc = refs[5 * n_g:]
    h = pl.program_id(1)
    blk = pl.program_id(2)
    t = DIL_BLOCK
    d_cur = (lax.broadcasted_iota(jnp.int32, (t, t), 0)
             - lax.broadcasted_iota(jnp.int32, (t, t), 1)).astype(F32)
    d_prev = d_cur + float(t)

    for g, (q_ref, k_ref, v_ref, kh_ref, vh_ref) in enumerate(groups):
        rate = DIL_RATES[g]
        halo = t * rate
        slope = slope_ref[g * N_HEADS + h] * (float(rate) * LOG2E)
        bias_cur = jnp.where(d_cur >= 0, -slope * d_cur, NEG_INF)
        bias_prev = jnp.where(d_prev <= float(t), -slope * d_prev, NEG_INF)
        bias_both = jnp.concatenate([bias_prev, bias_cur], axis=1)
        bias_first = jnp.concatenate([jnp.full((t, t), NEG_INF, F32), bias_cur], axis=1)

        def rows(start, rate=rate):
            return pl.ds(start, t) if rate == 1 else pl.ds(start, t, stride=rate)

        def units(ug, carry, g=g, rate=rate, halo=halo, q_ref=q_ref, k_ref=k_ref, v_ref=v_ref,
                  kh_ref=kh_ref, vh_ref=vh_ref, bias_both=bias_both, bias_first=bias_first, rows=rows):
            starts, qs, ks, vs, biases = [], [], [], [], []
            for uu in range(DIL_UNROLL):
                u = ug * DIL_UNROLL + uu
                r = u % rate
                mb = u // rate
                start = mb * halo + r
                starts.append(start)
                qs.append((q_ref[rows(start), :] * (HEAD_W ** -0.5 * LOG2E)).astype(BF16))
                kp = kh_ref[rows(r), :]
                vp = vh_ref[rows(r), :]
                prev_ok = blk > 0
                if tile > halo:
                    pstart = jnp.maximum(start - halo, r)
                    in_tile = mb > 0
                    kp = jnp.where(in_tile, k_ref[rows(pstart), :], kp)
                    vp = jnp.where(in_tile, v_ref[rows(pstart), :], vp)
                    prev_ok = jnp.logical_or(prev_ok, in_tile)
                ks.append(jnp.concatenate([kp, k_ref[rows(start), :]], axis=0).astype(BF16))
                vs.append(jnp.concatenate([vp, v_ref[rows(start), :]], axis=0).astype(BF16))
                biases.append(jnp.where(prev_ok, bias_both, bias_first))
            ss = [_dot_nt(q, k) + bias for q, k, bias in zip(qs, ks, biases)]
            ms = [jnp.max(s, axis=-1, keepdims=True) for s in ss]
            ps = [jnp.exp2(s - m) for s, m in zip(ss, ms)]
            ls = [jnp.sum(p, axis=-1, keepdims=True) for p in ps]
            accs = [_dot(p.astype(BF16), v) for p, v in zip(ps, vs)]
            for start, m, l, acc in zip(starts, ms, ls, accs):
                o = acc / l
                lse = jnp.broadcast_to(m + jnp.log2(l), (t, HEAD_W))
                if g > 0:
                    lse_p = l_sc[rows(start), :]
                    gap = lse_p - lse
                    w_p = 1.0 / (1.0 + jnp.exp2(-gap))
                    o = o + w_p * (o_sc[rows(start), :] - o)
                    if g < n_g - 1:
                        lse = jnp.maximum(lse, lse_p) + jnp.log2(1.0 + jnp.exp2(-jnp.abs(gap)))
                o_sc[rows(start), :] = o
                if g < n_g - 1:
                    l_sc[rows(start), :] = lse
            return carry

        lax.fori_loop(0, tile // t // DIL_UNROLL, units, 0)

    o_ref[...] = o_sc[...].astype(o_ref.dtype)


def _dilated_attention(proj3, slopes4):
    b, s, _ = proj3.shape
    t = DIL_BLOCK
    tile = t * max(DIL_RATES)
    assert s % tile == 0
    in_specs, args = [], []
    for g, rate in enumerate(DIL_RATES):
        halo = t * rate
        per = tile // halo

        def cur(off, g=g):
            return pl.BlockSpec((None, tile, HEAD_W),
                                lambda bi, h, blk, sl: (bi, blk, off + g * N_HEADS + h))

        def front(off, g=g, halo=halo, per=per):
            return pl.BlockSpec((None, halo, HEAD_W),
                                lambda bi, h, blk, sl: (bi, jnp.maximum(blk * per - 1, 0), off + g * N_HEADS + h))

        in_specs += [cur(OFF_DL_Q), cur(OFF_DL_K), cur(OFF_DL_V), front(OFF_DL_K), front(OFF_DL_V)]
        args += [proj3] * 5
    grid_spec = pltpu.PrefetchScalarGridSpec(
        num_scalar_prefetch=1,
        grid=(b, N_HEADS, s // tile),
        in_specs=in_specs,
        out_specs=pl.BlockSpec((None, tile, HEAD_W), lambda bi, h, blk, sl: (bi, blk, h)),
        scratch_shapes=[pltpu.VMEM((tile, HEAD_W), F32), pltpu.VMEM((tile, HEAD_W), F32)],
    )
    return pl.pallas_call(
        functools.partial(_dil_kernel, tile=tile),
        grid_spec=grid_spec,
        out_shape=jax.ShapeDtypeStruct((b, s, BRANCH_W), BF16),
        compiler_params=_params(("parallel", "parallel", "parallel")),
        name="dilated",
    )(slopes4[:, 1:].T.reshape(-1), *args)


def _merge_kernel(ya_ref, yr_ref, yc_ref, yd_ref, w_ref, g0_ref, g1_ref, g2_ref, g3_ref, o_ref, wb_ref):
    @pl.when(pl.program_id(1) == 0)
    def _():
        _cast_weights(w_ref, wb_ref)

    acc = None
    for n, (y_ref, g_ref) in enumerate(((ya_ref, g0_ref), (yr_ref, g1_ref), (yc_ref, g2_ref), (yd_ref, g3_ref))):
        t = jax.nn.sigmoid(g_ref[...]) * _dot(y_ref[...], wb_ref[n])
        acc = t if acc is None else acc + t
    o_ref[...] = acc.astype(o_ref.dtype)


def _merge(ys, w_branch, layer, proj, d_model, *, tm=512, tn=1024):
    m = proj.shape[0]
    tm, tn = _pick(m, tm), _pick(d_model, tn)
    gcols = d_model // tn
    goff = OFF_GATES * 128 // tn
    y_spec = pl.BlockSpec((tm, BRANCH_W), lambda j, i: (i, 0))

    def gate_spec(n):
        return pl.BlockSpec((tm, tn), lambda j, i: (i, goff + n * gcols + j))

    return pl.pallas_call(
        _merge_kernel,
        grid=(d_model // tn, m // tm),
        in_specs=[y_spec, y_spec, y_spec, y_spec,
                  pl.BlockSpec((None, N_BRANCH, BRANCH_W, tn), lambda j, i: (layer, 0, 0, j),
                               pipeline_mode=pl.Buffered(1)),
                  gate_spec(0), gate_spec(1), gate_spec(2), gate_spec(3)],
        out_specs=pl.BlockSpec((tm, tn), lambda j, i: (i, j)),
        out_shape=jax.ShapeDtypeStruct((m, d_model), BF16),
        scratch_shapes=[pltpu.VMEM((N_BRANCH, BRANCH_W, tn), BF16)],
        compiler_params=_params(("parallel", "arbitrary")),
        name="merge",
    )(*ys, w_branch, proj, proj, proj, proj)


FFN_SUB_ROWS = 512


def _ffn_up_kernel(x_ref, w_hbm, cw_ref, cb_ref, o_ref, sa_ref, sg_ref, wab_ref, wgb_ref, sems,
                   carry_ref, *, tiles_per_seq, layer, n_blocks):
    i = pl.program_id(1)
    tn = o_ref.shape[1]
    _stage_weights(w_hbm, layer, 0, n_blocks, sa_ref, wab_ref, sems.at[0])
    _stage_weights(w_hbm, layer, n_blocks * tn, n_blocks, sg_ref, wgb_ref, sems.at[1])

    @pl.when(i % tiles_per_seq == 0)
    def _():
        carry_ref[...] = jnp.zeros(carry_ref.shape, F32)

    tm, tn = o_ref.shape
    sub = min(tm, FFN_SUB_ROWS)
    cw = cw_ref[...]
    cb = cb_ref[...]
    row = lax.broadcasted_iota(jnp.int32, (sub, tn), 0)
    prev = carry_ref[...]
    for r0 in range(0, tm, sub):
        x = x_ref[r0:r0 + sub, :]
        a = _dot(x, wab_ref[...])
        gl = _dot(x, wgb_ref[...])
        a1 = jnp.where(row == 0, prev[7:8, :], pltpu.roll(a, 1, 0))
        a2 = jnp.where(row == 0, prev[6:7, :],
                       jnp.where(row == 1, prev[7:8, :], pltpu.roll(a, 2, 0)))
        conv = cb + cw[0:1, :] * a2 + cw[1:2, :] * a1 + cw[2:3, :] * a
        gelu = 0.5 * conv * (1.0 + lax.erf(conv * (2.0 ** -0.5)))
        o_ref[r0:r0 + sub, :] = (gelu * gl).astype(o_ref.dtype)
        prev = a[sub - 8:sub, :]
    carry_ref[...] = prev


def _ffn_up(x, w_up, layer, conv_w, conv_b, seq, *, tm=2048, tn=256):
    m, kdim = x.shape
    d_ff = w_up.shape[-1] // 2
    tm, tn = _pick(seq, tm), _pick(d_ff, tn)
    nj = d_ff // tn
    return pl.pallas_call(
        functools.partial(_ffn_up_kernel, tiles_per_seq=seq // tm, layer=layer, n_blocks=nj),
        grid=(nj, m // tm),
        in_specs=[
            pl.BlockSpec((tm, kdim), lambda j, i: (i, 0)),
            pl.BlockSpec(memory_space=pl.ANY),
            pl.BlockSpec((CONV_W, tn), lambda j, i: (0, j)),
            pl.BlockSpec((1, tn), lambda j, i: (0, j)),
        ],
        out_specs=pl.BlockSpec((tm, tn), lambda j, i: (i, j)),
        out_shape=jax.ShapeDtypeStruct((m, d_ff), BF16),
        scratch_shapes=[pltpu.VMEM((kdim, tn), F32), pltpu.VMEM((kdim, tn), F32),
                        pltpu.VMEM((kdim, tn), BF16), pltpu.VMEM((kdim, tn), BF16),
                        pltpu.SemaphoreType.DMA((2,)), pltpu.VMEM((8, tn), F32)],
        compiler_params=_params(("arbitrary", "arbitrary")),
        name="ffn_up",
    )(x, w_up, conv_w, conv_b.reshape(1, d_ff))


def _alibi_slopes():
    n = 4 * N_HEADS
    slopes = 2.0 ** (-8.0 * jnp.arange(1, n + 1, dtype=F32) / n)
    return slopes.reshape(N_HEADS, 4)


def kernel(x, w_in, diff_lambda, diff_norm_g, hgrn_lb_logits, hgrn_norm_g, w_branch, w_o,
           ln1_g, ln1_b, w_up, conv_w, conv_b, w_down, ln2_g, ln2_b):
    b, s, d = x.shape
    depth = w_in.shape[0]
    m = b * s
    alpha = (2 * depth) ** 0.25
    slopes4 = _alibi_slopes()
    p = jax.nn.softmax(hgrn_lb_logits.astype(F32), axis=0)
    lower_bounds = jnp.cumsum(p, axis=0) - p[0]

    w_down_b = w_down.astype(BF16)
    res = x.reshape(m, d)
    hb = res.astype(BF16)
    for l in range(depth):
        lam_init = 0.8 - 0.6 * math.exp(-0.3 * l)
        proj = _matmul(hb, w_in, l, out_dtype=F32, tm=1024, name="in_proj")
        proj3 = proj.reshape(b, s, -1)
        ya = _diff_attention(proj3, slopes4[:, 0], diff_lambda[l], diff_norm_g[l], lam_init)
        yr = _retention(proj3)
        yc = _hgrn2(proj3, lower_bounds[l], hgrn_norm_g[l])
        yd = _dilated_attention(proj3, slopes4)
        ys = [y.reshape(m, BRANCH_W) for y in (ya, yr, yc, yd)]
        merged = _merge(ys, w_branch, l, proj, d)
        z = _matmul(merged, w_o, l, out_dtype=F32, res=res, alpha=alpha, name="out_proj")
        hb, mean, rstd = _layernorm(z, ln1_g[l], ln1_b[l], final=False)
        res = (z, mean, rstd, ln1_g[l], ln1_b[l])
        act = _ffn_up(hb, w_up, l, conv_w[l], conv_b[l], s)
        z = _matmul(act, w_down_b, l, out_dtype=F32, tn=512, res=res, alpha=alpha, name="ffn_down")
        if l == depth - 1:
            (out,) = _layernorm(z, ln2_g[l], ln2_b[l], final=True)
            return out.reshape(b, s, d)
        hb, mean, rstd = _layernorm(z, ln2_g[l], ln2_b[l], final=False)
        res = (z, mean, rstd, ln2_g[l], ln2_b[l])
```

```python
import functools
import math

import jax
import jax.numpy as jnp
from jax import lax
from jax.experimental import pallas as pl
from jax.experimental.pallas import tpu as pltpu

F32 = jnp.float32
BF16 = jnp.bfloat16

N_HEADS = 8
HEAD_W = 128
BRANCH_W = N_HEADS * HEAD_W
N_BRANCH = 4
DA_QK = 64
RET_QK = 64
RET_CHUNK = 256
RET_DECAY_BASE = 5.0
HG_CHUNK = 64
DIL_RATES = (1, 4, 16)
DIL_BLOCK = 128
DIL_UNROLL = 16
CONV_W = 3
EPS = 1e-5
NEG_INF = float("-inf")

_C = BRANCH_W // 128
OFF_DA_Q, OFF_DA_K, OFF_DA_V = 0, _C, 2 * _C
OFF_RT_Q, OFF_RT_K = 3 * _C, 3 * _C + _C // 2
OFF_RT_V, OFF_RT_G = 4 * _C, 5 * _C
OFF_HG_Q, OFF_HG_F, OFF_HG_I, OFF_HG_G = 6 * _C, 7 * _C, 8 * _C, 9 * _C
OFF_DL_Q, OFF_DL_K, OFF_DL_V = 10 * _C, 13 * _C, 16 * _C
OFF_GATES = 19 * _C

VMEM_LIMIT = 56 * 1024 * 1024


def _params(sem):
    return pltpu.CompilerParams(dimension_semantics=sem, vmem_limit_bytes=VMEM_LIMIT)


def _dot(a, b):
    return jnp.dot(a, b, preferred_element_type=F32)


def _dot_nt(a, b):
    return lax.dot_general(a, b, (((1,), (1,)), ((), ())), preferred_element_type=F32)


def _dot_tn(a, b):
    return lax.dot_general(a, b, (((0,), (0,)), ((), ())), preferred_element_type=F32)


def _pick(dim, pref):
    if dim <= pref:
        return dim
    t = pref
    while t >= 128:
        if dim % t == 0 and t % 128 == 0:
            return t
        t -= 128
    return dim


CAST_ROWS = 512


def _cast_weights(w_ref, wb_ref):
    rows = w_ref.shape[-2]
    step = min(rows, CAST_ROWS)
    for r0 in range(0, rows, step):
        r1 = min(rows, r0 + step)
        wb_ref[..., r0:r1, :] = w_ref[..., r0:r1, :].astype(BF16)


def _stage_weights(w_hbm, layer, first_col, n_blocks, stage_ref, wb_ref, sem):
    j = pl.program_id(0)
    tn = stage_ref.shape[-1]

    def copy(jj):
        col = pl.multiple_of(first_col + jj * tn, tn)
        return pltpu.make_async_copy(w_hbm.at[layer, :, pl.ds(col, tn)], stage_ref, sem)

    @pl.when(pl.program_id(1) == 0)
    def _():
        @pl.when(j == 0)
        def _():
            copy(0).start()

        copy(j).wait()
        _cast_weights(stage_ref, wb_ref)

        @pl.when(j + 1 < n_blocks)
        def _():
            copy(j + 1).start()


def _mm_kernel(*refs, alpha, n_res, layer, n_blocks):
    x_ref, w_ref = refs[:2]
    res_refs = refs[2:2 + n_res]
    o_ref = refs[2 + n_res]
    scratch = refs[3 + n_res:]

    if scratch:
        stage_ref, wb_ref, sem = scratch
        _stage_weights(w_ref, layer, 0, n_blocks, stage_ref, wb_ref, sem)
    else:
        wb_ref = w_ref

    acc = _dot(x_ref[...], wb_ref[...])
    if n_res == 1:
        acc = alpha * res_refs[0][...] + acc
    elif n_res:
        z_ref, mu_ref, rs_ref, g_ref, b_ref = res_refs
        reps = (1, o_ref.shape[1] // HEAD_W)
        h = (z_ref[...] - jnp.tile(mu_ref[...], reps)) * jnp.tile(rs_ref[...], reps) * g_ref[...] + b_ref[...]
        acc = alpha * h + acc
    o_ref[...] = acc.astype(o_ref.dtype)


def _matmul(x, w, layer, *, out_dtype, tm=512, tn=1024, res=None, alpha=1.0, name="mm"):
    m, kdim = x.shape
    n = w.shape[-1]
    tm, tn = _pick(m, tm), _pick(n, tn)
    if w.dtype == BF16:
        w_spec = pl.BlockSpec((None, kdim, tn), lambda j, i: (layer, 0, j), pipeline_mode=pl.Buffered(1))
        scratch = []
    else:
        w_spec = pl.BlockSpec(memory_space=pl.ANY)
        scratch = [pltpu.VMEM((kdim, tn), F32), pltpu.VMEM((kdim, tn), BF16), pltpu.SemaphoreType.DMA(())]
    in_specs = [pl.BlockSpec((tm, kdim), lambda j, i: (i, 0)), w_spec]
    args = [x, w]
    n_res = 0
    if res is not None:
        tile = pl.BlockSpec((tm, tn), lambda j, i: (i, j))
        if isinstance(res, tuple):
            z, mean, rstd, gain, bias = res
            stat = pl.BlockSpec((tm, HEAD_W), lambda j, i: (i, 0))
            vec = pl.BlockSpec((1, tn), lambda j, i: (0, j))
            in_specs += [tile, stat, stat, vec, vec]
            args += [z, mean, rstd, gain.reshape(1, n), bias.reshape(1, n)]
        else:
            in_specs.append(tile)
            args.append(res)
        n_res = len(args) - 2
    return pl.pallas_call(
        functools.partial(_mm_kernel, alpha=alpha, n_res=n_res, layer=layer, n_blocks=n // tn),
        grid=(n // tn, m // tm),
        in_specs=in_specs,
        out_specs=pl.BlockSpec((tm, tn), lambda j, i: (i, j)),
        out_shape=jax.ShapeDtypeStruct((m, n), out_dtype),
        scratch_shapes=scratch,
        compiler_params=_params(("arbitrary", "arbitrary")),
        name=name,
    )(*args)


LN_STRIP = 8


def _ln_kernel(x_ref, g_ref, b_ref, *out_refs, final):
    g = g_ref[...]
    b = b_ref[...]
    strip = min(LN_STRIP, x_ref.shape[0])
    for r0 in range(0, x_ref.shape[0], strip):
        rows = slice(r0, r0 + strip)
        x = x_ref[rows, :]
        mu = jnp.mean(x, axis=-1, keepdims=True)
        xc = x - mu
        rstd = lax.rsqrt(jnp.mean(xc * xc, axis=-1, keepdims=True) + EPS)
        y = xc * rstd * g + b
        if final:
            out_refs[0][rows, :] = y
        else:
            ob_ref, mu_ref, rs_ref = out_refs
            ob_ref[rows, :] = y.astype(BF16)
            mu_ref[rows, :] = jnp.broadcast_to(mu, (strip, HEAD_W))
            rs_ref[rows, :] = jnp.broadcast_to(rstd, (strip, HEAD_W))


def _layernorm(x, g, b, *, final, tm=256):
    m, d = x.shape
    tm = _pick(m, tm)
    row = pl.BlockSpec((tm, d), lambda i: (i, 0))
    stat = pl.BlockSpec((tm, HEAD_W), lambda i: (i, 0))
    if final:
        out_specs, out_shape = [row], [jax.ShapeDtypeStruct((m, d), F32)]
    else:
        out_specs = [row, stat, stat]
        out_shape = [jax.ShapeDtypeStruct((m, d), BF16)] + [jax.ShapeDtypeStruct((m, HEAD_W), F32)] * 2
    return pl.pallas_call(
        functools.partial(_ln_kernel, final=final),
        grid=(m // tm,),
        in_specs=[row, pl.BlockSpec((1, d), lambda i: (0, 0)), pl.BlockSpec((1, d), lambda i: (0, 0))],
        out_specs=out_specs,
        out_shape=out_shape,
        compiler_params=_params(("parallel",)),
        name="layernorm",
    )(x, g.reshape(1, d), b.reshape(1, d))


LOG2E = 1.4426950408889634


def _split3_f32(x):
    hi = x.astype(BF16).astype(F32)
    mid = (x - hi).astype(BF16).astype(F32)
    lo = (x - hi - mid).astype(BF16).astype(F32)
    return hi, mid, lo


def _lane_fill(lane, first, vals, other):
    out = other
    for n, val in enumerate(vals):
        out = jnp.where(lane == first + n, val, out)
    return out


def _diff_attn_kernel(slope_ref, lam_ref, g_ref, q_ref, k_ref, v_ref, o_ref,
                      m1_ref, a1_ref, m2_ref, a2_ref, kaug_sc, mask_sc, sa_sc, sb_sc, kk_sc, vv_sc, *, t, lam_init):
    h = pl.program_id(1)
    qi = pl.program_id(2)
    slope2 = slope_ref[h] * LOG2E
    lane = lax.broadcasted_iota(jnp.int32, (t, HEAD_W), 1)
    aug_first = (DA_QK, 0)
    owns = (lane < DA_QK, lane >= DA_QK)
    accs = ((m1_ref, a1_ref), (m2_ref, a2_ref))

    @pl.when(qi == 0)
    def _():
        j = lax.broadcasted_iota(jnp.int32, (t, HEAD_W), 0).astype(F32)
        parts = _split3_f32(slope2 * j)
        for idx in range(2):
            kaug_sc[idx] = _lane_fill(lane, aug_first[idx], list(parts) + [1.0, 1.0, 1.0], 0.0)
        d = (lax.broadcasted_iota(jnp.int32, (t, t), 0) - lax.broadcasted_iota(jnp.int32, (t, t), 1))
        mask_sc[...] = jnp.where(d >= 0, 0.0, NEG_INF)
        ones0 = jnp.where(lane == 0, 1.0, 0.0).astype(BF16)

        def prep(kb, carry):
            r0 = pl.multiple_of(kb * t, t)
            k = k_ref[pl.ds(r0, t), :]
            for idx in range(2):
                kk_sc[idx, pl.ds(r0, t), :] = jnp.where(owns[idx], k, kaug_sc[idx]).astype(BF16)
            vv_sc[pl.ds(r0, t), :] = jnp.concatenate([v_ref[pl.ds(r0, t), :].astype(BF16), ones0], axis=1)
            return carry

        lax.fori_loop(0, k_ref.shape[0] // t, prep, 0)

    for m_ref, a_ref in accs:
        m_ref[...] = jnp.full(m_ref.shape, NEG_INF, F32)
        a_ref[...] = jnp.zeros(a_ref.shape, F32)

    q = q_ref[...] * (DA_QK ** -0.5 * LOG2E)

    def scores(ki, s_ref):
        r0 = pl.multiple_of(ki * t, t)
        block_c = jnp.full((1, HEAD_W), -slope2 * ((qi - ki) * t).astype(F32), F32)
        c_parts = _split3_f32(block_c)
        for idx in range(2):
            q_aug = _lane_fill(lane[0:1], aug_first[idx], [1.0, 1.0, 1.0] + list(c_parts), 0.0)
            qq = jnp.where(owns[idx], q, q_aug).astype(BF16)
            s_ref[idx] = _dot_nt(qq, kk_sc[idx, pl.ds(r0, t), :])

    def update(ki, s_ref, masked):
        r0 = pl.multiple_of(ki * t, t)
        v_aug = vv_sc[pl.ds(r0, t), :]
        for idx, (m_ref, a_ref) in enumerate(accs):
            s = s_ref[idx]
            if masked:
                s = s + mask_sc[...]
            m_prev = m_ref[...]
            m_new = jnp.maximum(m_prev, jnp.max(s, axis=-1, keepdims=True))
            corr = jnp.exp2(m_prev - m_new)
            p = jnp.exp2(s - jnp.tile(m_new, (1, t // HEAD_W)))
            a_ref[...] = jnp.tile(corr, (1, 2)) * a_ref[...] + _dot(p.astype(BF16), v_aug)
            m_ref[...] = m_new

    scores(0, sa_sc)

    def body(pair, carry):
        k0 = 2 * pair
        scores(k0 + 1, sb_sc)
        update(k0, sa_sc, False)
        scores(k0 + 2, sa_sc)
        update(k0 + 1, sb_sc, False)
        return carry

    lax.fori_loop(0, qi // 2, body, 0)

    @pl.when(qi % 2 == 0)
    def _():
        update(qi, sa_sc, True)

    @pl.when(qi % 2 == 1)
    def _():
        scores(qi, sb_sc)
        update(qi - 1, sa_sc, False)
        update(qi, sb_sc, True)

    lv = lam_ref[...]
    lam = (jnp.exp(jnp.sum(lv[0:1] * lv[1:2], axis=-1, keepdims=True))
           - jnp.exp(jnp.sum(lv[2:3] * lv[3:4], axis=-1, keepdims=True)) + lam_init)
    o1 = a1_ref[:, :HEAD_W] / a1_ref[:, HEAD_W:HEAD_W + 1]
    o2 = a2_ref[:, :HEAD_W] / a2_ref[:, HEAD_W:HEAD_W + 1]
    o = o1 - lam * o2
    ms = jnp.mean(o * o, axis=-1, keepdims=True)
    o = o * lax.rsqrt(ms + EPS) * g_ref[...] * (1.0 - lam_init)
    o_ref[...] = o.astype(o_ref.dtype)


def _diff_attention(proj3, slopes, lam_vec, g, lam_init, *, t=512):
    b, s, _ = proj3.shape
    t = _pick(s, t)
    kern = functools.partial(_diff_attn_kernel, t=t, lam_init=lam_init)
    grid_spec = pltpu.PrefetchScalarGridSpec(
        num_scalar_prefetch=1,
        grid=(b, N_HEADS, s // t),
        in_specs=[
            pl.BlockSpec(lam_vec.shape, lambda bi, h, qi, sl: (0, 0)),
            pl.BlockSpec((1, HEAD_W), lambda bi, h, qi, sl: (0, 0)),
            pl.BlockSpec((None, t, HEAD_W), lambda bi, h, qi, sl: (bi, qi, OFF_DA_Q + h)),
            pl.BlockSpec((None, s, HEAD_W), lambda bi, h, qi, sl: (bi, 0, OFF_DA_K + h)),
            pl.BlockSpec((None, s, HEAD_W), lambda bi, h, qi, sl: (bi, 0, OFF_DA_V + h)),
        ],
        out_specs=pl.BlockSpec((None, t, HEAD_W), lambda bi, h, qi, sl: (bi, qi, h)),
        scratch_shapes=[pltpu.VMEM((t, HEAD_W), F32), pltpu.VMEM((t, 2 * HEAD_W), F32)] * 2
        + [pltpu.VMEM((2, t, HEAD_W), F32), pltpu.VMEM((t, t), F32)]
        + [pltpu.VMEM((2, t, t), F32)] * 2
        + [pltpu.VMEM((2, s, HEAD_W), BF16), pltpu.VMEM((s, 2 * HEAD_W), BF16)],
    )
    return pl.pallas_call(
        kern,
        grid_spec=grid_spec,
        out_shape=jax.ShapeDtypeStruct((b, s, BRANCH_W), BF16),
        compiler_params=_params(("parallel", "parallel", "arbitrary")),
        name="diff_attention",
    )(slopes, lam_vec, g.reshape(1, HEAD_W), proj3, proj3, proj3)


def _retention_kernel(inner_ref, qd_ref, kd_ref, cd_ref, q_ref, k_ref, v_ref, g_ref, o_ref,
                      state_ref, *, rows):
    ci = pl.program_id(2)

    @pl.when(ci == 0)
    def _():
        state_ref[...] = jnp.zeros(state_ref.shape, F32)

    c = RET_CHUNK
    for p in range(2):
        inner = inner_ref[p]
        qdec = qd_ref[p]
        kdec = kd_ref[p]
        cdec = cd_ref[p]
        state = state_ref[p]
        for n in range(rows // c):
            r0 = n * c
            qc = (q_ref[r0:r0 + c, p * RET_QK:(p + 1) * RET_QK] * (RET_QK ** -0.5)).astype(BF16)
            kf = k_ref[r0:r0 + c, p * RET_QK:(p + 1) * RET_QK]
            vc = v_ref[r0:r0 + c, p * HEAD_W:(p + 1) * HEAD_W].astype(BF16)
            att = _dot_nt(qc, kf.astype(BF16)) * inner
            o = _dot(att.astype(BF16), vc) + _dot(qc, state.astype(BF16)) * qdec
            state = cdec * state + _dot_tn((kf * kdec).astype(BF16), vc)
            mu = jnp.mean(o, axis=-1, keepdims=True)
            oc = o - mu
            var = jnp.mean(oc * oc, axis=-1, keepdims=True)
            gate = g_ref[r0:r0 + c, p * HEAD_W:(p + 1) * HEAD_W]
            y = oc * lax.rsqrt(var + EPS) * (gate * jax.nn.sigmoid(gate))
            o_ref[r0:r0 + c, p * HEAD_W:(p + 1) * HEAD_W] = y.astype(o_ref.dtype)
        state_ref[p] = state


def _retention_tables():
    h = N_HEADS
    c = RET_CHUNK
    log_g = jnp.log1p(-(2.0 ** (-RET_DECAY_BASE - jnp.arange(h, dtype=F32))))
    idx = jnp.arange(c, dtype=F32)
    rel = idx[:, None] - idx[None, :]
    inner = jnp.where(rel >= 0, jnp.exp(log_g[:, None, None] * jnp.maximum(rel, 0.0)), 0.0)
    qd = jnp.broadcast_to(jnp.exp(log_g[:, None] * (idx + 1.0))[:, :, None], (h, c, HEAD_W))
    kd = jnp.broadcast_to(jnp.exp(log_g[:, None] * (c - 1.0 - idx))[:, :, None], (h, c, RET_QK))
    cd = jnp.broadcast_to(jnp.exp(log_g * c)[:, None, None], (h, RET_QK, HEAD_W))
    return inner, qd, kd, cd


def _retention(proj3, *, rows=512):
    b, s, _ = proj3.shape
    rows = _pick(s, rows)
    inner, qd, kd, cd = _retention_tables()
    c = RET_CHUNK
    return pl.pallas_call(
        functools.partial(_retention_kernel, rows=rows),
        grid=(b, N_HEADS // 2, s // rows),
        in_specs=[
            pl.BlockSpec((2, c, c), lambda bi, hp, ci: (hp, 0, 0)),
            pl.BlockSpec((2, c, HEAD_W), lambda bi, hp, ci: (hp, 0, 0)),
            pl.BlockSpec((2, c, RET_QK), lambda bi, hp, ci: (hp, 0, 0)),
            pl.BlockSpec((2, RET_QK, HEAD_W), lambda bi, hp, ci: (hp, 0, 0)),
            pl.BlockSpec((None, rows, 128), lambda bi, hp, ci: (bi, ci, OFF_RT_Q + hp)),
            pl.BlockSpec((None, rows, 128), lambda bi, hp, ci: (bi, ci, OFF_RT_K + hp)),
            pl.BlockSpec((None, rows, 2 * HEAD_W), lambda bi, hp, ci: (bi, ci, OFF_RT_V // 2 + hp)),
            pl.BlockSpec((None, rows, 2 * HEAD_W), lambda bi, hp, ci: (bi, ci, OFF_RT_G // 2 + hp)),
        ],
        out_specs=pl.BlockSpec((None, rows, 2 * HEAD_W), lambda bi, hp, ci: (bi, ci, hp)),
        out_shape=jax.ShapeDtypeStruct((b, s, BRANCH_W), BF16),
        scratch_shapes=[pltpu.VMEM((2, RET_QK, HEAD_W), F32)],
        compiler_params=_params(("parallel", "parallel", "arbitrary")),
        name="retention",
    )(inner, qd, kd, cd, proj3, proj3, proj3, proj3)


def _split3(x):
    hi = x.astype(BF16)
    r1 = x - hi.astype(F32)
    mid = r1.astype(BF16)
    lo = (r1 - mid.astype(F32)).astype(BF16)
    return hi, mid, lo


HG_SUB = 16


def _hgrn2_kernel(lb_ref, gn_ref, q_ref, f_ref, i_ref, gt_ref, o_ref,
                  state_ref, b_sc, k_sc, q_sc, *, rows):
    ci = pl.program_id(2)
    c, sc = HG_CHUNK, HG_SUB

    @pl.when(ci == 0)
    def _():
        state_ref[...] = jnp.zeros(state_ref.shape, F32)

    lb = lb_ref[...]
    gn = gn_ref[...]
    fr = f_ref[...]
    log_f = jnp.log(lb + (1.0 - lb) * jax.nn.sigmoid(fr))
    k_sc[...] = (1.0 - lb) * jax.nn.sigmoid(-fr)
    qraw = q_ref[...]
    q_sc[...] = qraw * jax.nn.sigmoid(qraw)
    ri = lax.broadcasted_iota(jnp.int32, (rows, rows), 0)
    cj = lax.broadcasted_iota(jnp.int32, (rows, rows), 1)
    tri = jnp.where(jnp.logical_and(ri >= cj, (ri // c) == (cj // c)), 1.0, 0.0).astype(BF16)
    hi, mid, lo = _split3(log_f)
    b_sc[...] = (_dot(tri, hi) + _dot(tri, mid) + _dot(tri, lo)) * LOG2E
    hs = sc // 2
    row_half = lax.broadcasted_iota(jnp.int32, (hs, HEAD_W), 0)
    row_chunk = lax.broadcasted_iota(jnp.int32, (c, HEAD_W), 0)

    for n in range(rows // c):
        r0 = n * c
        parts = []
        for sub in range(c // sc):
            s0 = r0 + sub * sc
            b_i = b_sc[s0:s0 + sc, :]
            q_i = q_sc[s0:s0 + sc, :]
            halves = [(b_i[:hs], q_i[:hs]), (b_i[hs:], q_i[hs:])]
            acc_h = [jnp.zeros((hs, HEAD_W), F32), jnp.zeros((hs, HEAD_W), F32)]
            for jj in range(sc):
                j = s0 + jj
                b_j, k_j, v_j = b_sc[j:j + 1, :], k_sc[j:j + 1, :], i_ref[j:j + 1, :]
                for hh, (b_h, q_h) in enumerate(halves):
                    first = jj - hh * hs
                    if first >= hs:
                        continue
                    d = b_h - b_j
                    if first > 0:
                        d = jnp.where(row_half >= first, d, NEG_INF)
                    a = jnp.sum(q_h * k_j * jnp.exp2(d), axis=-1, keepdims=True)
                    acc_h[hh] = acc_h[hh] + a * v_j
            parts.append(jnp.concatenate(acc_h, axis=0))
        o = jnp.concatenate(parts, axis=0)
        bc = b_sc[r0:r0 + c, :]
        qc = q_sc[r0:r0 + c, :]
        kc = k_sc[r0:r0 + c, :]
        vc = i_ref[r0:r0 + c, :].astype(BF16)
        q_segs, k_segs = [], []
        for sub in range(1, c // sc):
            anchor = b_sc[r0 + sub * sc:r0 + sub * sc + 1, :]
            in_sub = jnp.logical_and(row_chunk >= sub * sc, row_chunk < (sub + 1) * sc)
            q_segs.append(qc * jnp.exp2(jnp.where(in_sub, bc - anchor, NEG_INF)))
            k_segs.append(kc * jnp.exp2(jnp.where(row_chunk < sub * sc, anchor - bc, NEG_INF)))
        att = _dot_nt(jnp.concatenate(q_segs, axis=1).astype(BF16),
                      jnp.concatenate(k_segs, axis=1).astype(BF16))
        state_t = state_ref[...]
        o = (o + _dot(att.astype(BF16), vc)
             + _dot_nt((qc * jnp.exp2(bc)).astype(BF16), state_t.astype(BF16)))
        blast = b_sc[r0 + c - 1:r0 + c, :]
        kd = (kc * jnp.exp2(blast - bc)).astype(BF16)
        state_ref[...] = jnp.exp2(blast) * state_t + _dot_tn(vc, kd)
        ms = jnp.mean(o * o, axis=-1, keepdims=True)
        gate = gt_ref[r0:r0 + c, :]
        y = o * lax.rsqrt(ms + EPS) * gn * (gate * jax.nn.sigmoid(gate))
        o_ref[r0:r0 + c, :] = y.astype(o_ref.dtype)


def _hgrn2(proj3, lb, gn, *, rows=256):
    b, s, _ = proj3.shape
    rows = _pick(s, rows)
    blk = lambda off: pl.BlockSpec((None, rows, HEAD_W), lambda bi, h, ci: (bi, ci, off + h))
    return pl.pallas_call(
        functools.partial(_hgrn2_kernel, rows=rows),
        grid=(b, N_HEADS, s // rows),
        in_specs=[
            pl.BlockSpec((None, 1, HEAD_W), lambda bi, h, ci: (h, 0, 0)),
            pl.BlockSpec((1, HEAD_W), lambda bi, h, ci: (0, 0)),
            blk(OFF_HG_Q), blk(OFF_HG_F), blk(OFF_HG_I), blk(OFF_HG_G),
        ],
        out_specs=pl.BlockSpec((None, rows, HEAD_W), lambda bi, h, ci: (bi, ci, h)),
        out_shape=jax.ShapeDtypeStruct((b, s, BRANCH_W), BF16),
        scratch_shapes=[pltpu.VMEM((HEAD_W, HEAD_W), F32)] + [pltpu.VMEM((rows, HEAD_W), F32)] * 3,
        compiler_params=_params(("parallel", "parallel", "arbitrary")),
        name="hgrn2",
    )(lb.reshape(N_HEADS, 1, HEAD_W), gn.reshape(1, HEAD_W), proj3, proj3, proj3, proj3)


def _dil_kernel(slope_ref, *refs, tile):
    n_g = len(DIL_RATES)
    groups = [refs[5 * g:5 * g + 5] for g in range(n_g)]
    o_ref, o_sc, l_sc = refs[5 * n_g:]
    h = pl.program_id(1)
    blk = pl.program_id(2)
    t = DIL_BLOCK
    d_cur = (lax.broadcasted_iota(jnp.int32, (t, t), 0)
             - lax.broadcasted_iota(jnp.int32, (t, t), 1)).astype(F32)
    d_prev = d_cur + float(t)

    for g, (q_ref, k_ref, v_ref, kh_ref, vh_ref) in enumerate(groups):
        rate = DIL_RATES[g]
        halo = t * rate
        slope = slope_ref[g * N_HEADS + h] * (float(rate) * LOG2E)
        bias_cur = jnp.where(d_cur >= 0, -slope * d_cur, NEG_INF)
        bias_prev = jnp.where(d_prev <= float(t), -slope * d_prev, NEG_INF)
        bias_both = jnp.concatenate([bias_prev, bias_cur], axis=1)
        bias_first = jnp.concatenate([jnp.full((t, t), NEG_INF, F32), bias_cur], axis=1)

        def rows(start, rate=rate):
            return pl.ds(start, t) if rate == 1 else pl.ds(start, t, stride=rate)

        def units(ug, carry, g=g, rate=rate, halo=halo, q_ref=q_ref, k_ref=k_ref, v_ref=v_ref,
                  kh_ref=kh_ref, vh_ref=vh_ref, bias_both=bias_both, bias_first=bias_first, rows=rows):
            starts, qs, ks, vs, biases = [], [], [], [], []
            for uu in range(DIL_UNROLL):
                u = ug * DIL_UNROLL + uu
                r = u % rate
                mb = u // rate
                start = mb * halo + r
                starts.append(start)
                qs.append((q_ref[rows(start), :] * (HEAD_W ** -0.5 * LOG2E)).astype(BF16))
                kp = kh_ref[rows(r), :]
                vp = vh_ref[rows(r), :]
                prev_ok = blk > 0
                if tile > halo:
                    pstart = jnp.maximum(start - halo, r)
                    in_tile = mb > 0
                    kp = jnp.where(in_tile, k_ref[rows(pstart), :], kp)
                    vp = jnp.where(in_tile, v_ref[rows(pstart), :], vp)
                    prev_ok = jnp.logical_or(prev_ok, in_tile)
                ks.append(jnp.concatenate([kp, k_ref[rows(start), :]], axis=0).astype(BF16))
                vs.append(jnp.concatenate([vp, v_ref[rows(start), :]], axis=0).astype(BF16))
                biases.append(jnp.where(prev_ok, bias_both, bias_first))
            ss = [_dot_nt(q, k) + bias for q, k, bias in zip(qs, ks, biases)]
            ms = [jnp.max(s, axis=-1, keepdims=True) for s in ss]
            ps = [jnp.exp2(s - m) for s, m in zip(ss, ms)]
            ls = [jnp.sum(p, axis=-1, keepdims=True) for p in ps]
            accs = [_dot(p.astype(BF16), v) for p, v in zip(ps, vs)]
            for start, m, l, acc in zip(starts, ms, ls, accs):
                o = acc / l
                lse = jnp.broadcast_to(m + jnp.log2(l), (t, HEAD_W))
                if g > 0:
                    lse_p = l_sc[rows(start), :]
                    gap = lse_p - lse
                    w_p = 1.0 / (1.0 + jnp.exp2(-gap))
                    o = o + w_p * (o_sc[rows(start), :] - o)
                    if g < n_g - 1:
                        lse = jnp.maximum(lse, lse_p) + jnp.log2(1.0 + jnp.exp2(-jnp.abs(gap)))
                o_sc[rows(start), :] = o
                if g < n_g - 1:
                    l_sc[rows(start), :] = lse
            return carry

        lax.fori_loop(0, tile // t // DIL_UNROLL, units, 0)

    o_ref[...] = o_sc[...].astype(o_ref.dtype)


def _dilated_attention(proj3, slopes4):
    b, s, _ = proj3.shape
    t = DIL_BLOCK
    tile = t * max(DIL_RATES)
    assert s % tile == 0
    in_specs, args = [], []
    for g, rate in enumerate(DIL_RATES):
        halo = t * rate
        per = tile // halo

        def cur(off, g=g):
            return pl.BlockSpec((None, tile, HEAD_W),
                                lambda bi, h, blk, sl: (bi, blk, off + g * N_HEADS + h))

        def front(off, g=g, halo=halo, per=per):
            return pl.BlockSpec((None, halo, HEAD_W),
                                lambda bi, h, blk, sl: (bi, jnp.maximum(blk * per - 1, 0), off + g * N_HEADS + h))

        in_specs += [cur(OFF_DL_Q), cur(OFF_DL_K), cur(OFF_DL_V), front(OFF_DL_K), front(OFF_DL_V)]
        args += [proj3] * 5
    grid_spec = pltpu.PrefetchScalarGridSpec(
        num_scalar_prefetch=1,
        grid=(b, N_HEADS, s // tile),
        in_specs=in_specs,
        out_specs=pl.BlockSpec((None, tile, HEAD_W), lambda bi, h, blk, sl: (bi, blk, h)),
        scratch_shapes=[pltpu.VMEM((tile, HEAD_W), F32), pltpu.VMEM((tile, HEAD_W), F32)],
    )
    return pl.pallas_call(
        functools.partial(_dil_kernel, tile=tile),
        grid_spec=grid_spec,
        out_shape=jax.ShapeDtypeStruct((b, s, BRANCH_W), BF16),
        compiler_params=_params(("parallel", "parallel", "parallel")),
        name="dilated",
    )(slopes4[:, 1:].T.reshape(-1), *args)


def _merge_kernel(ya_ref, yr_ref, yc_ref, yd_ref, w_ref, g0_ref, g1_ref, g2_ref, g3_ref, o_ref, wb_ref):
    @pl.when(pl.program_id(1) == 0)
    def _():
        _cast_weights(w_ref, wb_ref)

    acc = None
    for n, (y_ref, g_ref) in enumerate(((ya_ref, g0_ref), (yr_ref, g1_ref), (yc_ref, g2_ref), (yd_ref, g3_ref))):
        t = jax.nn.sigmoid(g_ref[...]) * _dot(y_ref[...], wb_ref[n])
        acc = t if acc is None else acc + t
    o_ref[...] = acc.astype(o_ref.dtype)


def _merge(ys, w_branch, layer, proj, d_model, *, tm=512, tn=1024):
    m = proj.shape[0]
    tm, tn = _pick(m, tm), _pick(d_model, tn)
    gcols = d_model // tn
    goff = OFF_GATES * 128 // tn
    y_spec = pl.BlockSpec((tm, BRANCH_W), lambda j, i: (i, 0))

    def gate_spec(n):
        return pl.BlockSpec((tm, tn), lambda j, i: (i, goff + n * gcols + j))

    return pl.pallas_call(
        _merge_kernel,
        grid=(d_model // tn, m // tm),
        in_specs=[y_spec, y_spec, y_spec, y_spec,
                  pl.BlockSpec((None, N_BRANCH, BRANCH_W, tn), lambda j, i: (layer, 0, 0, j),
                               pipeline_mode=pl.Buffered(1)),
                  gate_spec(0), gate_spec(1), gate_spec(2), gate_spec(3)],
        out_specs=pl.BlockSpec((tm, tn), lambda j, i: (i, j)),
        out_shape=jax.ShapeDtypeStruct((m, d_model), BF16),
        scratch_shapes=[pltpu.VMEM((N_BRANCH, BRANCH_W, tn), BF16)],
        compiler_params=_params(("parallel", "arbitrary")),
        name="merge",
    )(*ys, w_branch, proj, proj, proj, proj)


FFN_SUB_ROWS = 512


def _ffn_up_kernel(x_ref, w_hbm, cw_ref, cb_ref, o_ref, sa_ref, sg_ref, wab_ref, wgb_ref, sems,
                   carry_ref, *, tiles_per_seq, layer, n_blocks):
    i = pl.program_id(1)
    tn = o_ref.shape[1]
    _stage_weights(w_hbm, layer, 0, n_blocks, sa_ref, wab_ref, sems.at[0])
    _stage_weights(w_hbm, layer, n_blocks * tn, n_blocks, sg_ref, wgb_ref, sems.at[1])

    @pl.when(i % tiles_per_seq == 0)
    def _():
        carry_ref[...] = jnp.zeros(carry_ref.shape, F32)

    tm, tn = o_ref.shape
    sub = min(tm, FFN_SUB_ROWS)
    cw = cw_ref[...]
    cb = cb_ref[...]
    row = lax.broadcasted_iota(jnp.int32, (sub, tn), 0)
    prev = carry_ref[...]
    for r0 in range(0, tm, sub):
        x = x_ref[r0:r0 + sub, :]
        a = _dot(x, wab_ref[...])
        gl = _dot(x, wgb_ref[...])
        a1 = jnp.where(row == 0, prev[7:8, :], pltpu.roll(a, 1, 0))
        a2 = jnp.where(row == 0, prev[6:7, :],
                       jnp.where(row == 1, prev[7:8, :], pltpu.roll(a, 2, 0)))
        conv = cb + cw[0:1, :] * a2 + cw[1:2, :] * a1 + cw[2:3, :] * a
        gelu = 0.5 * conv * (1.0 + lax.erf(conv * (2.0 ** -0.5)))
        o_ref[r0:r0 + sub, :] = (gelu * gl).astype(o_ref.dtype)
        prev = a[sub - 8:sub, :]
    carry_ref[...] = prev


def _ffn_up(x, w_up, layer, conv_w, conv_b, seq, *, tm=2048, tn=256):
    m, kdim = x.shape
    d_ff = w_up.shape[-1] // 2
    tm, tn = _pick(seq, tm), _pick(d_ff, tn)
    nj = d_ff // tn
    return pl.pallas_call(
        functools.partial(_ffn_up_kernel, tiles_per_seq=seq // tm, layer=layer, n_blocks=nj),
        grid=(nj, m // tm),
        in_specs=[
            pl.BlockSpec((tm, kdim), lambda j, i: (i, 0)),
            pl.BlockSpec(memory_space=pl.ANY),
            pl.BlockSpec((CONV_W, tn), lambda j, i: (0, j)),
            pl.BlockSpec((1, tn), lambda j, i: (0, j)),
        ],
        out_specs=pl.BlockSpec((tm, tn), lambda j, i: (i, j)),
        out_shape=jax.ShapeDtypeStruct((m, d_ff), BF16),
        scratch_shapes=[pltpu.VMEM((kdim, tn), F32), pltpu.VMEM((kdim, tn), F32),
                        pltpu.VMEM((kdim, tn), BF16), pltpu.VMEM((kdim, tn), BF16),
                        pltpu.SemaphoreType.DMA((2,)), pltpu.VMEM((8, tn), F32)],
        compiler_params=_params(("arbitrary", "arbitrary")),
        name="ffn_up",
    )(x, w_up, conv_w, conv_b.reshape(1, d_ff))


def _alibi_slopes():
    n = 4 * N_HEADS
    slopes = 2.0 ** (-8.0 * jnp.arange(1, n + 1, dtype=F32) / n)
    return slopes.reshape(N_HEADS, 4)


def kernel(x, w_in, diff_lambda, diff_norm_g, hgrn_lb_logits, hgrn_norm_g, w_branch, w_o,
           ln1_g, ln1_b, w_up, conv_w, conv_b, w_down, ln2_g, ln2_b):
    b, s, d = x.shape
    depth = w_in.shape[0]
    m = b * s
    alpha = (2 * depth) ** 0.25
    slopes4 = _alibi_slopes()
    p = jax.nn.softmax(hgrn_lb_logits.astype(F32), axis=0)
    lower_bounds = jnp.cumsum(p, axis=0) - p[0]

    w_down_b = w_down.astype(BF16)
    res = x.reshape(m, d)
    hb = res.astype(BF16)
    for l in range(depth):
        lam_init = 0.8 - 0.6 * math.exp(-0.3 * l)
        proj = _matmul(hb, w_in, l, out_dtype=F32, tm=1024, name="in_proj")
        proj3 = proj.reshape(b, s, -1)
        ya = _diff_attention(proj3, slopes4[:, 0], diff_lambda[l], diff_norm_g[l], lam_init)
        yr = _retention(proj3)
        yc = _hgrn2(proj3, lower_bounds[l], hgrn_norm_g[l])
        yd = _dilated_attention(proj3, slopes4)
        ys = [y.reshape(m, BRANCH_W) for y in (ya, yr, yc, yd)]
        merged = _merge(ys, w_branch, l, proj, d)
        z = _matmul(merged, w_o, l, out_dtype=F32, res=res, alpha=alpha, name="out_proj")
        hb, mean, rstd = _layernorm(z, ln1_g[l], ln1_b[l], final=False)
        res = (z, mean, rstd, ln1_g[l], ln1_b[l])
        act = _ffn_up(hb, w_up, l, conv_w[l], conv_b[l], s)
        z = _matmul(act, w_down_b, l, out_dtype=F32, tn=512, res=res, alpha=alpha, name="ffn_down")
        if l == depth - 1:
            (out,) = _layernorm(z, ln2_g[l], ln2_b[l], final=True)
            return out.reshape(b, s, d)
        hb, mean, rstd = _layernorm(z, ln2_g[l], ln2_b[l], final=False)
        res = (z, mean, rstd, ln2_g[l], ln2_b[l])
```

```python
import functools
import math

import jax
import jax.numpy as jnp
from jax import lax
from jax.experimental import pallas as pl
from jax.experimental.pallas import tpu as pltpu

F32 = jnp.float32
BF16 = jnp.bfloat16

N_HEADS = 8
HEAD_W = 128
BRANCH_W = N_HEADS * HEAD_W
N_BRANCH = 4
DA_QK = 64
RET_QK = 64
RET_CHUNK = 256
RET_DECAY_BASE = 5.0
HG_CHUNK = 64
DIL_RATES = (1, 4, 16)
DIL_BLOCK = 128
DIL_UNROLL = 16
CONV_W = 3
EPS = 1e-5
NEG_INF = float("-inf")

_C = BRANCH_W // 128
OFF_DA_Q, OFF_DA_K, OFF_DA_V = 0, _C, 2 * _C
OFF_RT_Q, OFF_RT_K = 3 * _C, 3 * _C + _C // 2
OFF_RT_V, OFF_RT_G = 4 * _C, 5 * _C
OFF_HG_Q, OFF_HG_F, OFF_HG_I, OFF_HG_G = 6 * _C, 7 * _C, 8 * _C, 9 * _C
OFF_DL_Q, OFF_DL_K, OFF_DL_V = 10 * _C, 13 * _C, 16 * _C
OFF_GATES = 19 * _C

VMEM_LIMIT = 56 * 1024 * 1024


def _params(sem):
    return pltpu.CompilerParams(dimension_semantics=sem, vmem_limit_bytes=VMEM_LIMIT)


def _dot(a, b):
    return jnp.dot(a, b, preferred_element_type=F32)


def _dot_nt(a, b):
    return lax.dot_general(a, b, (((1,), (1,)), ((), ())), preferred_element_type=F32)


def _dot_tn(a, b):
    return lax.dot_general(a, b, (((0,), (0,)), ((), ())), preferred_element_type=F32)


def _pick(dim, pref):
    if dim <= pref:
        return dim
    t = pref
    while t >= 128:
        if dim % t == 0 and t % 128 == 0:
            return t
        t -= 128
    return dim


CAST_ROWS = 512


def _cast_weights(w_ref, wb_ref):
    rows = w_ref.shape[-2]
    step = min(rows, CAST_ROWS)
    for r0 in range(0, rows, step):
        r1 = min(rows, r0 + step)
        wb_ref[..., r0:r1, :] = w_ref[..., r0:r1, :].astype(BF16)


def _stage_weights(w_hbm, layer, first_col, n_blocks, stage_ref, wb_ref, sem):
    j = pl.program_id(0)
    tn = stage_ref.shape[-1]

    def copy(jj):
        col = pl.multiple_of(first_col + jj * tn, tn)
        return pltpu.make_async_copy(w_hbm.at[layer, :, pl.ds(col, tn)], stage_ref, sem)

    @pl.when(pl.program_id(1) == 0)
    def _():
        @pl.when(j == 0)
        def _():
            copy(0).start()

        copy(j).wait()
        _cast_weights(stage_ref, wb_ref)

        @pl.when(j + 1 < n_blocks)
        def _():
            copy(j + 1).start()


def _mm_kernel(*refs, alpha, n_res, layer, n_blocks):
    x_ref, w_ref = refs[:2]
    res_refs = refs[2:2 + n_res]
    o_ref = refs[2 + n_res]
    scratch = refs[3 + n_res:]

    if scratch:
        stage_ref, wb_ref, sem = scratch
        _stage_weights(w_ref, layer, 0, n_blocks, stage_ref, wb_ref, sem)
    else:
        wb_ref = w_ref

    acc = _dot(x_ref[...], wb_ref[...])
    if n_res == 1:
        acc = alpha * res_refs[0][...] + acc
    elif n_res:
        z_ref, mu_ref, rs_ref, g_ref, b_ref = res_refs
        reps = (1, o_ref.shape[1] // HEAD_W)
        h = (z_ref[...] - jnp.tile(mu_ref[...], reps)) * jnp.tile(rs_ref[...], reps) * g_ref[...] + b_ref[...]
        acc = alpha * h + acc
    o_ref[...] = acc.astype(o_ref.dtype)


def _matmul(x, w, layer, *, out_dtype, tm=512, tn=1024, res=None, alpha=1.0, name="mm"):
    m, kdim = x.shape
    n = w.shape[-1]
    tm, tn = _pick(m, tm), _pick(n, tn)
    if w.dtype == BF16:
        w_spec = pl.BlockSpec((None, kdim, tn), lambda j, i: (layer, 0, j), pipeline_mode=pl.Buffered(1))
        scratch = []
    else:
        w_spec = pl.BlockSpec(memory_space=pl.ANY)
        scratch = [pltpu.VMEM((kdim, tn), F32), pltpu.VMEM((kdim, tn), BF16), pltpu.SemaphoreType.DMA(())]
    in_specs = [pl.BlockSpec((tm, kdim), lambda j, i: (i, 0)), w_spec]
    args = [x, w]
    n_res = 0
    if res is not None:
        tile = pl.BlockSpec((tm, tn), lambda j, i: (i, j))
        if isinstance(res, tuple):
            z, mean, rstd, gain, bias = res
            stat = pl.BlockSpec((tm, HEAD_W), lambda j, i: (i, 0))
            vec = pl.BlockSpec((1, tn), lambda j, i: (0, j))
            in_specs += [tile, stat, stat, vec, vec]
            args += [z, mean, rstd, gain.reshape(1, n), bias.reshape(1, n)]
        else:
            in_specs.append(tile)
            args.append(res)
        n_res = len(args) - 2
    return pl.pallas_call(
        functools.partial(_mm_kernel, alpha=alpha, n_res=n_res, layer=layer, n_blocks=n // tn),
        grid=(n // tn, m // tm),
        in_specs=in_specs,
        out_specs=pl.BlockSpec((tm, tn), lambda j, i: (i, j)),
        out_shape=jax.ShapeDtypeStruct((m, n), out_dtype),
        scratch_shapes=scratch,
        compiler_params=_params(("arbitrary", "arbitrary")),
        name=name,
    )(*args)


LN_STRIP = 8


def _ln_kernel(x_ref, g_ref, b_ref, *out_refs, final):
    g = g_ref[...]
    b = b_ref[...]
    strip = min(LN_STRIP, x_ref.shape[0])
    for r0 in range(0, x_ref.shape[0], strip):
        rows = slice(r0, r0 + strip)
        x = x_ref[rows, :]
        mu = jnp.mean(x, axis=-1, keepdims=True)
        xc = x - mu
        rstd = lax.rsqrt(jnp.mean(xc * xc, axis=-1, keepdims=True) + EPS)
        y = xc * rstd * g + b
        if final:
            out_refs[0][rows, :] = y
        else:
            ob_ref, mu_ref, rs_ref = out_refs
            ob_ref[rows, :] = y.astype(BF16)
            mu_ref[rows, :] = jnp.broadcast_to(mu, (strip, HEAD_W))
            rs_ref[rows, :] = jnp.broadcast_to(rstd, (strip, HEAD_W))


def _layernorm(x, g, b, *, final, tm=256):
    m, d = x.shape
    tm = _pick(m, tm)
    row = pl.BlockSpec((tm, d), lambda i: (i, 0))
    stat = pl.BlockSpec((tm, HEAD_W), lambda i: (i, 0))
    if final:
        out_specs, out_shape = [row], [jax.ShapeDtypeStruct((m, d), F32)]
    else:
        out_specs = [row, stat, stat]
        out_shape = [jax.ShapeDtypeStruct((m, d), BF16)] + [jax.ShapeDtypeStruct((m, HEAD_W), F32)] * 2
    return pl.pallas_call(
        functools.partial(_ln_kernel, final=final),
        grid=(m // tm,),
        in_specs=[row, pl.BlockSpec((1, d), lambda i: (0, 0)), pl.BlockSpec((1, d), lambda i: (0, 0))],
        out_specs=out_specs,
        out_shape=out_shape,
        compiler_params=_params(("parallel",)),
        name="layernorm",
    )(x, g.reshape(1, d), b.reshape(1, d))


LOG2E = 1.4426950408889634


def _split3_f32(x):
    hi = x.astype(BF16).astype(F32)
    mid = (x - hi).astype(BF16).astype(F32)
    lo = (x - hi - mid).astype(BF16).astype(F32)
    return hi, mid, lo


def _lane_fill(lane, first, vals, other):
    out = other
    for n, val in enumerate(vals):
        out = jnp.where(lane == first + n, val, out)
    return out


def _diff_attn_kernel(slope_ref, lam_ref, g_ref, q_ref, k_ref, v_ref, o_ref,
                      m1_ref, a1_ref, m2_ref, a2_ref, kaug_sc, mask_sc, sa_sc, sb_sc, kk_sc, vv_sc, *, t, lam_init):
    h = pl.program_id(1)
    qi = pl.program_id(2)
    slope2 = slope_ref[h] * LOG2E
    lane = lax.broadcasted_iota(jnp.int32, (t, HEAD_W), 1)
    aug_first = (DA_QK, 0)
    owns = (lane < DA_QK, lane >= DA_QK)
    accs = ((m1_ref, a1_ref), (m2_ref, a2_ref))

    @pl.when(qi == 0)
    def _():
        j = lax.broadcasted_iota(jnp.int32, (t, HEAD_W), 0).astype(F32)
        parts = _split3_f32(slope2 * j)
        for idx in range(2):
            kaug_sc[idx] = _lane_fill(lane, aug_first[idx], list(parts) + [1.0, 1.0, 1.0], 0.0)
        d = (lax.broadcasted_iota(jnp.int32, (t, t), 0) - lax.broadcasted_iota(jnp.int32, (t, t), 1))
        mask_sc[...] = jnp.where(d >= 0, 0.0, NEG_INF)
        ones0 = jnp.where(lane == 0, 1.0, 0.0).astype(BF16)

        def prep(kb, carry):
            r0 = pl.multiple_of(kb * t, t)
            k = k_ref[pl.ds(r0, t), :]
            for idx in range(2):
                kk_sc[idx, pl.ds(r0, t), :] = jnp.where(owns[idx], k, kaug_sc[idx]).astype(BF16)
            vv_sc[pl.ds(r0, t), :] = jnp.concatenate([v_ref[pl.ds(r0, t), :].astype(BF16), ones0], axis=1)
            return carry

        lax.fori_loop(0, k_ref.shape[0] // t, prep, 0)

    for m_ref, a_ref in accs:
        m_ref[...] = jnp.full(m_ref.shape, NEG_INF, F32)
        a_ref[...] = jnp.zeros(a_ref.shape, F32)

    q = q_ref[...] * (DA_QK ** -0.5 * LOG2E)

    def scores(ki, s_ref):
        r0 = pl.multiple_of(ki * t, t)
        block_c = jnp.full((1, HEAD_W), -slope2 * ((qi - ki) * t).astype(F32), F32)
        c_parts = _split3_f32(block_c)
        for idx in range(2):
            q_aug = _lane_fill(lane[0:1], aug_first[idx], [1.0, 1.0, 1.0] + list(c_parts), 0.0)
            qq = jnp.where(owns[idx], q, q_aug).astype(BF16)
            s_ref[idx] = _dot_nt(qq, kk_sc[idx, pl.ds(r0, t), :])

    def update(ki, s_ref, masked):
        r0 = pl.multiple_of(ki * t, t)
        v_aug = vv_sc[pl.ds(r0, t), :]
        for idx, (m_ref, a_ref) in enumerate(accs):
            s = s_ref[idx]
            if masked:
                s = s + mask_sc[...]
            m_prev = m_ref[...]
            m_new = jnp.maximum(m_prev, jnp.max(s, axis=-1, keepdims=True))
            corr = jnp.exp2(m_prev - m_new)
            p = jnp.exp2(s - jnp.tile(m_new, (1, t // HEAD_W)))
            a_ref[...] = jnp.tile(corr, (1, 2)) * a_ref[...] + _dot(p.astype(BF16), v_aug)
            m_ref[...] = m_new

    scores(0, sa_sc)

    def body(pair, carry):
        k0 = 2 * pair
        scores(k0 + 1, sb_sc)
        update(k0, sa_sc, False)
        scores(k0 + 2, sa_sc)
        update(k0 + 1, sb_sc, False)
        return carry

    lax.fori_loop(0, qi // 2, body, 0)

    @pl.when(qi % 2 == 0)
    def _():
        update(qi, sa_sc, True)

    @pl.when(qi % 2 == 1)
    def _():
        scores(qi, sb_sc)
        update(qi - 1, sa_sc, False)
        update(qi, sb_sc, True)

    lv = lam_ref[...]
    lam = (jnp.exp(jnp.sum(lv[0:1] * lv[1:2], axis=-1, keepdims=True))
           - jnp.exp(jnp.sum(lv[2:3] * lv[3:4], axis=-1, keepdims=True)) + lam_init)
    o1 = a1_ref[:, :HEAD_W] / a1_ref[:, HEAD_W:HEAD_W + 1]
    o2 = a2_ref[:, :HEAD_W] / a2_ref[:, HEAD_W:HEAD_W + 1]
    o = o1 - lam * o2
    ms = jnp.mean(o * o, axis=-1, keepdims=True)
    o = o * lax.rsqrt(ms + EPS) * g_ref[...] * (1.0 - lam_init)
    o_ref[...] = o.astype(o_ref.dtype)


def _diff_attention(proj3, slopes, lam_vec, g, lam_init, *, t=512):
    b, s, _ = proj3.shape
    t = _pick(s, t)
    kern = functools.partial(_diff_attn_kernel, t=t, lam_init=lam_init)
    grid_spec = pltpu.PrefetchScalarGridSpec(
        num_scalar_prefetch=1,
        grid=(b, N_HEADS, s // t),
        in_specs=[
            pl.BlockSpec(lam_vec.shape, lambda bi, h, qi, sl: (0, 0)),
            pl.BlockSpec((1, HEAD_W), lambda bi, h, qi, sl: (0, 0)),
            pl.BlockSpec((None, t, HEAD_W), lambda bi, h, qi, sl: (bi, qi, OFF_DA_Q + h)),
            pl.BlockSpec((None, s, HEAD_W), lambda bi, h, qi, sl: (bi, 0, OFF_DA_K + h)),
            pl.BlockSpec((None, s, HEAD_W), lambda bi, h, qi, sl: (bi, 0, OFF_DA_V + h)),
        ],
        out_specs=pl.BlockSpec((None, t, HEAD_W), lambda bi, h, qi, sl: (bi, qi, h)),
        scratch_shapes=[pltpu.VMEM((t, HEAD_W), F32), pltpu.VMEM((t, 2 * HEAD_W), F32)] * 2
        + [pltpu.VMEM((2, t, HEAD_W), F32), pltpu.VMEM((t, t), F32)]
        + [pltpu.VMEM((2, t, t), F32)] * 2
        + [pltpu.VMEM((2, s, HEAD_W), BF16), pltpu.VMEM((s, 2 * HEAD_W), BF16)],
    )
    return pl.pallas_call(
        kern,
        grid_spec=grid_spec,
        out_shape=jax.ShapeDtypeStruct((b, s, BRANCH_W), BF16),
        compiler_params=_params(("parallel", "parallel", "arbitrary")),
        name="diff_attention",
    )(slopes, lam_vec, g.reshape(1, HEAD_W), proj3, proj3, proj3)


def _retention_kernel(inner_ref, qd_ref, kd_ref, cd_ref, q_ref, k_ref, v_ref, g_ref, o_ref,
                      state_ref, *, rows):
    ci = pl.program_id(2)

    @pl.when(ci == 0)
    def _():
        state_ref[...] = jnp.zeros(state_ref.shape, F32)

    c = RET_CHUNK
    for p in range(2):
        inner = inner_ref[p]
        qdec = qd_ref[p]
        kdec = kd_ref[p]
        cdec = cd_ref[p]
        state = state_ref[p]
        for n in range(rows // c):
            r0 = n * c
            qc = (q_ref[r0:r0 + c, p * RET_QK:(p + 1) * RET_QK] * (RET_QK ** -0.5)).astype(BF16)
            kf = k_ref[r0:r0 + c, p * RET_QK:(p + 1) * RET_QK]
            vc = v_ref[r0:r0 + c, p * HEAD_W:(p + 1) * HEAD_W].astype(BF16)
            att = _dot_nt(qc, kf.astype(BF16)) * inner
            o = _dot(att.astype(BF16), vc) + _dot(qc, state.astype(BF16)) * qdec
            state = cdec * state + _dot_tn((kf * kdec).astype(BF16), vc)
            mu = jnp.mean(o, axis=-1, keepdims=True)
            oc = o - mu
            var = jnp.mean(oc * oc, axis=-1, keepdims=True)
            gate = g_ref[r0:r0 + c, p * HEAD_W:(p + 1) * HEAD_W]
            y = oc * lax.rsqrt(var + EPS) * (gate * jax.nn.sigmoid(gate))
            o_ref[r0:r0 + c, p * HEAD_W:(p + 1) * HEAD_W] = y.astype(o_ref.dtype)
        state_ref[p] = state


def _retention_tables():
    h = N_HEADS
    c = RET_CHUNK
    log_g = jnp.log1p(-(2.0 ** (-RET_DECAY_BASE - jnp.arange(h, dtype=F32))))
    idx = jnp.arange(c, dtype=F32)
    rel = idx[:, None] - idx[None, :]
    inner = jnp.where(rel >= 0, jnp.exp(log_g[:, None, None] * jnp.maximum(rel, 0.0)), 0.0)
    qd = jnp.broadcast_to(jnp.exp(log_g[:, None] * (idx + 1.0))[:, :, None], (h, c, HEAD_W))
    kd = jnp.broadcast_to(jnp.exp(log_g[:, None] * (c - 1.0 - idx))[:, :, None], (h, c, RET_QK))
    cd = jnp.broadcast_to(jnp.exp(log_g * c)[:, None, None], (h, RET_QK, HEAD_W))
    return inner, qd, kd, cd


def _retention(proj3, *, rows=512):
    b, s, _ = proj3.shape
    rows = _pick(s, rows)
    inner, qd, kd, cd = _retention_tables()
    c = RET_CHUNK
    return pl.pallas_call(
        functools.partial(_retention_kernel, rows=rows),
        grid=(b, N_HEADS // 2, s // rows),
        in_specs=[
            pl.BlockSpec((2, c, c), lambda bi, hp, ci: (hp, 0, 0)),
            pl.BlockSpec((2, c, HEAD_W), lambda bi, hp, ci: (hp, 0, 0)),
            pl.BlockSpec((2, c, RET_QK), lambda bi, hp, ci: (hp, 0, 0)),
            pl.BlockSpec((2, RET_QK, HEAD_W), lambda bi, hp, ci: (hp, 0, 0)),
            pl.BlockSpec((None, rows, 128), lambda bi, hp, ci: (bi, ci, OFF_RT_Q + hp)),
            pl.BlockSpec((None, rows, 128), lambda bi, hp, ci: (bi, ci, OFF_RT_K + hp)),
            pl.BlockSpec((None, rows, 2 * HEAD_W), lambda bi, hp, ci: (bi, ci, OFF_RT_V // 2 + hp)),
            pl.BlockSpec((None, rows, 2 * HEAD_W), lambda bi, hp, ci: (bi, ci, OFF_RT_G // 2 + hp)),
        ],
        out_specs=pl.BlockSpec((None, rows, 2 * HEAD_W), lambda bi, hp, ci: (bi, ci, hp)),
        out_shape=jax.ShapeDtypeStruct((b, s, BRANCH_W), BF16),
        scratch_shapes=[pltpu.VMEM((2, RET_QK, HEAD_W), F32)],
        compiler_params=_params(("parallel", "parallel", "arbitrary")),
        name="retention",
    )(inner, qd, kd, cd, proj3, proj3, proj3, proj3)


def _split3(x):
    hi = x.astype(BF16)
    r1 = x - hi.astype(F32)
    mid = r1.astype(BF16)
    lo = (r1 - mid.astype(F32)).astype(BF16)
    return hi, mid, lo


HG_HEADS_PER_STEP = 4
HG_SUB = 16


def _hgrn2_kernel(lb_ref, gn_ref, q_ref, f_ref, i_ref, gt_ref, o_ref,
                  state_ref, b_sc, k_sc, q_sc, *, rows):
    for p in range(HG_HEADS_PER_STEP):
        lanes = pl.ds(p * HEAD_W, HEAD_W)
        _hgrn2_head(lb_ref.at[p], gn_ref, q_ref.at[:, lanes], f_ref.at[:, lanes], i_ref.at[:, lanes],
                    gt_ref.at[:, lanes], o_ref.at[:, lanes], state_ref.at[p], b_sc.at[p], k_sc.at[p],
                    q_sc.at[p], rows=rows)


def _hgrn2_head(lb_ref, gn_ref, q_ref, f_ref, i_ref, gt_ref, o_ref,
                state_ref, b_sc, k_sc, q_sc, *, rows):
    ci = pl.program_id(2)
    c, sc = HG_CHUNK, HG_SUB

    @pl.when(ci == 0)
    def _():
        state_ref[...] = jnp.zeros(state_ref.shape, F32)

    lb = lb_ref[...]
    gn = gn_ref[...]
    fr = f_ref[...]
    log_f = jnp.log(lb + (1.0 - lb) * jax.nn.sigmoid(fr))
    k_sc[...] = (1.0 - lb) * jax.nn.sigmoid(-fr)
    qraw = q_ref[...]
    q_sc[...] = qraw * jax.nn.sigmoid(qraw)
    ri = lax.broadcasted_iota(jnp.int32, (rows, rows), 0)
    cj = lax.broadcasted_iota(jnp.int32, (rows, rows), 1)
    tri = jnp.where(jnp.logical_and(ri >= cj, (ri // c) == (cj // c)), 1.0, 0.0).astype(BF16)
    hi, mid, lo = _split3(log_f)
    b_sc[...] = (_dot(tri, hi) + _dot(tri, mid) + _dot(tri, lo)) * LOG2E
    hs = sc // 2
    row_half = lax.broadcasted_iota(jnp.int32, (hs, HEAD_W), 0)
    row_chunk = lax.broadcasted_iota(jnp.int32, (c, HEAD_W), 0)

    for n in range(rows // c):
        r0 = n * c
        parts = []
        for sub in range(c // sc):
            s0 = r0 + sub * sc
            b_i = b_sc[s0:s0 + sc, :]
            q_i = q_sc[s0:s0 + sc, :]
            halves = [(b_i[:hs], q_i[:hs]), (b_i[hs:], q_i[hs:])]
            acc_h = [jnp.zeros((hs, HEAD_W), F32), jnp.zeros((hs, HEAD_W), F32)]
            for jj in range(sc):
                j = s0 + jj
                b_j, k_j, v_j = b_sc[j:j + 1, :], k_sc[j:j + 1, :], i_ref[j:j + 1, :]
                for hh, (b_h, q_h) in enumerate(halves):
                    first = jj - hh * hs
                    if first >= hs:
                        continue
                    d = b_h - b_j
                    if first > 0:
                        d = jnp.where(row_half >= first, d, NEG_INF)
                    a = jnp.sum(q_h * k_j * jnp.exp2(d), axis=-1, keepdims=True)
                    acc_h[hh] = acc_h[hh] + a * v_j
            parts.append(jnp.concatenate(acc_h, axis=0))
        o = jnp.concatenate(parts, axis=0)
        bc = b_sc[r0:r0 + c, :]
        qc = q_sc[r0:r0 + c, :]
        kc = k_sc[r0:r0 + c, :]
        vc = i_ref[r0:r0 + c, :].astype(BF16)
        q_segs, k_segs = [], []
        for sub in range(1, c // sc):
            anchor = b_sc[r0 + sub * sc:r0 + sub * sc + 1, :]
            in_sub = jnp.logical_and(row_chunk >= sub * sc, row_chunk < (sub + 1) * sc)
            q_segs.append(qc * jnp.exp2(jnp.where(in_sub, bc - anchor, NEG_INF)))
            k_segs.append(kc * jnp.exp2(jnp.where(row_chunk < sub * sc, anchor - bc, NEG_INF)))
        att = _dot_nt(jnp.concatenate(q_segs, axis=1).astype(BF16),
                      jnp.concatenate(k_segs, axis=1).astype(BF16))
        state_t = state_ref[...]
        o = (o + _dot(att.astype(BF16), vc)
             + _dot_nt((qc * jnp.exp2(bc)).astype(BF16), state_t.astype(BF16)))
        blast = b_sc[r0 + c - 1:r0 + c, :]
        kd = (kc * jnp.exp2(blast - bc)).astype(BF16)
        state_ref[...] = jnp.exp2(blast) * state_t + _dot_tn(vc, kd)
        ms = jnp.mean(o * o, axis=-1, keepdims=True)
        gate = gt_ref[r0:r0 + c, :]
        y = o * lax.rsqrt(ms + EPS) * gn * (gate * jax.nn.sigmoid(gate))
        o_ref[r0:r0 + c, :] = y.astype(o_ref.dtype)


def _hgrn2(proj3, lb, gn, *, rows=256):
    b, s, _ = proj3.shape
    rows = _pick(s, rows)
    g = HG_HEADS_PER_STEP
    blk = lambda off: pl.BlockSpec((None, rows, g * HEAD_W), lambda bi, hp, ci: (bi, ci, off // g + hp))
    return pl.pallas_call(
        functools.partial(_hgrn2_kernel, rows=rows),
        grid=(b, N_HEADS // g, s // rows),
        in_specs=[
            pl.BlockSpec((g, 1, HEAD_W), lambda bi, hp, ci: (hp, 0, 0)),
            pl.BlockSpec((1, HEAD_W), lambda bi, hp, ci: (0, 0)),
            blk(OFF_HG_Q), blk(OFF_HG_F), blk(OFF_HG_I), blk(OFF_HG_G),
        ],
        out_specs=pl.BlockSpec((None, rows, g * HEAD_W), lambda bi, hp, ci: (bi, ci, hp)),
        out_shape=jax.ShapeDtypeStruct((b, s, BRANCH_W), BF16),
        scratch_shapes=[pltpu.VMEM((g, HEAD_W, HEAD_W), F32)] + [pltpu.VMEM((g, rows, HEAD_W), F32)] * 3,
        compiler_params=_params(("parallel", "parallel", "arbitrary")),
        name="hgrn2",
    )(lb.reshape(N_HEADS, 1, HEAD_W), gn.reshape(1, HEAD_W), proj3, proj3, proj3, proj3)


def _dil_kernel(slope_ref, *refs, tile):
    n_g = len(DIL_RATES)
    groups = [refs[5 * g:5 * g + 5] for g in range(n_g)]
    o_ref, o_sc, l_sc = refs[5 * n_g:]
    h = pl.program_id(1)
    blk = pl.program_id(2)
    t = DIL_BLOCK
    d_cur = (lax.broadcasted_iota(jnp.int32, (t, t), 0)
             - lax.broadcasted_iota(jnp.int32, (t, t), 1)).astype(F32)
    d_prev = d_cur + float(t)

    for g, (q_ref, k_ref, v_ref, kh_ref, vh_ref) in enumerate(groups):
        rate = DIL_RATES[g]
        halo = t * rate
        slope = slope_ref[g * N_HEADS + h] * (float(rate) * LOG2E)
        bias_cur = jnp.where(d_cur >= 0, -slope * d_cur, NEG_INF)
        bias_prev = jnp.where(d_prev <= float(t), -slope * d_prev, NEG_INF)
        bias_both = jnp.concatenate([bias_prev, bias_cur], axis=1)
        bias_first = jnp.concatenate([jnp.full((t, t), NEG_INF, F32), bias_cur], axis=1)

        def rows(start, rate=rate):
            return pl.ds(start, t) if rate == 1 else pl.ds(start, t, stride=rate)

        def units(ug, carry, g=g, rate=rate, halo=halo, q_ref=q_ref, k_ref=k_ref, v_ref=v_ref,
                  kh_ref=kh_ref, vh_ref=vh_ref, bias_both=bias_both, bias_first=bias_first, rows=rows):
            starts, qs, ks, vs, biases = [], [], [], [], []
            for uu in range(DIL_UNROLL):
                u = ug * DIL_UNROLL + uu
                r = u % rate
                mb = u // rate
                start = mb * halo + r
                starts.append(start)
                qs.append((q_ref[rows(start), :] * (HEAD_W ** -0.5 * LOG2E)).astype(BF16))
                kp = kh_ref[rows(r), :]
                vp = vh_ref[rows(r), :]
                prev_ok = blk > 0
                if tile > halo:
                    pstart = jnp.maximum(start - halo, r)
                    in_tile = mb > 0
                    kp = jnp.where(in_tile, k_ref[rows(pstart), :], kp)
                    vp = jnp.where(in_tile, v_ref[rows(pstart), :], vp)
                    prev_ok = jnp.logical_or(prev_ok, in_tile)
                ks.append(jnp.concatenate([kp, k_ref[rows(start), :]], axis=0).astype(BF16))
                vs.append(jnp.concatenate([vp, v_ref[rows(start), :]], axis=0).astype(BF16))
                biases.append(jnp.where(prev_ok, bias_both, bias_first))
            ss = [_dot_nt(q, k) + bias for q, k, bias in zip(qs, ks, biases)]
            ms = [jnp.max(s, axis=-1, keepdims=True) for s in ss]
            ps = [jnp.exp2(s - m) for s, m in zip(ss, ms)]
            ls = [jnp.sum(p, axis=-1, keepdims=True) for p in ps]
            accs = [_dot(p.astype(BF16), v) for p, v in zip(ps, vs)]
            for start, m, l, acc in zip(starts, ms, ls, accs):
                o = acc / l
                lse = jnp.broadcast_to(m + jnp.log2(l), (t, HEAD_W))
                if g > 0:
                    lse_p = l_sc[rows(start), :]
                    gap = lse_p - lse
                    w_p = 1.0 / (1.0 + jnp.exp2(-gap))
                    o = o + w_p * (o_sc[rows(start), :] - o)
                    if g < n_g - 1:
                        lse = jnp.maximum(lse, lse_p) + jnp.log2(1.0 + jnp.exp2(-jnp.abs(gap)))
                o_sc[rows(start), :] = o
                if g < n_g - 1:
                    l_sc[rows(start), :] = lse
            return carry

        lax.fori_loop(0, tile // t // DIL_UNROLL, units, 0)

    o_ref[...] = o_sc[...].astype(o_ref.dtype)


def _dilated_attention(proj3, slopes4):
    b, s, _ = proj3.shape
    t = DIL_BLOCK
    tile = t * max(DIL_RATES)
    assert s % tile == 0
    in_specs, args = [], []
    for g, rate in enumerate(DIL_RATES):
        halo = t * rate
        per = tile // halo

        def cur(off, g=g):
            return pl.BlockSpec((None, tile, HEAD_W),
                                lambda bi, h, blk, sl: (bi, blk, off + g * N_HEADS + h))

        def front(off, g=g, halo=halo, per=per):
            return pl.BlockSpec((None, halo, HEAD_W),
                                lambda bi, h, blk, sl: (bi, jnp.maximum(blk * per - 1, 0), off + g * N_HEADS + h))

        in_specs += [cur(OFF_DL_Q), cur(OFF_DL_K), cur(OFF_DL_V), front(OFF_DL_K), front(OFF_DL_V)]
        args += [proj3] * 5
    grid_spec = pltpu.PrefetchScalarGridSpec(
        num_scalar_prefetch=1,
        grid=(b, N_HEADS, s // tile),
        in_specs=in_specs,
        out_specs=pl.BlockSpec((None, tile, HEAD_W), lambda bi, h, blk, sl: (bi, blk, h)),
        scratch_shapes=[pltpu.VMEM((tile, HEAD_W), F32), pltpu.VMEM((tile, HEAD_W), F32)],
    )
    return pl.pallas_call(
        functools.partial(_dil_kernel, tile=tile),
        grid_spec=grid_spec,
        out_shape=jax.ShapeDtypeStruct((b, s, BRANCH_W), BF16),
        compiler_params=_params(("parallel", "parallel", "parallel")),
        name="dilated",
    )(slopes4[:, 1:].T.reshape(-1), *args)


def _merge_kernel(ya_ref, yr_ref, yc_ref, yd_ref, w_ref, g0_ref, g1_ref, g2_ref, g3_ref, o_ref, wb_ref):
    @pl.when(pl.program_id(1) == 0)
    def _():
        _cast_weights(w_ref, wb_ref)

    acc = None
    for n, (y_ref, g_ref) in enumerate(((ya_ref, g0_ref), (yr_ref, g1_ref), (yc_ref, g2_ref), (yd_ref, g3_ref))):
        t = jax.nn.sigmoid(g_ref[...]) * _dot(y_ref[...], wb_ref[n])
        acc = t if acc is None else acc + t
    o_ref[...] = acc.astype(o_ref.dtype)


def _merge(ys, w_branch, layer, proj, d_model, *, tm=512, tn=1024):
    m = proj.shape[0]
    tm, tn = _pick(m, tm), _pick(d_model, tn)
    gcols = d_model // tn
    goff = OFF_GATES * 128 // tn
    y_spec = pl.BlockSpec((tm, BRANCH_W), lambda j, i: (i, 0))

    def gate_spec(n):
        return pl.BlockSpec((tm, tn), lambda j, i: (i, goff + n * gcols + j))

    return pl.pallas_call(
        _merge_kernel,
        grid=(d_model // tn, m // tm),
        in_specs=[y_spec, y_spec, y_spec, y_spec,
                  pl.BlockSpec((None, N_BRANCH, BRANCH_W, tn), lambda j, i: (layer, 0, 0, j),
                               pipeline_mode=pl.Buffered(1)),
                  gate_spec(0), gate_spec(1), gate_spec(2), gate_spec(3)],
        out_specs=pl.BlockSpec((tm, tn), lambda j, i: (i, j)),
        out_shape=jax.ShapeDtypeStruct((m, d_model), BF16),
        scratch_shapes=[pltpu.VMEM((N_BRANCH, BRANCH_W, tn), BF16)],
        compiler_params=_params(("parallel", "arbitrary")),
        name="merge",
    )(*ys, w_branch, proj, proj, proj, proj)


FFN_SUB_ROWS = 512


def _ffn_up_kernel(x_ref, w_hbm, cw_ref, cb_ref, o_ref, sa_ref, sg_ref, wab_ref, wgb_ref, sems,
                   carry_ref, *, tiles_per_seq, layer, n_blocks):
    i = pl.program_id(1)
    tn = o_ref.shape[1]
    _stage_weights(w_hbm, layer, 0, n_blocks, sa_ref, wab_ref, sems.at[0])
    _stage_weights(w_hbm, layer, n_blocks * tn, n_blocks, sg_ref, wgb_ref, sems.at[1])

    @pl.when(i % tiles_per_seq == 0)
    def _():
        carry_ref[...] = jnp.zeros(carry_ref.shape, F32)

    tm, tn = o_ref.shape
    sub = min(tm, FFN_SUB_ROWS)
    cw = cw_ref[...]
    cb = cb_ref[...]
    row = lax.broadcasted_iota(jnp.int32, (sub, tn), 0)
    prev = carry_ref[...]
    for r0 in range(0, tm, sub):
        x = x_ref[r0:r0 + sub, :]
        a = _dot(x, wab_ref[...])
        gl = _dot(x, wgb_ref[...])
        a1 = jnp.where(row == 0, prev[7:8, :], pltpu.roll(a, 1, 0))
        a2 = jnp.where(row == 0, prev[6:7, :],
                       jnp.where(row == 1, prev[7:8, :], pltpu.roll(a, 2, 0)))
        conv = cb + cw[0:1, :] * a2 + cw[1:2, :] * a1 + cw[2:3, :] * a
        gelu = 0.5 * conv * (1.0 + lax.erf(conv * (2.0 ** -0.5)))
        o_ref[r0:r0 + sub, :] = (gelu * gl).astype(o_ref.dtype)
        prev = a[sub - 8:sub, :]
    carry_ref[...] = prev


def _ffn_up(x, w_up, layer, conv_w, conv_b, seq, *, tm=2048, tn=256):
    m, kdim = x.shape
    d_ff = w_up.shape[-1] // 2
    tm, tn = _pick(seq, tm), _pick(d_ff, tn)
    nj = d_ff // tn
    return pl.pallas_call(
        functools.partial(_ffn_up_kernel, tiles_per_seq=seq // tm, layer=layer, n_blocks=nj),
        grid=(nj, m // tm),
        in_specs=[
            pl.BlockSpec((tm, kdim), lambda j, i: (i, 0)),
            pl.BlockSpec(memory_space=pl.ANY),
            pl.BlockSpec((CONV_W, tn), lambda j, i: (0, j)),
            pl.BlockSpec((1, tn), lambda j, i: (0, j)),
        ],
        out_specs=pl.BlockSpec((tm, tn), lambda j, i: (i, j)),
        out_shape=jax.ShapeDtypeStruct((m, d_ff), BF16),
        scratch_shapes=[pltpu.VMEM((kdim, tn), F32), pltpu.VMEM((kdim, tn), F32),
                        pltpu.VMEM((kdim, tn), BF16), pltpu.VMEM((kdim, tn), BF16),
                        pltpu.SemaphoreType.DMA((2,)), pltpu.VMEM((8, tn), F32)],
        compiler_params=_params(("arbitrary", "arbitrary")),
        name="ffn_up",
    )(x, w_up, conv_w, conv_b.reshape(1, d_ff))


def _alibi_slopes():
    n = 4 * N_HEADS
    slopes = 2.0 ** (-8.0 * jnp.arange(1, n + 1, dtype=F32) / n)
    return slopes.reshape(N_HEADS, 4)


def kernel(x, w_in, diff_lambda, diff_norm_g, hgrn_lb_logits, hgrn_norm_g, w_branch, w_o,
           ln1_g, ln1_b, w_up, conv_w, conv_b, w_down, ln2_g, ln2_b):
    b, s, d = x.shape
    depth = w_in.shape[0]
    m = b * s
    alpha = (2 * depth) ** 0.25
    slopes4 = _alibi_slopes()
    p = jax.nn.softmax(hgrn_lb_logits.astype(F32), axis=0)
    lower_bounds = jnp.cumsum(p, axis=0) - p[0]

    w_down_b = w_down.astype(BF16)
    res = x.reshape(m, d)
    hb = res.astype(BF16)
    for l in range(depth):
        lam_init = 0.8 - 0.6 * math.exp(-0.3 * l)
        proj = _matmul(hb, w_in, l, out_dtype=F32, tm=1024, name="in_proj")
        proj3 = proj.reshape(b, s, -1)
        ya = _diff_attention(proj3, slopes4[:, 0], diff_lambda[l], diff_norm_g[l], lam_init)
        yr = _retention(proj3)
        yc = _hgrn2(proj3, lower_bounds[l], hgrn_norm_g[l])
        yd = _dilated_attention(proj3, slopes4)
        ys = [y.reshape(m, BRANCH_W) for y in (ya, yr, yc, yd)]
        merged = _merge(ys, w_branch, l, proj, d)
        z = _matmul(merged, w_o, l, out_dtype=F32, res=res, alpha=alpha, name="out_proj")
        hb, mean, rstd = _layernorm(z, ln1_g[l], ln1_b[l], final=False)
        res = (z, mean, rstd, ln1_g[l], ln1_b[l])
        act = _ffn_up(hb, w_up, l, conv_w[l], conv_b[l], s)
        z = _matmul(act, w_down_b, l, out_dtype=F32, tn=512, res=res, alpha=alpha, name="ffn_down")
        if l == depth - 1:
            (out,) = _layernorm(z, ln2_g[l], ln2_b[l], final=True)
            return out.reshape(b, s, d)
        hb, mean, rstd = _layernorm(z, ln2_g[l], ln2_b[l], final=False)
        res = (z, mean, rstd, ln2_g[l], ln2_b[l])
```
